```python
import jax, jax.numpy as jnp
from jax import lax
import numpy as np

D_MODEL = 2048
BATCH = 4
SEQ = 4096
DEPTH = 4

GRID_W = 64
CTX_LEN = 256
EPS = 1e-6

D_SGU = D_MODEL // 4
SGU_HEADS = 4
SGU_HD = D_SGU // SGU_HEADS
SGU_CHUNK = 128
D_MLSTM = D_MODEL // 2
MLSTM_HEADS = 4
MLSTM_HD = D_MLSTM // MLSTM_HEADS
MLSTM_CHUNK = 128
QK_CONV = 3
D_POOL = D_MODEL // 4
POOL_WINDOWS = (2, 4, 8, 16)
POOL_GC = D_POOL // len(POOL_WINDOWS)
N_GATES = 2 * 2 * MLSTM_HEADS

OFF_U = 0
OFF_V = OFF_U + D_SGU
OFF_P = OFF_V + D_SGU
OFF_Q = OFF_P + D_POOL
OFF_O = OFF_Q + D_MLSTM
OFF_K = OFF_O + D_MLSTM
OFF_VM = OFF_K + D_MLSTM
OFF_G = OFF_VM + D_MLSTM
D_IN = OFF_G + N_GATES

N_KEYS = 128
N_EXPERTS = N_KEYS * N_KEYS
PEER_HEADS = 8
PEER_TOPK = 16
PEER_DK = 256
PEER_BLOCK = 128

kernel_name = 'hybrid_sgu_mlstm_pool_peer_dit'


def rmsnorm(x, g):
    xf = x.astype(jnp.float32)
    y = xf * lax.rsqrt(jnp.mean(xf * xf, axis=-1, keepdims=True) + EPS)
    return (y * g.astype(jnp.float32)).astype(x.dtype)


def ada_modulation(cond, w, b):
    m = jax.nn.silu(cond) @ w + b
    return jnp.split(m[:, None, :], 6, axis=-1)


def modulate(h, shift, scale):
    return h * (1 + scale) + shift


def chunk_sgu(u, v, g, w_s, b_s):
    B, T, _ = u.shape
    v = rmsnorm(v, g).reshape(B, T // SGU_CHUNK, SGU_CHUNK, SGU_HEADS, SGU_HD)
    mixed = jnp.einsum('hts,bcshd->bcthd', w_s, v) + b_s.T[None, None, :, :, None]
    return u * mixed.reshape(B, T, D_SGU)


def centred_dwconv(x, w):
    K, ch = w.shape
    pad = K // 2
    return lax.conv_general_dilated(x, w.astype(x.dtype)[:, None, :], window_strides=(1,),
                                    padding=[(pad, K - 1 - pad)],
                                    dimension_numbers=('NWC', 'WIO', 'NWC'), feature_group_count=ch)


def to_heads(a):
    B, T, _ = a.shape
    return a.reshape(B, T, MLSTM_HEADS, MLSTM_HD).transpose(0, 2, 1, 3)


def mlstm_inputs(q_raw, k_raw, v_raw, g_raw, conv_w, b_gate):
    k = to_heads(jax.nn.silu(centred_dwconv(k_raw, conv_w[:, D_MLSTM:])))
    v = to_heads(v_raw)
    B, T, _ = g_raw.shape
    g = (g_raw + b_gate).astype(jnp.float32).reshape(B, T, 2, 2, MLSTM_HEADS)
    g = g.transpose(2, 3, 0, 4, 1)
    q = None if q_raw is None else to_heads(jax.nn.silu(centred_dwconv(q_raw, conv_w[:, :D_MLSTM])))
    return q, k, v, g


def to_chunks(a):
    B, H, T = a.shape[:3]
    a = a.reshape((B, H, T // MLSTM_CHUNK, MLSTM_CHUNK) + a.shape[3:])
    return jnp.moveaxis(a, 2, 0)


def zero_state(batch):
    return (jnp.zeros((batch, MLSTM_HEADS, MLSTM_HD, MLSTM_HD), jnp.float32),
            jnp.zeros((batch, MLSTM_HEADS, MLSTM_HD), jnp.float32),
            jnp.zeros((batch, MLSTM_HEADS), jnp.float32))


def mlstm_scan(q, k, v, ig, fg, state, with_outputs):
    L = MLSTM_CHUNK
    causal = jnp.tril(jnp.ones((L, L), bool))
    kf = k.astype(jnp.float32) * (MLSTM_HD ** -0.5)
    xs = [to_chunks(a) for a in (kf, v.astype(jnp.float32), ig, fg)]
    if with_outputs:
        xs.append(to_chunks(q.astype(jnp.float32)))

    def step(carry, inp):
        C, n, m = carry
        kc, vc, ic, fc = inp[:4]
        b = jnp.cumsum(jax.nn.log_sigmoid(fc), axis=-1)
        b_last = b[..., -1]
        g = b_last[..., None] - b + ic
        m_new = jnp.maximum(b_last + m, g.max(-1))
        wk = jnp.exp(g - m_new[..., None])
        decay = jnp.exp(b_last + m - m_new)
        C_new = decay[..., None, None] * C + jnp.einsum('bhsv,bhsk->bhvk', vc * wk[..., None], kc)
        n_new = decay[..., None] * n + jnp.einsum('bhs,bhsk->bhk', wk, kc)
        if not with_outputs:
            return (C_new, n_new, m_new), None
        qc = inp[4]
        log_w = jnp.where(causal, b[..., :, None] - b[..., None, :] + ic[..., None, :], -jnp.inf)
        inter = b + m[..., None]
        m_t = jnp.maximum(inter, log_w.max(-1))
        w_prev = jnp.exp(inter - m_t)
        s = jnp.einsum('bhtd,bhsd->bhts', qc, kc) * jnp.exp(log_w - m_t[..., None])
        num = jnp.einsum('bhts,bhsv->bhtv', s, vc) + w_prev[..., None] * jnp.einsum('bhvk,bhtk->bhtv', C, qc)
        den = s.sum(-1) + w_prev * jnp.einsum('bhk,bhtk->bht', n, qc)
        h = num / jnp.maximum(jnp.abs(den), jnp.exp(-m_t))[..., None]
        return (C_new, n_new, m_new), h

    state, hs = lax.scan(step, state, tuple(xs))
    if not with_outputs:
        return None, state
    B, H, T, Dh = v.shape
    return jnp.moveaxis(hs, 0, 2).reshape(B, H, T, Dh), state


def mlstm_bidir(q, k, v, g, states, with_outputs):
    flip = lambda a: None if a is None else jnp.flip(a, axis=2)
    h_f, st_f = mlstm_scan(q, k, v, g[0, 0], g[0, 1], states[0], with_outputs)
    h_b, st_b = mlstm_scan(flip(q), flip(k), flip(v), flip(g[1, 0]), flip(g[1, 1]), states[1], with_outputs)
    h = h_f + jnp.flip(h_b, axis=2) if with_outputs else None
    return h, (st_f, st_b)


def mlstm_out(h, o_raw, g_norm, dtype):
    B, H, T, Dh = h.shape
    hn = h * lax.rsqrt(jnp.mean(h * h, axis=-1, keepdims=True) + EPS)
    hn = hn.transpose(0, 2, 1, 3).reshape(B, T, D_MLSTM) * g_norm.astype(jnp.float32)
    return (jax.nn.sigmoid(o_raw.astype(jnp.float32)) * hn).astype(dtype)


def window_bounds(n, w):
    idx = jnp.arange(n)
    return jnp.clip(idx - w // 2, 0, n), jnp.clip(idx - w // 2 + w, 0, n)


def grid_window_mean(x, rows, cols, w):
    B, T, ch = x.shape
    s = jnp.cumsum(jnp.cumsum(x.astype(jnp.float32).reshape(B, rows, cols, ch), axis=1), axis=2)
    s = jnp.pad(s, ((0, 0), (1, 0), (1, 0), (0, 0)))
    rlo, rhi = window_bounds(rows, w)
    clo, chi = window_bounds(cols, w)
    box = lambda r, c: jnp.take(jnp.take(s, r, axis=1), c, axis=2)
    tot = box(rhi, chi) - box(rlo, chi) - box(rhi, clo) + box(rlo, clo)
    cnt = ((rhi - rlo)[:, None] * (chi - clo)[None, :]).astype(jnp.float32)
    return (tot / cnt[None, :, :, None]).reshape(B, T, ch)


def pool_mixer(x, rows, cols, w_g, scale):
    B, T, _ = x.shape
    xg = x.reshape(B, T, len(POOL_WINDOWS), POOL_GC)
    means = jnp.stack([grid_window_mean(xg[:, :, i], rows, cols, w) for i, w in enumerate(POOL_WINDOWS)], axis=2)
    d = (means - xg.astype(jnp.float32)).astype(x.dtype)
    y = jnp.einsum('btgc,gcd->btgd', d, w_g)
    return y.reshape(B, T, D_POOL) * scale


def stream_mixers(p, h_mlstm, rows, cols, sgu_norm, sgu_w, sgu_b, mlstm_norm, pool_w, pool_scale):
    u = jax.nn.gelu(p[..., OFF_U:OFF_V])
    v = jax.nn.gelu(p[..., OFF_V:OFF_P])
    y_a = chunk_sgu(u, v, sgu_norm, sgu_w, sgu_b)
    y_b = mlstm_out(h_mlstm, p[..., OFF_O:OFF_K], mlstm_norm, p.dtype)
    y_c = pool_mixer(p[..., OFF_P:OFF_Q], rows, cols, pool_w, pool_scale)
    return jnp.concatenate([y_a, y_b, y_c], axis=-1)


def peer_ffn(h, wq, keys, u_tab, v_tab):
    B, T, D = h.shape
    ht = h.reshape(B * T, D)
    n = B * T
    q = (ht @ wq).astype(jnp.float32).reshape(n, PEER_HEADS, 2, PEER_DK // 2)
    s = jnp.einsum('nhpc,pkc->nhpk', q, keys.astype(jnp.float32))
    s1, i1 = lax.top_k(s[:, :, 0], PEER_TOPK)
    s2, i2 = lax.top_k(s[:, :, 1], PEER_TOPK)
    cand = (s1[..., :, None] + s2[..., None, :]).reshape(n, PEER_HEADS, PEER_TOPK * PEER_TOPK)
    sc, ci = lax.top_k(cand, PEER_TOPK)
    e = (jnp.take_along_axis(i1, ci // PEER_TOPK, axis=-1) * N_KEYS
         + jnp.take_along_axis(i2, ci % PEER_TOPK, axis=-1))
    gate = jax.nn.softmax(sc, axis=-1)
    nb = n // PEER_BLOCK

    def block(args):
        xb, eb, gb = args
        a = jnp.einsum('nd,nhkd->nhk', xb, jnp.take(u_tab, eb, axis=0))
        w = (gb * jax.nn.gelu(a.astype(jnp.float32))).astype(xb.dtype)
        return jnp.einsum('nhk,nhkd->nd', w, jnp.take(v_tab, eb, axis=0))

    out = lax.map(block, (ht.reshape(nb, PEER_BLOCK, D),
                          e.reshape(nb, PEER_BLOCK, PEER_HEADS, PEER_TOPK),
                          gate.reshape(nb, PEER_BLOCK, PEER_HEADS, PEER_TOPK)))
    return out.reshape(B, T, D)


def setup_inputs(seed: int = 0) -> dict:
    key = jax.random.key(seed)
    ks = jax.random.split(key, 24)
    L, D = DEPTH, D_MODEL
    nrm = lambda k, shape, s: jax.random.normal(k, shape, jnp.float32) * s
    gate_base = jnp.array([0.0, 3.0], jnp.float32)[None, None, :, None]
    return {
        'x': nrm(ks[0], (BATCH, SEQ, D), 1.0),
        'c': nrm(ks[1], (BATCH, D), 1.0),
        'ctx': nrm(ks[2], (BATCH, CTX_LEN, D), 1.0),
        'c_ctx': nrm(ks[3], (D,), 1.0),
        'ada_w': nrm(ks[4], (L, D, 6 * D), 0.5 * D ** -0.5),
        'ada_b': nrm(ks[5], (L, 6 * D), 0.02),
        'norm_mix': 1.0 + nrm(ks[6], (L, D), 0.05),
        'norm_ffn': 1.0 + nrm(ks[7], (L, D), 0.05),
        'w_in': nrm(ks[8], (L, D, D_IN), D ** -0.5),
        'b_gate': (gate_base + nrm(ks[9], (L, 2, 2, MLSTM_HEADS), 0.1)).reshape(L, N_GATES),
        'sgu_norm': 1.0 + nrm(ks[10], (L, D_SGU), 0.05),
        'sgu_w': nrm(ks[11], (L, SGU_HEADS, SGU_CHUNK, SGU_CHUNK), SGU_CHUNK ** -0.5),
        'sgu_b': 1.0 + nrm(ks[12], (L, SGU_HEADS, SGU_CHUNK), 0.1),
        'qk_conv': nrm(ks[13], (L, QK_CONV, 2 * D_MLSTM), QK_CONV ** -0.5),
        'mlstm_norm': 1.0 + nrm(ks[14], (L, D_MLSTM), 0.05),
        'pool_w': nrm(ks[15], (L, len(POOL_WINDOWS), POOL_GC, POOL_GC), POOL_GC ** -0.5),
        'pool_scale': 1.0 + nrm(ks[16], (L, D_POOL), 0.1),
        'w_out': nrm(ks[17], (L, D, D), D ** -0.5),
        'peer_wq': nrm(ks[18], (L, D, PEER_HEADS * PEER_DK), D ** -0.5),
        'peer_keys': nrm(ks[19], (L, 2, N_KEYS, PEER_DK // 2), (PEER_DK // 2) ** -0.5),
        'peer_u': nrm(ks[20], (L, N_EXPERTS, D), D ** -0.5),
        'peer_v': nrm(ks[21], (L, N_EXPERTS, D), 0.25),
        'norm_final': 1.0 + nrm(ks[22], (D,), 0.05),
    }


def reference(x, c, ctx, c_ctx, ada_w, ada_b, norm_mix, norm_ffn, w_in, b_gate, sgu_norm, sgu_w, sgu_b,
              qk_conv, mlstm_norm, pool_w, pool_scale, w_out, peer_wq, peer_keys, peer_u, peer_v, norm_final):
    rows = x.shape[1] // GRID_W
    ctx_len = ctx.shape[1]
    x_lat, x_ctx = x, ctx
    for l in range(DEPTH):
        last = l == DEPTH - 1
        m_lat = ada_modulation(c, ada_w[l], ada_b[l])
        m_ctx = ada_modulation(c_ctx[None, :], ada_w[l], ada_b[l])
        mix_args = (sgu_norm[l], sgu_w[l], sgu_b[l], mlstm_norm[l], pool_w[l], pool_scale[l])

        hc = modulate(rmsnorm(x_ctx, norm_mix[l]), m_ctx[0], m_ctx[1])
        if last:
            pk = hc @ w_in[l][:, OFF_K:]
            _, kc, vc, gc = mlstm_inputs(None, pk[..., :D_MLSTM], pk[..., D_MLSTM:2 * D_MLSTM],
                                          pk[..., 2 * D_MLSTM:], qk_conv[l], b_gate[l])
            _, ctx_states = mlstm_bidir(None, kc, vc, gc, (zero_state(x.shape[0]), zero_state(x.shape[0])), False)
        else:
            pc = hc @ w_in[l]
            qc, kc, vc, gc = mlstm_inputs(pc[..., OFF_Q:OFF_O], pc[..., OFF_K:OFF_VM], pc[..., OFF_VM:OFF_G],
                                          pc[..., OFF_G:], qk_conv[l], b_gate[l])
            h_c, ctx_states = mlstm_bidir(qc, kc, vc, gc, (zero_state(x.shape[0]), zero_state(x.shape[0])), True)
            y_c = stream_mixers(pc, h_c, 1, ctx_len, *mix_args)
            x_ctx_new = x_ctx + m_ctx[2] * (y_c @ w_out[l])
            hf = modulate(rmsnorm(x_ctx_new, norm_ffn[l]), m_ctx[3], m_ctx[4])
            x_ctx_new = x_ctx_new + m_ctx[5] * peer_ffn(hf, peer_wq[l], peer_keys[l], peer_u[l], peer_v[l])

        hl = modulate(rmsnorm(x_lat, norm_mix[l]), m_lat[0], m_lat[1])
        pl = hl @ w_in[l]
        ql, kl, vl, gl = mlstm_inputs(pl[..., OFF_Q:OFF_O], pl[..., OFF_K:OFF_VM], pl[..., OFF_VM:OFF_G],
                                      pl[..., OFF_G:], qk_conv[l], b_gate[l])
        h_l, _ = mlstm_bidir(ql, kl, vl, gl, ctx_states, True)
        y_l = stream_mixers(pl, h_l, rows, GRID_W, *mix_args)
        x_lat = x_lat + m_lat[2] * (y_l @ w_out[l])
        hf = modulate(rmsnorm(x_lat, norm_ffn[l]), m_lat[3], m_lat[4])
        x_lat = x_lat + m_lat[5] * peer_ffn(hf, peer_wq[l], peer_keys[l], peer_u[l], peer_v[l])

        if not last:
            x_ctx = x_ctx_new
    return rmsnorm(x_lat, norm_final)
```

```python
import functools

import jax
import jax.numpy as jnp
from jax import lax
from jax.experimental import pallas as pl
from jax.experimental.pallas import tpu as pltpu

F32 = jnp.float32
BF16 = jnp.bfloat16
I32 = jnp.int32

EPS = 1e-6
GRID_W = 64
GRID_W_LOG2 = 6
D_MODEL = 2048
D_SGU = D_MODEL // 4
SGU_HEADS = 4
SGU_HD = D_SGU // SGU_HEADS
CHUNK = 128
D_MLSTM = D_MODEL // 2
MLSTM_HEADS = 4
MLSTM_HD = D_MLSTM // MLSTM_HEADS
D_POOL = D_MODEL // 4
POOL_GROUPS = 4
POOL_GC = D_POOL // POOL_GROUPS
N_GATES = 2 * 2 * MLSTM_HEADS
OFF_U = 0
OFF_V = OFF_U + D_SGU
OFF_P = OFF_V + D_SGU
OFF_Q = OFF_P + D_POOL
OFF_O = OFF_Q + D_MLSTM
OFF_K = OFF_O + D_MLSTM
OFF_VM = OFF_K + D_MLSTM
OFF_G = OFF_VM + D_MLSTM
N_KEYS = 128
PEER_HEADS = 8
PEER_TOPK = 16
PEER_DK = 256
PEER_PAIRS = PEER_HEADS * PEER_TOPK

LANE = 128
ROW_TILE = 256
GATHER_TOKENS = 8
VMEM_LIMIT = 56 * 1024 * 1024

CAND_PAIRS = [(a, b) for a in range(PEER_TOPK) for b in range(PEER_TOPK)
              if (a + 1) * (b + 1) <= PEER_TOPK]
CAND_ROWS = 56


def _params(sem, vmem=VMEM_LIMIT):
    return pltpu.CompilerParams(dimension_semantics=sem, vmem_limit_bytes=vmem)


def _norm_mod(x, g, shift, scale):
    ms = jnp.mean(x * x, axis=-1, keepdims=True)
    y = x * lax.rsqrt(ms + EPS)
    return (y * g) * (1.0 + scale) + shift


def _gelu(x):
    return jax.nn.gelu(x, approximate=True)


def _sigmoid(x):
    return 1.0 / (1.0 + jnp.exp(-x))


def _ada_kernel(cond_ref, w_ref, b_ref, o_ref):
    a = cond_ref[...]
    a = (a * _sigmoid(a)).astype(BF16)
    o_ref[0] = jnp.dot(a, w_ref[0].astype(BF16), preferred_element_type=F32) + b_ref[0]


def ada_modulation_all(cond, ada_w, ada_b):
    depth, d, d6 = ada_w.shape
    rows = cond.shape[0]
    tn = 1024
    return pl.pallas_call(
        _ada_kernel,
        grid=(depth, d6 // tn),
        in_specs=[pl.BlockSpec((rows, d), lambda l, j: (0, 0)),
                  pl.BlockSpec((1, d, tn), lambda l, j: (l, 0, j)),
                  pl.BlockSpec((1, 1, tn), lambda l, j: (l, 0, j))],
        out_specs=pl.BlockSpec((1, rows, tn), lambda l, j: (l, 0, j)),
        out_shape=jax.ShapeDtypeStruct((depth, rows, d6), F32),
        compiler_params=_params(("parallel", "parallel")),
    )(cond, ada_w, ada_b.reshape(depth, 1, d6))


def _in_proj_kernel(x_ref, mod_ref, g_ref, w_ref, wg_ref, p_ref, gate_ref, *, ncols):
    h = _norm_mod(x_ref[...], g_ref[...], mod_ref[0, 0:1, :], mod_ref[0, 1:2, :]).astype(BF16)
    for j in range(ncols // 256):
        sl = slice(j * 256, (j + 1) * 256)
        p_ref[:, sl] = jnp.dot(h, w_ref[:, sl], preferred_element_type=F32).astype(BF16)
    gate_ref[0] = jnp.dot(h, wg_ref[...], preferred_element_type=F32)


def in_proj(x, mod, norm_g, w_main, w_gate, mod_row):
    n, d = x.shape
    ncol_blocks = 2
    ncols = OFF_G // ncol_blocks
    return pl.pallas_call(
        functools.partial(_in_proj_kernel, ncols=ncols),
        grid=(ncol_blocks, n // ROW_TILE),
        in_specs=[pl.BlockSpec((ROW_TILE, d), lambda c, i: (i, 0)),
                  pl.BlockSpec((1, 6, d), lambda c, i: (mod_row(i), 0, 0)),
                  pl.BlockSpec((1, d), lambda c, i: (0, 0)),
                  pl.BlockSpec((d, ncols), lambda c, i: (0, c)),
                  pl.BlockSpec((d, LANE), lambda c, i: (0, 0))],
        out_specs=[pl.BlockSpec((ROW_TILE, ncols), lambda c, i: (i, c)),
                   pl.BlockSpec((1, ROW_TILE, LANE), lambda c, i: (c, i, 0))],
        out_shape=[jax.ShapeDtypeStruct((n, OFF_G), BF16),
                   jax.ShapeDtypeStruct((ncol_blocks, n, LANE), F32)],
        compiler_params=_params(("arbitrary", "arbitrary")),
    )(x, mod, norm_g, w_main, w_gate)


def _sgu_kernel(u_ref, v_ref, g_ref, ws_ref, bs_ref, y_ref):
    u = _gelu(u_ref[...].astype(F32))
    v = _gelu(v_ref[...].astype(F32))
    ms = jnp.mean(v * v, axis=-1, keepdims=True)
    v = (v * lax.rsqrt(ms + EPS) * g_ref[...]).astype(BF16)
    for c in range(ROW_TILE // CHUNK):
        rows = slice(c * CHUNK, (c + 1) * CHUNK)
        for h in range(SGU_HEADS):
            cols = slice(h * SGU_HD, (h + 1) * SGU_HD)
            mixed = jnp.dot(ws_ref[h], v[rows, cols], preferred_element_type=F32) + bs_ref[:, h:h + 1]
            y_ref[rows, cols] = (u[rows, cols] * mixed).astype(BF16)


def sgu_mixer(p, sgu_norm, sgu_w, sgu_bt):
    n = p.shape[0]
    return pl.pallas_call(
        _sgu_kernel,
        grid=(n // ROW_TILE,),
        in_specs=[pl.BlockSpec((ROW_TILE, D_SGU), lambda i: (i, OFF_U // D_SGU)),
                  pl.BlockSpec((ROW_TILE, D_SGU), lambda i: (i, OFF_V // D_SGU)),
                  pl.BlockSpec((1, D_SGU), lambda i: (0, 0)),
                  pl.BlockSpec((SGU_HEADS, CHUNK, CHUNK), lambda i: (0, 0, 0)),
                  pl.BlockSpec((CHUNK, SGU_HEADS), lambda i: (0, 0))],
        out_specs=pl.BlockSpec((ROW_TILE, D_SGU), lambda i: (i, 0)),
        out_shape=jax.ShapeDtypeStruct((n, D_SGU), BF16),
        compiler_params=_params(("parallel",)),
    )(p, p, sgu_norm, sgu_w, sgu_bt)


def _gate_prep_kernel(g_ref, bias_ref, o_ref):
    g = g_ref[...] + bias_ref[...]
    lf = jnp.minimum(g, 0.0) - jnp.log(1.0 + jnp.exp(-jnp.abs(g)))
    row = lax.broadcasted_iota(I32, (CHUNK, CHUNK), 0)
    col = lax.broadcasted_iota(I32, (CHUNK, CHUNK), 1)
    lower = (col <= row).astype(F32)
    upper = (col >= row).astype(F32)
    b_fwd = jnp.dot(lower, lf, preferred_element_type=F32, precision=lax.Precision.HIGHEST)
    b_bwd = jnp.dot(upper, lf, preferred_element_type=F32, precision=lax.Precision.HIGHEST)
    lane = lax.broadcasted_iota(I32, (1, LANE), 1)
    is_input_gate = (lane // MLSTM_HEADS) % 2 == 0
    is_fwd = lane < 2 * MLSTM_HEADS
    o_ref[...] = jnp.where(is_input_gate, g, jnp.where(is_fwd, b_fwd, b_bwd))


def gate_prep(graw, bias):
    n = graw.shape[0]
    return pl.pallas_call(
        _gate_prep_kernel,
        grid=(n // CHUNK,),
        in_specs=[pl.BlockSpec((CHUNK, LANE), lambda i: (i, 0)),
                  pl.BlockSpec((1, LANE), lambda i: (0, 0))],
        out_specs=pl.BlockSpec((CHUNK, LANE), lambda i: (i, 0)),
        out_shape=jax.ShapeDtypeStruct((n, LANE), F32),
        compiler_params=_params(("parallel",)),
    )(graw, bias)


HALO = 16


def _qk_conv_kernel(xm_ref, xp_ref, xn_ref, w_ref, o_ref, *, blocks_per_batch, ctx_blocks):
    i = pl.program_id(0)
    j = pl.program_id(1)
    x = xm_ref[...].astype(F32)
    tm = x.shape[0]
    prev_row = xp_ref[...].astype(F32)[HALO - 1:HALO, :]
    next_row = xn_ref[...].astype(F32)[0:1, :]
    ib = i % blocks_per_batch
    at_start = jnp.logical_or(ib == 0, ib == ctx_blocks)
    at_end = jnp.logical_or(ib == ctx_blocks - 1, ib == blocks_per_batch - 1)
    prev_row = jnp.where(at_start, 0.0, prev_row)
    next_row = jnp.where(at_end, 0.0, next_row)
    rows = lax.broadcasted_iota(I32, (tm, 1), 0)
    x_prev = jnp.where(rows == 0, prev_row, pltpu.roll(x, 1, 0))
    x_next = jnp.where(rows == tm - 1, next_row, pltpu.roll(x, tm - 1, 0))
    y = w_ref[0:1, :] * x_prev + w_ref[1:2, :] * x + w_ref[2:3, :] * x_next
    y = y * _sigmoid(y)
    y = y * jnp.where(j >= 2, MLSTM_HD ** -0.5, 1.0)
    o_ref[...] = y.astype(BF16)


def qk_conv(p, conv_w, blocks_per_batch, ctx_blocks):
    n = p.shape[0]
    tc = 512
    nhalo = n // HALO
    per = ROW_TILE // HALO

    def col(j):
        return jnp.where(j < 2, OFF_Q // tc + j, OFF_K // tc + j - 2)

    return pl.pallas_call(
        functools.partial(_qk_conv_kernel, blocks_per_batch=blocks_per_batch, ctx_blocks=ctx_blocks),
        grid=(n // ROW_TILE, 2 * D_MLSTM // tc),
        in_specs=[pl.BlockSpec((ROW_TILE, tc), lambda i, j: (i, col(j))),
                  pl.BlockSpec((HALO, tc), lambda i, j: (jnp.maximum(i * per - 1, 0), col(j))),
                  pl.BlockSpec((HALO, tc), lambda i, j: (jnp.minimum((i + 1) * per, nhalo - 1), col(j))),
                  pl.BlockSpec((3, tc), lambda i, j: (0, j))],
        out_specs=pl.BlockSpec((ROW_TILE, tc), lambda i, j: (i, j)),
        out_shape=jax.ShapeDtypeStruct((n, 2 * D_MLSTM), BF16),
        compiler_params=_params(("parallel", "parallel")),
    )(p, p, p, conv_w)


def _mlstm_kernel(q_ref, k_ref, v_ref, o_ref, gc_ref, gr_ref, nrm_ref, y_ref,
                  hf_ref, hb_ref, c_ref, n_ref, m_ref, *, nchunks, nctx):
    c_ref[...] = jnp.zeros_like(c_ref)
    n_ref[...] = jnp.zeros_like(n_ref)
    m_ref[...] = jnp.zeros_like(m_ref)
    row = lax.broadcasted_iota(I32, (CHUNK, CHUNK), 0)
    col = lax.broadcasted_iota(I32, (CHUNK, CHUNK), 1)
    seen = (row >= col, row <= col)
    h_refs = (hf_ref, hb_ref)

    def step(d, c):
        r0 = pl.multiple_of(c * CHUNK, CHUNK)
        rows = pl.ds(r0, CHUNK)
        q = q_ref[rows, :]
        k = k_ref[rows, :]
        v = v_ref[rows, :]
        gcol = gc_ref[0, rows, :]
        grow = gr_ref[0, c]
        ig_c, b_c = gcol[:, 2 * d:2 * d + 1], gcol[:, 2 * d + 1:2 * d + 2]
        ig_r, b_r = grow[2 * d:2 * d + 1, :], grow[2 * d + 1:2 * d + 2, :]
        b_last = b_c[CHUNK - 1:CHUNK, :] if d == 0 else b_c[0:1, :]
        m = m_ref[d, 0:1, 0:1]
        g_c = b_last - b_c + ig_c
        m_new = jnp.maximum(b_last + m, jnp.max(g_c, axis=0, keepdims=True))
        wk_c = jnp.exp(g_c - m_new)
        decay = jnp.exp(b_last + m - m_new)

        log_w = jnp.where(seen[d], b_c - b_r + ig_r, -jnp.inf)
        inter = b_c + m
        m_t = jnp.maximum(inter, jnp.max(log_w, axis=1, keepdims=True))
        w_prev = jnp.exp(inter - m_t)
        s = lax.dot_general(q, k, (((1,), (1,)), ((), ())), preferred_element_type=F32)
        s = s * jnp.exp(log_w - m_t)
        ct = c_ref[d]
        nvec = n_ref[d]
        num = (jnp.dot(s.astype(BF16), v, preferred_element_type=F32)
               + w_prev * jnp.dot(q, ct.astype(BF16), preferred_element_type=F32))
        qf = q.astype(F32)
        kf = k.astype(F32)
        den = jnp.sum(s, axis=1, keepdims=True) + w_prev * jnp.sum(qf * nvec, axis=1, keepdims=True)
        h_refs[d][rows, :] = num / jnp.maximum(jnp.abs(den), jnp.exp(-m_t))

        wv = (v.astype(F32) * wk_c).astype(BF16)
        c_ref[d] = decay * ct + lax.dot_general(k, wv, (((0,), (0,)), ((), ())),
                                                preferred_element_type=F32)
        n_ref[d] = decay * nvec + jnp.sum(kf * wk_c, axis=0, keepdims=True)
        m_ref[d] = jnp.broadcast_to(m_new, (1, LANE))

    def body(i, carry):
        step(0, i)
        step(1, jnp.where(i < nctx, nctx - 1 - i, nchunks - 1 - (i - nctx)))
        return carry

    lax.fori_loop(0, nchunks, body, 0)

    def finish(c, carry):
        rows = pl.ds(pl.multiple_of(c * CHUNK, CHUNK), CHUNK)
        h = hf_ref[rows, :] + hb_ref[rows, :]
        hn = h * lax.rsqrt(jnp.mean(h * h, axis=-1, keepdims=True) + EPS) * nrm_ref[...]
        y_ref[rows, :] = (_sigmoid(o_ref[rows, :].astype(F32)) * hn).astype(BF16)
        return carry

    lax.fori_loop(0, nchunks, finish, 0)


def mlstm_mixer(p, qk, gcol, grow, mlstm_norm, batch, tb, nctx):
    n = p.shape[0]
    nchunks = tb // CHUNK
    hd = MLSTM_HD
    return pl.pallas_call(
        functools.partial(_mlstm_kernel, nchunks=nchunks, nctx=nctx),
        grid=(batch, MLSTM_HEADS),
        in_specs=[pl.BlockSpec((tb, hd), lambda b, h: (b, h)),
                  pl.BlockSpec((tb, hd), lambda b, h: (b, MLSTM_HEADS + h)),
                  pl.BlockSpec((tb, hd), lambda b, h: (b, OFF_VM // hd + h)),
                  pl.BlockSpec((tb, hd), lambda b, h: (b, OFF_O // hd + h)),
                  pl.BlockSpec((1, tb, 4), lambda b, h: (h, b, 0)),
                  pl.BlockSpec((1, nchunks, 4, CHUNK), lambda b, h: (h, b, 0, 0)),
                  pl.BlockSpec((1, hd), lambda b, h: (0, h))],
        out_specs=pl.BlockSpec((tb, hd), lambda b, h: (b, h)),
        out_shape=jax.ShapeDtypeStruct((n, D_MLSTM), BF16),
        scratch_shapes=[pltpu.VMEM((tb, hd), F32), pltpu.VMEM((tb, hd), F32),
                        pltpu.VMEM((2, hd, hd), F32), pltpu.VMEM((2, 1, hd), F32),
                        pltpu.VMEM((2, 1, LANE), F32)],
        compiler_params=_params(("parallel", "parallel")),
    )(qk, qk, p, p, gcol, grow, mlstm_norm)


def _pool_kernel(xfull_ref, xrow_ref, w_ref, sc_ref, y_ref, m_ref, inv_ref, *, tb, ctx, seq):
    g = pl.program_id(0)
    i = pl.program_id(1)
    b = pl.program_id(2)
    tm = xrow_ref.shape[0]
    half = jnp.left_shift(1, g)
    win = 2 * half

    def grid_coords(pos):
        is_ctx = pos < ctx
        lat = pos - ctx
        r = jnp.where(is_ctx, 0, jnp.right_shift(lat, GRID_W_LOG2))
        c = jnp.where(is_ctx, pos, jnp.bitwise_and(lat, GRID_W - 1))
        return is_ctx, r, c

    @pl.when(b == 0)
    def _build_window_matrix():
        t = i * tm + lax.broadcasted_iota(I32, (tm, 1), 0)
        s = lax.broadcasted_iota(I32, (1, tb), 1)
        ctx_t, r_t, c_t = grid_coords(t)
        ctx_s, r_s, c_s = grid_coords(s)
        dr = r_s - r_t + half
        dc = c_s - c_t + half
        inside = (ctx_t == ctx_s) & (dr >= 0) & (dr < win) & (dc >= 0) & (dc < win)
        m_ref[...] = jnp.where(inside, 1.0, 0.0).astype(BF16)
        nrows = jnp.where(ctx_t, 1, seq // GRID_W)
        ncols = jnp.where(ctx_t, ctx, GRID_W)
        cnt_r = jnp.minimum(r_t - half + win, nrows) - jnp.maximum(r_t - half, 0)
        cnt_c = jnp.minimum(c_t - half + win, ncols) - jnp.maximum(c_t - half, 0)
        inv_ref[...] = 1.0 / (cnt_r * cnt_c).astype(F32)

    tot = jnp.dot(m_ref[...], xfull_ref[...], preferred_element_type=F32)
    d = (tot * inv_ref[...] - xrow_ref[...].astype(F32)).astype(BF16)
    y = jnp.dot(d, w_ref[0], preferred_element_type=F32) * sc_ref[0]
    y_ref[...] = y.astype(BF16)


def pool_mixer(p, pool_w, pool_scale, batch, tb, ctx, seq):
    n = p.shape[0]
    nrb = tb // ROW_TILE
    c0 = OFF_P // POOL_GC
    return pl.pallas_call(
        functools.partial(_pool_kernel, tb=tb, ctx=ctx, seq=seq),
        grid=(POOL_GROUPS, nrb, batch),
        in_specs=[pl.BlockSpec((tb, POOL_GC), lambda g, i, b: (b, c0 + g)),
                  pl.BlockSpec((ROW_TILE, POOL_GC), lambda g, i, b: (b * nrb + i, c0 + g)),
                  pl.BlockSpec((1, POOL_GC, POOL_GC), lambda g, i, b: (g, 0, 0)),
                  pl.BlockSpec((1, 1, POOL_GC), lambda g, i, b: (g, 0, 0))],
        out_specs=pl.BlockSpec((ROW_TILE, POOL_GC), lambda g, i, b: (b * nrb + i, g)),
        out_shape=jax.ShapeDtypeStruct((n, D_POOL), BF16),
        scratch_shapes=[pltpu.VMEM((ROW_TILE, tb), BF16), pltpu.VMEM((ROW_TILE, 1), F32)],
        compiler_params=_params(("arbitrary", "arbitrary", "arbitrary")),
    )(p, p, pool_w, pool_scale)


def _out_proj_kernel(x_ref, ya_ref, yb_ref, yc_ref, mod_ref, w_ref, o_ref):
    ya, yb, yc = ya_ref[...], yb_ref[...], yc_ref[...]
    k1 = D_SGU
    k2 = D_SGU + D_MLSTM
    for j in range(D_MODEL // 256):
        sl = slice(j * 256, (j + 1) * 256)
        acc = jnp.dot(ya, w_ref[0:k1, sl], preferred_element_type=F32)
        acc += jnp.dot(yb, w_ref[k1:k2, sl], preferred_element_type=F32)
        acc += jnp.dot(yc, w_ref[k2:D_MODEL, sl], preferred_element_type=F32)
        o_ref[:, sl] = x_ref[:, sl] + mod_ref[0, 2:3, sl] * acc


def out_proj(x, ya, yb, yc, mod, w_out, mod_row):
    n, d = x.shape
    return pl.pallas_call(
        _out_proj_kernel,
        grid=(n // ROW_TILE,),
        in_specs=[pl.BlockSpec((ROW_TILE, d), lambda i: (i, 0)),
                  pl.BlockSpec((ROW_TILE, D_SGU), lambda i: (i, 0)),
                  pl.BlockSpec((ROW_TILE, D_MLSTM), lambda i: (i, 0)),
                  pl.BlockSpec((ROW_TILE, D_POOL), lambda i: (i, 0)),
                  pl.BlockSpec((1, 6, d), lambda i: (mod_row(i), 0, 0)),
                  pl.BlockSpec((d, d), lambda i: (0, 0))],
        out_specs=pl.BlockSpec((ROW_TILE, d), lambda i: (i, 0)),
        out_shape=jax.ShapeDtypeStruct((n, d), F32),
        compiler_params=_params(("parallel",)),
    )(x, ya, yb, yc, mod, w_out)


def _topk_rows(vals, k):
    nrows = vals.shape[0]
    iota = lax.broadcasted_iota(I32, vals.shape, 0)
    out_v, out_i = [], []
    for _ in range(k):
        mx = jnp.max(vals, axis=0, keepdims=True)
        am = jnp.min(jnp.where(vals == mx, iota, nrows), axis=0, keepdims=True)
        out_v.append(mx)
        out_i.append(am)
        vals = jnp.where(iota == am, -jnp.inf, vals)
    return out_v, out_i


def _peer_route_kernel(x_ref, mod_ref, g_ref, wq_ref, keys_ref, hf_ref, e_ref, gate_ref,
                       hb_ref, cand_ref, cid_ref):
    h = pl.program_id(1)

    @pl.when(h == 0)
    def _norm():
        hf = _norm_mod(x_ref[...], g_ref[...], mod_ref[0, 3:4, :], mod_ref[0, 4:5, :])
        hf_ref[...] = hf
        hb_ref[...] = hf.astype(BF16)

    q = jnp.dot(hb_ref[...], wq_ref[...], preferred_element_type=F32).astype(BF16)
    half = PEER_DK // 2
    nt = (((1,), (1,)), ((), ()))
    s1, i1 = _topk_rows(lax.dot_general(keys_ref[0], q[:, :half], nt, preferred_element_type=F32),
                        PEER_TOPK)
    s2, i2 = _topk_rows(lax.dot_general(keys_ref[1], q[:, half:], nt, preferred_element_type=F32),
                        PEER_TOPK)
    tt = q.shape[0]
    for r, (a, b) in enumerate(CAND_PAIRS):
        cand_ref[r:r + 1, :] = s1[a] + s2[b]
        cid_ref[r:r + 1, :] = i1[a] * N_KEYS + i2[b]
    npad = CAND_ROWS - len(CAND_PAIRS)
    cand_ref[len(CAND_PAIRS):, :] = jnp.full((npad, tt), -jnp.inf, F32)
    cid_ref[len(CAND_PAIRS):, :] = jnp.zeros((npad, tt), I32)
    sc, slot = _topk_rows(cand_ref[...], PEER_TOPK)
    cid = cid_ref[...]
    riota = lax.broadcasted_iota(I32, cid.shape, 0)
    ex = [jnp.exp(v - sc[0]) for v in sc]
    tot = ex[0]
    for v in ex[1:]:
        tot = tot + v
    for j in range(PEER_TOPK):
        e_ref[0, j:j + 1, :] = jnp.sum(jnp.where(riota == slot[j], cid, 0), axis=0, keepdims=True)
        gate_ref[0, j:j + 1, :] = ex[j] / tot


def peer_route(x, mod, norm_g, wq, keys, mod_row):
    n, d = x.shape
    tt = ROW_TILE
    return pl.pallas_call(
        _peer_route_kernel,
        grid=(n // tt, PEER_HEADS),
        in_specs=[pl.BlockSpec((tt, d), lambda i, h: (i, 0)),
                  pl.BlockSpec((1, 6, d), lambda i, h: (mod_row(i), 0, 0)),
                  pl.BlockSpec((1, d), lambda i, h: (0, 0)),
                  pl.BlockSpec((d, PEER_DK), lambda i, h: (0, h)),
                  pl.BlockSpec((2, N_KEYS, PEER_DK // 2), lambda i, h: (0, 0, 0))],
        out_specs=[pl.BlockSpec((tt, d), lambda i, h: (i, 0)),
                   pl.BlockSpec((1, PEER_TOPK, tt), lambda i, h: (h, 0, i)),
                   pl.BlockSpec((1, PEER_TOPK, tt), lambda i, h: (h, 0, i))],
        out_shape=[jax.ShapeDtypeStruct((n, d), F32),
                   jax.ShapeDtypeStruct((PEER_HEADS, PEER_TOPK, n), I32),
                   jax.ShapeDtypeStruct((PEER_HEADS, PEER_TOPK, n), F32)],
        scratch_shapes=[pltpu.VMEM((tt, d), BF16), pltpu.VMEM((CAND_ROWS, tt), F32),
                        pltpu.VMEM((CAND_ROWS, tt), I32)],
        compiler_params=_params(("parallel", "arbitrary")),
    )(x, mod, norm_g, wq, keys)


def _peer_gather_kernel(ids_ref, idp_ref, hf_ref, gt_ref, xn_ref, mod_ref, u_hbm, v_hbm, o_ref,
                        ubuf, vbuf, sem, *, ngroups):
    i = pl.program_id(0)
    slot = i % 2
    nrows = GATHER_TOKENS * PEER_PAIRS

    def row_copy(tab, buf, which, s, e, r):
        return pltpu.make_async_copy(tab.at[pl.ds(e, 1), :], buf.at[s, pl.ds(r, 1), :], sem.at[which, s])

    @pl.when(i < ngroups)
    def _issue():
        def body(r, carry):
            e = ids_ref[0, 0, r]
            row_copy(u_hbm, ubuf, 0, slot, e, r).start()
            row_copy(v_hbm, vbuf, 1, slot, e, r).start()
            return carry
        lax.fori_loop(0, nrows, body, 0, unroll=8)

    @pl.when(i > 0)
    def _compute():
        ps = 1 - slot

        def wait_body(r, carry):
            e = idp_ref[0, 0, r]
            row_copy(u_hbm, ubuf, 0, ps, e, r).wait()
            row_copy(v_hbm, vbuf, 1, ps, e, r).wait()
            return carry
        lax.fori_loop(0, nrows, wait_body, 0, unroll=8)

        nl = D_MODEL // LANE
        for t in range(GATHER_TOKENS):
            rows = slice(t * PEER_PAIRS, (t + 1) * PEER_PAIRS)
            acc = jnp.zeros((PEER_PAIRS, LANE), F32)
            for c in range(nl):
                cols = slice(c * LANE, (c + 1) * LANE)
                acc = acc + ubuf[ps, rows, cols] * hf_ref[t:t + 1, cols]
            a = jnp.sum(acc, axis=1, keepdims=True)
            w = jnp.broadcast_to(gt_ref[0][:, t:t + 1] * _gelu(a), (PEER_PAIRS, LANE))
            for c in range(nl):
                cols = slice(c * LANE, (c + 1) * LANE)
                oc = jnp.sum(vbuf[ps, rows, cols] * w, axis=0, keepdims=True)
                o_ref[t:t + 1, cols] = xn_ref[t:t + 1, cols] + mod_ref[0, 5:6, cols] * oc


def peer_gather(ids, hf, gates_t, xn, mod, u_tab, v_tab, mod_row_group):
    n, d = xn.shape
    gt = GATHER_TOKENS
    ngroups = n // gt
    nrows = gt * PEER_PAIRS

    def prev(i):
        return jnp.maximum(i - 1, 0)

    return pl.pallas_call(
        functools.partial(_peer_gather_kernel, ngroups=ngroups),
        grid=(ngroups + 1,),
        in_specs=[pl.BlockSpec((1, 1, nrows), lambda i: (jnp.minimum(i, ngroups - 1), 0, 0),
                               memory_space=pltpu.SMEM),
                  pl.BlockSpec((1, 1, nrows), lambda i: (prev(i), 0, 0), memory_space=pltpu.SMEM),
                  pl.BlockSpec((gt, d), lambda i: (prev(i), 0)),
                  pl.BlockSpec((1, PEER_PAIRS, gt), lambda i: (prev(i), 0, 0)),
                  pl.BlockSpec((gt, d), lambda i: (prev(i), 0)),
                  pl.BlockSpec((1, 6, d), lambda i: (mod_row_group(prev(i)), 0, 0)),
                  pl.BlockSpec(memory_space=pl.ANY),
                  pl.BlockSpec(memory_space=pl.ANY)],
        out_specs=pl.BlockSpec((gt, d), lambda i: (prev(i), 0)),
        out_shape=jax.ShapeDtypeStruct((n, d), F32),
        scratch_shapes=[pltpu.VMEM((2, nrows, d), F32), pltpu.VMEM((2, nrows, d), F32),
                        pltpu.SemaphoreType.DMA((2, 2))],
        compiler_params=_params(("arbitrary",)),
    )(ids, ids, hf, gates_t, xn, mod, u_tab, v_tab)


def _final_norm_kernel(x_ref, g_ref, o_ref):
    x = x_ref[...]
    o_ref[0] = x * lax.rsqrt(jnp.mean(x * x, axis=-1, keepdims=True) + EPS) * g_ref[...]


def final_norm(x, g, batch, seq, ctx):
    n, d = x.shape
    tb = ctx + seq
    nrb = tb // ROW_TILE
    cb = ctx // ROW_TILE
    return pl.pallas_call(
        _final_norm_kernel,
        grid=(batch, seq // ROW_TILE),
        in_specs=[pl.BlockSpec((ROW_TILE, d), lambda b, i: (b * nrb + cb + i, 0)),
                  pl.BlockSpec((1, d), lambda b, i: (0, 0))],
        out_specs=pl.BlockSpec((1, ROW_TILE, d), lambda b, i: (b, i, 0)),
        out_shape=jax.ShapeDtypeStruct((batch, seq, d), F32),
        compiler_params=_params(("parallel", "parallel")),
    )(x, g)


def kernel(x, c, ctx, c_ctx, ada_w, ada_b, norm_mix, norm_ffn, w_in, b_gate, sgu_norm, sgu_w, sgu_b,
           qk_conv_w, mlstm_norm, pool_w, pool_scale, w_out, peer_wq, peer_keys, peer_u, peer_v,
           norm_final):
    batch, seq, d = x.shape
    ctx_len = ctx.shape[1]
    depth = ada_w.shape[0]
    tb = ctx_len + seq
    n = batch * tb
    assert d == D_MODEL and w_in.shape[2] == OFF_G + N_GATES
    assert ctx_len % ROW_TILE == 0 and seq % ROW_TILE == 0 and seq % GRID_W == 0
    assert n % GATHER_TOKENS == 0 and ROW_TILE % GATHER_TOKENS == 0
    blocks_per_batch = tb // ROW_TILE
    ctx_blocks = ctx_len // ROW_TILE
    nchunks = tb // CHUNK

    def mod_row(i):
        return jnp.where(i % blocks_per_batch < ctx_blocks, batch, i // blocks_per_batch)

    def mod_row_group(i):
        return mod_row(i // (ROW_TILE // GATHER_TOKENS))

    cond_rows = -(-(batch + 1) // 8) * 8
    cond = jnp.zeros((cond_rows, d), F32).at[:batch].set(c).at[batch].set(c_ctx)
    mods = ada_modulation_all(cond, ada_w, ada_b).reshape(depth, cond_rows, 6, d)

    xs = jnp.concatenate([ctx, x], axis=1).reshape(n, d)
    lane_pad = LANE - N_GATES
    for l in range(depth):
        mod = mods[l]
        w_main = w_in[l, :, :OFF_G].astype(BF16)
        w_gate = jnp.pad(w_in[l, :, OFF_G:], ((0, 0), (0, lane_pad))).astype(BF16)
        p, graw = in_proj(xs, mod, norm_mix[l][None], w_main, w_gate, mod_row)

        y_a = sgu_mixer(p, sgu_norm[l][None], sgu_w[l].astype(BF16), sgu_b[l].T)

        gp = gate_prep(graw[0], jnp.pad(b_gate[l], (0, lane_pad))[None])
        gp = gp[:, :N_GATES].reshape(n, 2, 2, MLSTM_HEADS)
        gcol = gp.transpose(3, 0, 1, 2).reshape(MLSTM_HEADS, n, 4)
        grow = gcol.reshape(MLSTM_HEADS, n // CHUNK, CHUNK, 4).transpose(0, 1, 3, 2)
        qk = qk_conv(p, qk_conv_w[l], blocks_per_batch, ctx_blocks)
        y_b = mlstm_mixer(p, qk, gcol, grow, mlstm_norm[l][None], batch, tb, ctx_len // CHUNK)

        y_c = pool_mixer(p, pool_w[l].astype(BF16), pool_scale[l].reshape(POOL_GROUPS, 1, POOL_GC),
                         batch, tb, ctx_len, seq)

        xn = out_proj(xs, y_a, y_b, y_c, mod, w_out[l].astype(BF16), mod_row)

        hf, e, gates = peer_route(xn, mod, norm_ffn[l][None], peer_wq[l].astype(BF16),
                                  peer_keys[l].astype(BF16), mod_row)
        ids = e.reshape(PEER_PAIRS, n).T.reshape(n // GATHER_TOKENS, 1, GATHER_TOKENS * PEER_PAIRS)
        gates_t = gates.reshape(PEER_PAIRS, n // GATHER_TOKENS, GATHER_TOKENS).transpose(1, 0, 2)
        xs = peer_gather(ids, hf, gates_t, xn, mod, peer_u[l], peer_v[l], mod_row_group)

    return final_norm(xs, norm_final[None], batch, seq, ctx_len)
```

```python
import functools

import jax
import jax.numpy as jnp
from jax import lax
from jax.experimental import pallas as pl
from jax.experimental.pallas import tpu as pltpu

F32 = jnp.float32
BF16 = jnp.bfloat16
I32 = jnp.int32

EPS = 1e-6
GRID_W = 64
GRID_W_LOG2 = 6
D_MODEL = 2048
D_SGU = D_MODEL // 4
SGU_HEADS = 4
SGU_HD = D_SGU // SGU_HEADS
CHUNK = 128
D_MLSTM = D_MODEL // 2
MLSTM_HEADS = 4
MLSTM_HD = D_MLSTM // MLSTM_HEADS
D_POOL = D_MODEL // 4
POOL_GROUPS = 4
POOL_GC = D_POOL // POOL_GROUPS
N_GATES = 2 * 2 * MLSTM_HEADS
OFF_U = 0
OFF_V = OFF_U + D_SGU
OFF_P = OFF_V + D_SGU
OFF_Q = OFF_P + D_POOL
OFF_O = OFF_Q + D_MLSTM
OFF_K = OFF_O + D_MLSTM
OFF_VM = OFF_K + D_MLSTM
OFF_G = OFF_VM + D_MLSTM
N_KEYS = 128
PEER_HEADS = 8
PEER_TOPK = 16
PEER_DK = 256
PEER_PAIRS = PEER_HEADS * PEER_TOPK

LANE = 128
ROW_TILE = 256
GATHER_TOKENS = 8
VMEM_LIMIT = 56 * 1024 * 1024

CAND_PAIRS = [(a, b) for a in range(PEER_TOPK) for b in range(PEER_TOPK)
              if (a + 1) * (b + 1) <= PEER_TOPK]
CAND_ROWS = 56


def _params(sem, vmem=VMEM_LIMIT):
    return pltpu.CompilerParams(dimension_semantics=sem, vmem_limit_bytes=vmem)


def _norm_mod(x, g, shift, scale):
    ms = jnp.mean(x * x, axis=-1, keepdims=True)
    y = x * lax.rsqrt(ms + EPS)
    return (y * g) * (1.0 + scale) + shift


def _gelu(x):
    return jax.nn.gelu(x, approximate=True)


def _sigmoid(x):
    return 1.0 / (1.0 + jnp.exp(-x))


def _ada_kernel(cond_ref, w_ref, b_ref, o_ref):
    a = cond_ref[...]
    a = (a * _sigmoid(a)).astype(BF16)
    o_ref[0] = jnp.dot(a, w_ref[0].astype(BF16), preferred_element_type=F32) + b_ref[0]


def ada_modulation_all(cond, ada_w, ada_b):
    depth, d, d6 = ada_w.shape
    rows = cond.shape[0]
    tn = 1024
    return pl.pallas_call(
        _ada_kernel,
        grid=(depth, d6 // tn),
        in_specs=[pl.BlockSpec((rows, d), lambda l, j: (0, 0)),
                  pl.BlockSpec((1, d, tn), lambda l, j: (l, 0, j)),
                  pl.BlockSpec((1, 1, tn), lambda l, j: (l, 0, j))],
        out_specs=pl.BlockSpec((1, rows, tn), lambda l, j: (l, 0, j)),
        out_shape=jax.ShapeDtypeStruct((depth, rows, d6), F32),
        compiler_params=_params(("parallel", "parallel")),
    )(cond, ada_w, ada_b.reshape(depth, 1, d6))


def _in_proj_kernel(x_ref, mod_ref, g_ref, w_ref, wg_ref, p_ref, gate_ref, *, ncols):
    h = _norm_mod(x_ref[...], g_ref[...], mod_ref[0, 0:1, :], mod_ref[0, 1:2, :]).astype(BF16)
    for j in range(ncols // 256):
        sl = slice(j * 256, (j + 1) * 256)
        p_ref[:, sl] = jnp.dot(h, w_ref[:, sl], preferred_element_type=F32).astype(BF16)
    gate_ref[0] = jnp.dot(h, wg_ref[...], preferred_element_type=F32)


def in_proj(x, mod, norm_g, w_main, w_gate, mod_row):
    n, d = x.shape
    ncol_blocks = 2
    ncols = OFF_G // ncol_blocks
    return pl.pallas_call(
        functools.partial(_in_proj_kernel, ncols=ncols),
        grid=(ncol_blocks, n // ROW_TILE),
        in_specs=[pl.BlockSpec((ROW_TILE, d), lambda c, i: (i, 0)),
                  pl.BlockSpec((1, 6, d), lambda c, i: (mod_row(i), 0, 0)),
                  pl.BlockSpec((1, d), lambda c, i: (0, 0)),
                  pl.BlockSpec((d, ncols), lambda c, i: (0, c)),
                  pl.BlockSpec((d, LANE), lambda c, i: (0, 0))],
        out_specs=[pl.BlockSpec((ROW_TILE, ncols), lambda c, i: (i, c)),
                   pl.BlockSpec((1, ROW_TILE, LANE), lambda c, i: (c, i, 0))],
        out_shape=[jax.ShapeDtypeStruct((n, OFF_G), BF16),
                   jax.ShapeDtypeStruct((ncol_blocks, n, LANE), F32)],
        compiler_params=_params(("arbitrary", "arbitrary")),
    )(x, mod, norm_g, w_main, w_gate)


def _sgu_kernel(u_ref, v_ref, g_ref, ws_ref, bs_ref, y_ref):
    u = _gelu(u_ref[...].astype(F32))
    v = _gelu(v_ref[...].astype(F32))
    ms = jnp.mean(v * v, axis=-1, keepdims=True)
    v = (v * lax.rsqrt(ms + EPS) * g_ref[...]).astype(BF16)
    for c in range(ROW_TILE // CHUNK):
        rows = slice(c * CHUNK, (c + 1) * CHUNK)
        for h in range(SGU_HEADS):
            cols = slice(h * SGU_HD, (h + 1) * SGU_HD)
            mixed = jnp.dot(ws_ref[h], v[rows, cols], preferred_element_type=F32) + bs_ref[:, h:h + 1]
            y_ref[rows, cols] = (u[rows, cols] * mixed).astype(BF16)


def sgu_mixer(p, sgu_norm, sgu_w, sgu_bt):
    n = p.shape[0]
    return pl.pallas_call(
        _sgu_kernel,
        grid=(n // ROW_TILE,),
        in_specs=[pl.BlockSpec((ROW_TILE, D_SGU), lambda i: (i, OFF_U // D_SGU)),
                  pl.BlockSpec((ROW_TILE, D_SGU), lambda i: (i, OFF_V // D_SGU)),
                  pl.BlockSpec((1, D_SGU), lambda i: (0, 0)),
                  pl.BlockSpec((SGU_HEADS, CHUNK, CHUNK), lambda i: (0, 0, 0)),
                  pl.BlockSpec((CHUNK, SGU_HEADS), lambda i: (0, 0))],
        out_specs=pl.BlockSpec((ROW_TILE, D_SGU), lambda i: (i, 0)),
        out_shape=jax.ShapeDtypeStruct((n, D_SGU), BF16),
        compiler_params=_params(("parallel",)),
    )(p, p, sgu_norm, sgu_w, sgu_bt)


def _gate_prep_kernel(g_ref, bias_ref, o_ref):
    g = g_ref[...] + bias_ref[...]
    lf = jnp.minimum(g, 0.0) - jnp.log(1.0 + jnp.exp(-jnp.abs(g)))
    row = lax.broadcasted_iota(I32, (CHUNK, CHUNK), 0)
    col = lax.broadcasted_iota(I32, (CHUNK, CHUNK), 1)
    lower = (col <= row).astype(F32)
    upper = (col >= row).astype(F32)
    b_fwd = jnp.dot(lower, lf, preferred_element_type=F32, precision=lax.Precision.HIGHEST)
    b_bwd = jnp.dot(upper, lf, preferred_element_type=F32, precision=lax.Precision.HIGHEST)
    lane = lax.broadcasted_iota(I32, (1, LANE), 1)
    is_input_gate = (lane // MLSTM_HEADS) % 2 == 0
    is_fwd = lane < 2 * MLSTM_HEADS
    o_ref[...] = jnp.where(is_input_gate, g, jnp.where(is_fwd, b_fwd, b_bwd))


def gate_prep(graw, bias):
    n = graw.shape[0]
    return pl.pallas_call(
        _gate_prep_kernel,
        grid=(n // CHUNK,),
        in_specs=[pl.BlockSpec((CHUNK, LANE), lambda i: (i, 0)),
                  pl.BlockSpec((1, LANE), lambda i: (0, 0))],
        out_specs=pl.BlockSpec((CHUNK, LANE), lambda i: (i, 0)),
        out_shape=jax.ShapeDtypeStruct((n, LANE), F32),
        compiler_params=_params(("parallel",)),
    )(graw, bias)


HALO = 16


def _qk_conv_kernel(xm_ref, xp_ref, xn_ref, w_ref, o_ref, *, blocks_per_batch, ctx_blocks):
    i = pl.program_id(0)
    j = pl.program_id(1)
    x = xm_ref[...].astype(F32)
    tm = x.shape[0]
    prev_row = xp_ref[...].astype(F32)[HALO - 1:HALO, :]
    next_row = xn_ref[...].astype(F32)[0:1, :]
    ib = i % blocks_per_batch
    at_start = jnp.logical_or(ib == 0, ib == ctx_blocks)
    at_end = jnp.logical_or(ib == ctx_blocks - 1, ib == blocks_per_batch - 1)
    prev_row = jnp.where(at_start, 0.0, prev_row)
    next_row = jnp.where(at_end, 0.0, next_row)
    rows = lax.broadcasted_iota(I32, (tm, 1), 0)
    x_prev = jnp.where(rows == 0, prev_row, pltpu.roll(x, 1, 0))
    x_next = jnp.where(rows == tm - 1, next_row, pltpu.roll(x, tm - 1, 0))
    y = w_ref[0:1, :] * x_prev + w_ref[1:2, :] * x + w_ref[2:3, :] * x_next
    y = y * _sigmoid(y)
    y = y * jnp.where(j >= 2, MLSTM_HD ** -0.5, 1.0)
    o_ref[...] = y.astype(BF16)


def qk_conv(p, conv_w, blocks_per_batch, ctx_blocks):
    n = p.shape[0]
    tc = 512
    nhalo = n // HALO
    per = ROW_TILE // HALO

    def col(j):
        return jnp.where(j < 2, OFF_Q // tc + j, OFF_K // tc + j - 2)

    return pl.pallas_call(
        functools.partial(_qk_conv_kernel, blocks_per_batch=blocks_per_batch, ctx_blocks=ctx_blocks),
        grid=(n // ROW_TILE, 2 * D_MLSTM // tc),
        in_specs=[pl.BlockSpec((ROW_TILE, tc), lambda i, j: (i, col(j))),
                  pl.BlockSpec((HALO, tc), lambda i, j: (jnp.maximum(i * per - 1, 0), col(j))),
                  pl.BlockSpec((HALO, tc), lambda i, j: (jnp.minimum((i + 1) * per, nhalo - 1), col(j))),
                  pl.BlockSpec((3, tc), lambda i, j: (0, j))],
        out_specs=pl.BlockSpec((ROW_TILE, tc), lambda i, j: (i, j)),
        out_shape=jax.ShapeDtypeStruct((n, 2 * D_MLSTM), BF16),
        compiler_params=_params(("parallel", "parallel")),
    )(p, p, p, conv_w)


def _mlstm_kernel(q_ref, k_ref, v_ref, o_ref, gc_ref, gr_ref, nrm_ref, y_ref,
                  hf_ref, hb_ref, c_ref, n_ref, m_ref, *, nchunks, nctx):
    c_ref[...] = jnp.zeros_like(c_ref)
    n_ref[...] = jnp.zeros_like(n_ref)
    m_ref[...] = jnp.zeros_like(m_ref)
    row = lax.broadcasted_iota(I32, (CHUNK, CHUNK), 0)
    col = lax.broadcasted_iota(I32, (CHUNK, CHUNK), 1)
    seen = (row >= col, row <= col)
    h_refs = (hf_ref, hb_ref)

    def step(d, c):
        r0 = pl.multiple_of(c * CHUNK, CHUNK)
        rows = pl.ds(r0, CHUNK)
        q = q_ref[rows, :]
        k = k_ref[rows, :]
        v = v_ref[rows, :]
        gcol = gc_ref[0, rows, :]
        grow = gr_ref[0, c]
        ig_c, b_c = gcol[:, 2 * d:2 * d + 1], gcol[:, 2 * d + 1:2 * d + 2]
        ig_r, b_r = grow[2 * d:2 * d + 1, :], grow[2 * d + 1:2 * d + 2, :]
        b_last = b_c[CHUNK - 1:CHUNK, :] if d == 0 else b_c[0:1, :]
        m = m_ref[d, 0:1, 0:1]
        g_c = b_last - b_c + ig_c
        m_new = jnp.maximum(b_last + m, jnp.max(g_c, axis=0, keepdims=True))
        wk_c = jnp.exp(g_c - m_new)
        decay = jnp.exp(b_last + m - m_new)

        log_w = jnp.where(seen[d], b_c - b_r + ig_r, -jnp.inf)
        inter = b_c + m
        m_t = jnp.maximum(inter, jnp.max(log_w, axis=1, keepdims=True))
        w_prev = jnp.exp(inter - m_t)
        s = lax.dot_general(q, k, (((1,), (1,)), ((), ())), preferred_element_type=F32)
        s = s * jnp.exp(log_w - m_t)
        ct = c_ref[d]
        nvec = n_ref[d]
        num = (jnp.dot(s.astype(BF16), v, preferred_element_type=F32)
               + w_prev * jnp.dot(q, ct.astype(BF16), preferred_element_type=F32))
        qf = q.astype(F32)
        kf = k.astype(F32)
        den = jnp.sum(s, axis=1, keepdims=True) + w_prev * jnp.sum(qf * nvec, axis=1, keepdims=True)
        h_refs[d][rows, :] = num / jnp.maximum(jnp.abs(den), jnp.exp(-m_t))

        wv = (v.astype(F32) * wk_c).astype(BF16)
        c_ref[d] = decay * ct + lax.dot_general(k, wv, (((0,), (0,)), ((), ())),
                                                preferred_element_type=F32)
        n_ref[d] = decay * nvec + jnp.sum(kf * wk_c, axis=0, keepdims=True)
        m_ref[d] = jnp.broadcast_to(m_new, (1, LANE))

    def body(i, carry):
        step(0, i)
        step(1, jnp.where(i < nctx, nctx - 1 - i, nchunks - 1 - (i - nctx)))
        return carry

    lax.fori_loop(0, nchunks, body, 0)

    def finish(c, carry):
        rows = pl.ds(pl.multiple_of(c * CHUNK, CHUNK), CHUNK)
        h = hf_ref[rows, :] + hb_ref[rows, :]
        hn = h * lax.rsqrt(jnp.mean(h * h, axis=-1, keepdims=True) + EPS) * nrm_ref[...]
        y_ref[rows, :] = (_sigmoid(o_ref[rows, :].astype(F32)) * hn).astype(BF16)
        return carry

    lax.fori_loop(0, nchunks, finish, 0)


def mlstm_mixer(p, qk, gcol, grow, mlstm_norm, batch, tb, nctx):
    n = p.shape[0]
    nchunks = tb // CHUNK
    hd = MLSTM_HD
    return pl.pallas_call(
        functools.partial(_mlstm_kernel, nchunks=nchunks, nctx=nctx),
        grid=(batch, MLSTM_HEADS),
        in_specs=[pl.BlockSpec((tb, hd), lambda b, h: (b, h)),
                  pl.BlockSpec((tb, hd), lambda b, h: (b, MLSTM_HEADS + h)),
                  pl.BlockSpec((tb, hd), lambda b, h: (b, OFF_VM // hd + h)),
                  pl.BlockSpec((tb, hd), lambda b, h: (b, OFF_O // hd + h)),
                  pl.BlockSpec((1, tb, 4), lambda b, h: (h, b, 0)),
                  pl.BlockSpec((1, nchunks, 4, CHUNK), lambda b, h: (h, b, 0, 0)),
                  pl.BlockSpec((1, hd), lambda b, h: (0, h))],
        out_specs=pl.BlockSpec((tb, hd), lambda b, h: (b, h)),
        out_shape=jax.ShapeDtypeStruct((n, D_MLSTM), BF16),
        scratch_shapes=[pltpu.VMEM((tb, hd), F32), pltpu.VMEM((tb, hd), F32),
                        pltpu.VMEM((2, hd, hd), F32), pltpu.VMEM((2, 1, hd), F32),
                        pltpu.VMEM((2, 1, LANE), F32)],
        compiler_params=_params(("parallel", "parallel")),
    )(qk, qk, p, p, gcol, grow, mlstm_norm)


def _pool_kernel(xfull_ref, xrow_ref, w_ref, sc_ref, y_ref, m_ref, inv_ref, *, tb, ctx, seq):
    g = pl.program_id(0)
    i = pl.program_id(1)
    b = pl.program_id(2)
    tm = xrow_ref.shape[0]
    half = jnp.left_shift(1, g)
    win = 2 * half

    def grid_coords(pos):
        is_ctx = pos < ctx
        lat = pos - ctx
        r = jnp.where(is_ctx, 0, jnp.right_shift(lat, GRID_W_LOG2))
        c = jnp.where(is_ctx, pos, jnp.bitwise_and(lat, GRID_W - 1))
        return is_ctx, r, c

    @pl.when(b == 0)
    def _build_window_matrix():
        t = i * tm + lax.broadcasted_iota(I32, (tm, 1), 0)
        s = lax.broadcasted_iota(I32, (1, tb), 1)
        ctx_t, r_t, c_t = grid_coords(t)
        ctx_s, r_s, c_s = grid_coords(s)
        dr = r_s - r_t + half
        dc = c_s - c_t + half
        inside = (ctx_t == ctx_s) & (dr >= 0) & (dr < win) & (dc >= 0) & (dc < win)
        m_ref[...] = jnp.where(inside, 1.0, 0.0).astype(BF16)
        nrows = jnp.where(ctx_t, 1, seq // GRID_W)
        ncols = jnp.where(ctx_t, ctx, GRID_W)
        cnt_r = jnp.minimum(r_t - half + win, nrows) - jnp.maximum(r_t - half, 0)
        cnt_c = jnp.minimum(c_t - half + win, ncols) - jnp.maximum(c_t - half, 0)
        inv_ref[...] = 1.0 / (cnt_r * cnt_c).astype(F32)

    tot = jnp.dot(m_ref[...], xfull_ref[...], preferred_element_type=F32)
    d = (tot * inv_ref[...] - xrow_ref[...].astype(F32)).astype(BF16)
    y = jnp.dot(d, w_ref[0], preferred_element_type=F32) * sc_ref[0]
    y_ref[...] = y.astype(BF16)


def pool_mixer(p, pool_w, pool_scale, batch, tb, ctx, seq):
    n = p.shape[0]
    nrb = tb // ROW_TILE
    c0 = OFF_P // POOL_GC
    return pl.pallas_call(
        functools.partial(_pool_kernel, tb=tb, ctx=ctx, seq=seq),
        grid=(POOL_GROUPS, nrb, batch),
        in_specs=[pl.BlockSpec((tb, POOL_GC), lambda g, i, b: (b, c0 + g)),
                  pl.BlockSpec((ROW_TILE, POOL_GC), lambda g, i, b: (b * nrb + i, c0 + g)),
                  pl.BlockSpec((1, POOL_GC, POOL_GC), lambda g, i, b: (g, 0, 0)),
                  pl.BlockSpec((1, 1, POOL_GC), lambda g, i, b: (g, 0, 0))],
        out_specs=pl.BlockSpec((ROW_TILE, POOL_GC), lambda g, i, b: (b * nrb + i, g)),
        out_shape=jax.ShapeDtypeStruct((n, D_POOL), BF16),
        scratch_shapes=[pltpu.VMEM((ROW_TILE, tb), BF16), pltpu.VMEM((ROW_TILE, 1), F32)],
        compiler_params=_params(("arbitrary", "arbitrary", "arbitrary")),
    )(p, p, pool_w, pool_scale)


def _out_proj_kernel(x_ref, ya_ref, yb_ref, yc_ref, mod_ref, w_ref, o_ref):
    ya, yb, yc = ya_ref[...], yb_ref[...], yc_ref[...]
    k1 = D_SGU
    k2 = D_SGU + D_MLSTM
    for j in range(D_MODEL // 256):
        sl = slice(j * 256, (j + 1) * 256)
        acc = jnp.dot(ya, w_ref[0:k1, sl], preferred_element_type=F32)
        acc += jnp.dot(yb, w_ref[k1:k2, sl], preferred_element_type=F32)
        acc += jnp.dot(yc, w_ref[k2:D_MODEL, sl], preferred_element_type=F32)
        o_ref[:, sl] = x_ref[:, sl] + mod_ref[0, 2:3, sl] * acc


def out_proj(x, ya, yb, yc, mod, w_out, mod_row):
    n, d = x.shape
    return pl.pallas_call(
        _out_proj_kernel,
        grid=(n // ROW_TILE,),
        in_specs=[pl.BlockSpec((ROW_TILE, d), lambda i: (i, 0)),
                  pl.BlockSpec((ROW_TILE, D_SGU), lambda i: (i, 0)),
                  pl.BlockSpec((ROW_TILE, D_MLSTM), lambda i: (i, 0)),
                  pl.BlockSpec((ROW_TILE, D_POOL), lambda i: (i, 0)),
                  pl.BlockSpec((1, 6, d), lambda i: (mod_row(i), 0, 0)),
                  pl.BlockSpec((d, d), lambda i: (0, 0))],
        out_specs=pl.BlockSpec((ROW_TILE, d), lambda i: (i, 0)),
        out_shape=jax.ShapeDtypeStruct((n, d), F32),
        compiler_params=_params(("parallel",)),
    )(x, ya, yb, yc, mod, w_out)


def _topk_rows(vals, k):
    nrows = vals.shape[0]
    iota = lax.broadcasted_iota(I32, vals.shape, 0)
    out_v, out_i = [], []
    for _ in range(k):
        mx = jnp.max(vals, axis=0, keepdims=True)
        am = jnp.min(jnp.where(vals == mx, iota, nrows), axis=0, keepdims=True)
        out_v.append(mx)
        out_i.append(am)
        vals = jnp.where(iota == am, -jnp.inf, vals)
    return out_v, out_i


def _peer_route_kernel(x_ref, mod_ref, g_ref, wq_ref, keys_ref, hf_ref, e_ref, gate_ref,
                       hb_ref, cand_ref, cid_ref):
    h = pl.program_id(1)

    @pl.when(h == 0)
    def _norm():
        hf = _norm_mod(x_ref[...], g_ref[...], mod_ref[0, 3:4, :], mod_ref[0, 4:5, :])
        hf_ref[...] = hf
        hb_ref[...] = hf.astype(BF16)

    q = jnp.dot(hb_ref[...], wq_ref[...], preferred_element_type=F32).astype(BF16)
    half = PEER_DK // 2
    nt = (((1,), (1,)), ((), ()))
    s1, i1 = _topk_rows(lax.dot_general(keys_ref[0], q[:, :half], nt, preferred_element_type=F32),
                        PEER_TOPK)
    s2, i2 = _topk_rows(lax.dot_general(keys_ref[1], q[:, half:], nt, preferred_element_type=F32),
                        PEER_TOPK)
    tt = q.shape[0]
    for r, (a, b) in enumerate(CAND_PAIRS):
        cand_ref[r:r + 1, :] = s1[a] + s2[b]
        cid_ref[r:r + 1, :] = i1[a] * N_KEYS + i2[b]
    npad = CAND_ROWS - len(CAND_PAIRS)
    cand_ref[len(CAND_PAIRS):, :] = jnp.full((npad, tt), -jnp.inf, F32)
    cid_ref[len(CAND_PAIRS):, :] = jnp.zeros((npad, tt), I32)
    sc, slot = _topk_rows(cand_ref[...], PEER_TOPK)
    cid = cid_ref[...]
    riota = lax.broadcasted_iota(I32, cid.shape, 0)
    ex = [jnp.exp(v - sc[0]) for v in sc]
    tot = ex[0]
    for v in ex[1:]:
        tot = tot + v
    for j in range(PEER_TOPK):
        e_ref[0, j:j + 1, :] = jnp.sum(jnp.where(riota == slot[j], cid, 0), axis=0, keepdims=True)
        gate_ref[0, j:j + 1, :] = ex[j] / tot


def peer_route(x, mod, norm_g, wq, keys, mod_row):
    n, d = x.shape
    tt = ROW_TILE
    return pl.pallas_call(
        _peer_route_kernel,
        grid=(n // tt, PEER_HEADS),
        in_specs=[pl.BlockSpec((tt, d), lambda i, h: (i, 0)),
                  pl.BlockSpec((1, 6, d), lambda i, h: (mod_row(i), 0, 0)),
                  pl.BlockSpec((1, d), lambda i, h: (0, 0)),
                  pl.BlockSpec((d, PEER_DK), lambda i, h: (0, h)),
                  pl.BlockSpec((2, N_KEYS, PEER_DK // 2), lambda i, h: (0, 0, 0))],
        out_specs=[pl.BlockSpec((tt, d), lambda i, h: (i, 0)),
                   pl.BlockSpec((1, PEER_TOPK, tt), lambda i, h: (h, 0, i)),
                   pl.BlockSpec((1, PEER_TOPK, tt), lambda i, h: (h, 0, i))],
        out_shape=[jax.ShapeDtypeStruct((n, d), F32),
                   jax.ShapeDtypeStruct((PEER_HEADS, PEER_TOPK, n), I32),
                   jax.ShapeDtypeStruct((PEER_HEADS, PEER_TOPK, n), F32)],
        scratch_shapes=[pltpu.VMEM((tt, d), BF16), pltpu.VMEM((CAND_ROWS, tt), F32),
                        pltpu.VMEM((CAND_ROWS, tt), I32)],
        compiler_params=_params(("parallel", "arbitrary")),
    )(x, mod, norm_g, wq, keys)


ROW_SUB = D_MODEL // LANE
ISSUE_UNROLL = 8
SUBLANES = 8
FOLD_ORDER = (0, 4, 2, 6, 1, 5, 3, 7)


def _fold(a, b, shift, first):
    return (jnp.where(first, a, pltpu.roll(b, shift, 0))
            + jnp.where(first, pltpu.roll(a, SUBLANES - shift, 0), b))


def _peer_gather_kernel(ids_ref, hf_ref, gt_ref, xn_ref, gate_ref, u_hbm, v_hbm, o_ref,
                        ubuf, vbuf, wbuf, sem, *, ngroups):
    i = pl.program_id(0)
    slot = i % 2
    nrows = GATHER_TOKENS * PEER_PAIRS

    @pl.when(i < ngroups)
    def _issue():
        def body(g, carry):
            for j in range(ISSUE_UNROLL):
                r = g * ISSUE_UNROLL + j
                e = ids_ref[0, 0, r]
                pltpu.make_async_copy(u_hbm.at[e], ubuf.at[slot, r], sem.at[0, slot]).start(priority=j % 2)
                pltpu.make_async_copy(v_hbm.at[e], vbuf.at[slot, r], sem.at[1, slot]).start(priority=j % 2)
            return carry
        lax.fori_loop(0, nrows // ISSUE_UNROLL, body, 0)

    @pl.when(i > 0)
    def _compute():
        ps = 1 - slot
        pltpu.make_async_copy(u_hbm.at[pl.ds(0, nrows)], ubuf.at[ps], sem.at[0, ps]).wait()
        pltpu.make_async_copy(v_hbm.at[pl.ds(0, nrows)], vbuf.at[ps], sem.at[1, ps]).wait()

        sub = lax.broadcasted_iota(I32, (SUBLANES, LANE), 0)
        first4, first2, first1 = sub < 4, sub % 4 < 2, sub % 2 < 1
        for t in range(GATHER_TOKENS):
            x = hf_ref[t]
            base = t * PEER_PAIRS
            for j in range(PEER_PAIRS // SUBLANES):
                r = []
                for p in FOLD_ORDER:
                    prod = ubuf[ps, base + j * SUBLANES + p] * x
                    r.append(prod[:SUBLANES] + prod[SUBLANES:])
                l1 = [_fold(r[0], r[1], 4, first4), _fold(r[2], r[3], 4, first4),
                      _fold(r[4], r[5], 4, first4), _fold(r[6], r[7], 4, first4)]
                l2 = [_fold(l1[0], l1[1], 2, first2), _fold(l1[2], l1[3], 2, first2)]
                y = _fold(l2[0], l2[1], 1, first1)
                rows = slice(j * SUBLANES, (j + 1) * SUBLANES)
                a = jnp.sum(y, axis=1, keepdims=True)
                w = gt_ref[0, rows, t:t + 1] * _gelu(a)
                wbuf[rows, :] = jnp.broadcast_to(w, (SUBLANES, LANE))
            nacc = 4
            acc = [jnp.zeros((ROW_SUB, LANE), F32) for _ in range(nacc)]
            for k in range(PEER_PAIRS):
                acc[k % nacc] = acc[k % nacc] + vbuf[ps, base + k] * wbuf[k:k + 1, :]
            out = (acc[0] + acc[1]) + (acc[2] + acc[3])
            o_ref[t] = xn_ref[t] + gate_ref[0] * out


def peer_gather(ids, hf, gates_t, xn, gate_ffn, u_tab, v_tab, mod_row_group):
    n, d = xn.shape
    gt = GATHER_TOKENS
    ngroups = n // gt
    nrows = gt * PEER_PAIRS
    slab = (ROW_SUB, LANE)
    nexp = u_tab.shape[0]

    def prev(i):
        return jnp.maximum(i - 1, 0)

    tok_spec = pl.BlockSpec((gt,) + slab, lambda i: (prev(i), 0, 0))
    out = pl.pallas_call(
        functools.partial(_peer_gather_kernel, ngroups=ngroups),
        grid=(ngroups + 1,),
        in_specs=[pl.BlockSpec((1, 1, nrows), lambda i: (jnp.minimum(i, ngroups - 1), 0, 0),
                               memory_space=pltpu.SMEM),
                  tok_spec,
                  pl.BlockSpec((1, PEER_PAIRS, gt), lambda i: (prev(i), 0, 0)),
                  tok_spec,
                  pl.BlockSpec((1,) + slab, lambda i: (mod_row_group(prev(i)), 0, 0)),
                  pl.BlockSpec(memory_space=pl.ANY),
                  pl.BlockSpec(memory_space=pl.ANY)],
        out_specs=tok_spec,
        out_shape=jax.ShapeDtypeStruct((n,) + slab, F32),
        scratch_shapes=[pltpu.VMEM((2, nrows) + slab, F32), pltpu.VMEM((2, nrows) + slab, F32),
                        pltpu.VMEM((PEER_PAIRS, LANE), F32), pltpu.SemaphoreType.DMA((2, 2))],
        compiler_params=_params(("arbitrary",)),
    )(ids, hf.reshape((n,) + slab), gates_t, xn.reshape((n,) + slab),
      gate_ffn.reshape((-1,) + slab), u_tab.reshape((nexp,) + slab), v_tab.reshape((nexp,) + slab))
    return out.reshape(n, d)


def _final_norm_kernel(x_ref, g_ref, o_ref):
    x = x_ref[...]
    o_ref[0] = x * lax.rsqrt(jnp.mean(x * x, axis=-1, keepdims=True) + EPS) * g_ref[...]


def final_norm(x, g, batch, seq, ctx):
    n, d = x.shape
    tb = ctx + seq
    nrb = tb // ROW_TILE
    cb = ctx // ROW_TILE
    return pl.pallas_call(
        _final_norm_kernel,
        grid=(batch, seq // ROW_TILE),
        in_specs=[pl.BlockSpec((ROW_TILE, d), lambda b, i: (b * nrb + cb + i, 0)),
                  pl.BlockSpec((1, d), lambda b, i: (0, 0))],
        out_specs=pl.BlockSpec((1, ROW_TILE, d), lambda b, i: (b, i, 0)),
        out_shape=jax.ShapeDtypeStruct((batch, seq, d), F32),
        compiler_params=_params(("parallel", "parallel")),
    )(x, g)


def kernel(x, c, ctx, c_ctx, ada_w, ada_b, norm_mix, norm_ffn, w_in, b_gate, sgu_norm, sgu_w, sgu_b,
           qk_conv_w, mlstm_norm, pool_w, pool_scale, w_out, peer_wq, peer_keys, peer_u, peer_v,
           norm_final):
    batch, seq, d = x.shape
    ctx_len = ctx.shape[1]
    depth = ada_w.shape[0]
    tb = ctx_len + seq
    n = batch * tb
    assert d == D_MODEL and w_in.shape[2] == OFF_G + N_GATES
    assert ctx_len % ROW_TILE == 0 and seq % ROW_TILE == 0 and seq % GRID_W == 0
    assert n % GATHER_TOKENS == 0 and ROW_TILE % GATHER_TOKENS == 0
    blocks_per_batch = tb // ROW_TILE
    ctx_blocks = ctx_len // ROW_TILE
    nchunks = tb // CHUNK

    def mod_row(i):
        return jnp.where(i % blocks_per_batch < ctx_blocks, batch, i // blocks_per_batch)

    def mod_row_group(i):
        return mod_row(i // (ROW_TILE // GATHER_TOKENS))

    cond_rows = -(-(batch + 1) // 8) * 8
    cond = jnp.zeros((cond_rows, d), F32).at[:batch].set(c).at[batch].set(c_ctx)
    mods = ada_modulation_all(cond, ada_w, ada_b).reshape(depth, cond_rows, 6, d)

    xs = jnp.concatenate([ctx, x], axis=1).reshape(n, d)
    lane_pad = LANE - N_GATES
    for l in range(depth):
        mod = mods[l]
        w_main = w_in[l, :, :OFF_G].astype(BF16)
        w_gate = jnp.pad(w_in[l, :, OFF_G:], ((0, 0), (0, lane_pad))).astype(BF16)
        p, graw = in_proj(xs, mod, norm_mix[l][None], w_main, w_gate, mod_row)

        y_a = sgu_mixer(p, sgu_norm[l][None], sgu_w[l].astype(BF16), sgu_b[l].T)

        gp = gate_prep(graw[0], jnp.pad(b_gate[l], (0, lane_pad))[None])
        gp = gp[:, :N_GATES].reshape(n, 2, 2, MLSTM_HEADS)
        gcol = gp.transpose(3, 0, 1, 2).reshape(MLSTM_HEADS, n, 4)
        grow = gcol.reshape(MLSTM_HEADS, n // CHUNK, CHUNK, 4).transpose(0, 1, 3, 2)
        qk = qk_conv(p, qk_conv_w[l], blocks_per_batch, ctx_blocks)
        y_b = mlstm_mixer(p, qk, gcol, grow, mlstm_norm[l][None], batch, tb, ctx_len // CHUNK)

        y_c = pool_mixer(p, pool_w[l].astype(BF16), pool_scale[l].reshape(POOL_GROUPS, 1, POOL_GC),
                         batch, tb, ctx_len, seq)

        xn = out_proj(xs, y_a, y_b, y_c, mod, w_out[l].astype(BF16), mod_row)

        hf, e, gates = peer_route(xn, mod, norm_ffn[l][None], peer_wq[l].astype(BF16),
                                  peer_keys[l].astype(BF16), mod_row)
        ids = e.reshape(PEER_PAIRS, n).T.reshape(n // GATHER_TOKENS, 1, GATHER_TOKENS * PEER_PAIRS)
        gates_t = gates.reshape(PEER_PAIRS, n // GATHER_TOKENS, GATHER_TOKENS).transpose(1, 0, 2)
        xs = peer_gather(ids, hf, gates_t, xn, mod[:, 5], peer_u[l], peer_v[l], mod_row_group)

    return final_norm(xs, norm_final[None], batch, seq, ctx_len)
```

```python
import functools

import jax
import jax.numpy as jnp
from jax import lax
from jax.experimental import pallas as pl
from jax.experimental.pallas import tpu as pltpu

F32 = jnp.float32
BF16 = jnp.bfloat16
I32 = jnp.int32

EPS = 1e-6
GRID_W = 64
GRID_W_LOG2 = 6
D_MODEL = 2048
D_SGU = D_MODEL // 4
SGU_HEADS = 4
SGU_HD = D_SGU // SGU_HEADS
CHUNK = 128
D_MLSTM = D_MODEL // 2
MLSTM_HEADS = 4
MLSTM_HD = D_MLSTM // MLSTM_HEADS
D_POOL = D_MODEL // 4
POOL_GROUPS = 4
POOL_GC = D_POOL // POOL_GROUPS
N_GATES = 2 * 2 * MLSTM_HEADS
OFF_U = 0
OFF_V = OFF_U + D_SGU
OFF_P = OFF_V + D_SGU
OFF_Q = OFF_P + D_POOL
OFF_O = OFF_Q + D_MLSTM
OFF_K = OFF_O + D_MLSTM
OFF_VM = OFF_K + D_MLSTM
OFF_G = OFF_VM + D_MLSTM
N_KEYS = 128
PEER_HEADS = 8
PEER_TOPK = 16
PEER_DK = 256
PEER_PAIRS = PEER_HEADS * PEER_TOPK

LANE = 128
ROW_TILE = 256
GATHER_TOKENS = 8
VMEM_LIMIT = 56 * 1024 * 1024

CAND_PAIRS = [(a, b) for a in range(PEER_TOPK) for b in range(PEER_TOPK)
              if (a + 1) * (b + 1) <= PEER_TOPK]
CAND_ROWS = 56


def _params(sem, vmem=VMEM_LIMIT):
    return pltpu.CompilerParams(dimension_semantics=sem, vmem_limit_bytes=vmem)


def _norm_mod(x, g, shift, scale):
    ms = jnp.mean(x * x, axis=-1, keepdims=True)
    y = x * lax.rsqrt(ms + EPS)
    return (y * g) * (1.0 + scale) + shift


def _gelu(x):
    return jax.nn.gelu(x, approximate=True)


def _sigmoid(x):
    return 1.0 / (1.0 + jnp.exp(-x))


def _ada_kernel(cond_ref, w_ref, b_ref, o_ref):
    a = cond_ref[...]
    a = (a * _sigmoid(a)).astype(BF16)
    o_ref[0] = jnp.dot(a, w_ref[0].astype(BF16), preferred_element_type=F32) + b_ref[0]


def ada_modulation_all(cond, ada_w, ada_b):
    depth, d, d6 = ada_w.shape
    rows = cond.shape[0]
    tn = 1024
    return pl.pallas_call(
        _ada_kernel,
        grid=(depth, d6 // tn),
        in_specs=[pl.BlockSpec((rows, d), lambda l, j: (0, 0)),
                  pl.BlockSpec((1, d, tn), lambda l, j: (l, 0, j)),
                  pl.BlockSpec((1, 1, tn), lambda l, j: (l, 0, j))],
        out_specs=pl.BlockSpec((1, rows, tn), lambda l, j: (l, 0, j)),
        out_shape=jax.ShapeDtypeStruct((depth, rows, d6), F32),
        compiler_params=_params(("parallel", "parallel")),
    )(cond, ada_w, ada_b.reshape(depth, 1, d6))


def _in_proj_kernel(x_ref, mod_ref, g_ref, w_ref, wg_ref, p_ref, gate_ref, *, ncols):
    h = _norm_mod(x_ref[...], g_ref[...], mod_ref[0, 0:1, :], mod_ref[0, 1:2, :]).astype(BF16)
    for j in range(ncols // 256):
        sl = slice(j * 256, (j + 1) * 256)
        p_ref[:, sl] = jnp.dot(h, w_ref[:, sl], preferred_element_type=F32).astype(BF16)
    gate_ref[0] = jnp.dot(h, wg_ref[...], preferred_element_type=F32)


def in_proj(x, mod, norm_g, w_main, w_gate, mod_row):
    n, d = x.shape
    ncol_blocks = 2
    ncols = OFF_G // ncol_blocks
    return pl.pallas_call(
        functools.partial(_in_proj_kernel, ncols=ncols),
        grid=(ncol_blocks, n // ROW_TILE),
        in_specs=[pl.BlockSpec((ROW_TILE, d), lambda c, i: (i, 0)),
                  pl.BlockSpec((1, 6, d), lambda c, i: (mod_row(i), 0, 0)),
                  pl.BlockSpec((1, d), lambda c, i: (0, 0)),
                  pl.BlockSpec((d, ncols), lambda c, i: (0, c)),
                  pl.BlockSpec((d, LANE), lambda c, i: (0, 0))],
        out_specs=[pl.BlockSpec((ROW_TILE, ncols), lambda c, i: (i, c)),
                   pl.BlockSpec((1, ROW_TILE, LANE), lambda c, i: (c, i, 0))],
        out_shape=[jax.ShapeDtypeStruct((n, OFF_G), BF16),
                   jax.ShapeDtypeStruct((ncol_blocks, n, LANE), F32)],
        compiler_params=_params(("arbitrary", "arbitrary")),
    )(x, mod, norm_g, w_main, w_gate)


def _sgu_kernel(u_ref, v_ref, g_ref, ws_ref, bs_ref, y_ref):
    u = _gelu(u_ref[...].astype(F32))
    v = _gelu(v_ref[...].astype(F32))
    ms = jnp.mean(v * v, axis=-1, keepdims=True)
    v = (v * lax.rsqrt(ms + EPS) * g_ref[...]).astype(BF16)
    for c in range(ROW_TILE // CHUNK):
        rows = slice(c * CHUNK, (c + 1) * CHUNK)
        for h in range(SGU_HEADS):
            cols = slice(h * SGU_HD, (h + 1) * SGU_HD)
            mixed = jnp.dot(ws_ref[h], v[rows, cols], preferred_element_type=F32) + bs_ref[:, h:h + 1]
            y_ref[rows, cols] = (u[rows, cols] * mixed).astype(BF16)


def sgu_mixer(p, sgu_norm, sgu_w, sgu_bt):
    n = p.shape[0]
    return pl.pallas_call(
        _sgu_kernel,
        grid=(n // ROW_TILE,),
        in_specs=[pl.BlockSpec((ROW_TILE, D_SGU), lambda i: (i, OFF_U // D_SGU)),
                  pl.BlockSpec((ROW_TILE, D_SGU), lambda i: (i, OFF_V // D_SGU)),
                  pl.BlockSpec((1, D_SGU), lambda i: (0, 0)),
                  pl.BlockSpec((SGU_HEADS, CHUNK, CHUNK), lambda i: (0, 0, 0)),
                  pl.BlockSpec((CHUNK, SGU_HEADS), lambda i: (0, 0))],
        out_specs=pl.BlockSpec((ROW_TILE, D_SGU), lambda i: (i, 0)),
        out_shape=jax.ShapeDtypeStruct((n, D_SGU), BF16),
        compiler_params=_params(("parallel",)),
    )(p, p, sgu_norm, sgu_w, sgu_bt)


def _gate_prep_kernel(g_ref, bias_ref, o_ref):
    g = g_ref[...] + bias_ref[...]
    lf = jnp.minimum(g, 0.0) - jnp.log(1.0 + jnp.exp(-jnp.abs(g)))
    row = lax.broadcasted_iota(I32, (CHUNK, CHUNK), 0)
    col = lax.broadcasted_iota(I32, (CHUNK, CHUNK), 1)
    lower = (col <= row).astype(F32)
    upper = (col >= row).astype(F32)
    b_fwd = jnp.dot(lower, lf, preferred_element_type=F32, precision=lax.Precision.HIGHEST)
    b_bwd = jnp.dot(upper, lf, preferred_element_type=F32, precision=lax.Precision.HIGHEST)
    lane = lax.broadcasted_iota(I32, (1, LANE), 1)
    is_input_gate = (lane // MLSTM_HEADS) % 2 == 0
    is_fwd = lane < 2 * MLSTM_HEADS
    o_ref[...] = jnp.where(is_input_gate, g, jnp.where(is_fwd, b_fwd, b_bwd))


def gate_prep(graw, bias):
    n = graw.shape[0]
    return pl.pallas_call(
        _gate_prep_kernel,
        grid=(n // CHUNK,),
        in_specs=[pl.BlockSpec((CHUNK, LANE), lambda i: (i, 0)),
                  pl.BlockSpec((1, LANE), lambda i: (0, 0))],
        out_specs=pl.BlockSpec((CHUNK, LANE), lambda i: (i, 0)),
        out_shape=jax.ShapeDtypeStruct((n, LANE), F32),
        compiler_params=_params(("parallel",)),
    )(graw, bias)


HALO = 16


def _qk_conv_kernel(xm_ref, xp_ref, xn_ref, w_ref, o_ref, *, blocks_per_batch, ctx_blocks):
    i = pl.program_id(0)
    j = pl.program_id(1)
    x = xm_ref[...].astype(F32)
    tm = x.shape[0]
    prev_row = xp_ref[...].astype(F32)[HALO - 1:HALO, :]
    next_row = xn_ref[...].astype(F32)[0:1, :]
    ib = i % blocks_per_batch
    at_start = jnp.logical_or(ib == 0, ib == ctx_blocks)
    at_end = jnp.logical_or(ib == ctx_blocks - 1, ib == blocks_per_batch - 1)
    prev_row = jnp.where(at_start, 0.0, prev_row)
    next_row = jnp.where(at_end, 0.0, next_row)
    rows = lax.broadcasted_iota(I32, (tm, 1), 0)
    x_prev = jnp.where(rows == 0, prev_row, pltpu.roll(x, 1, 0))
    x_next = jnp.where(rows == tm - 1, next_row, pltpu.roll(x, tm - 1, 0))
    y = w_ref[0:1, :] * x_prev + w_ref[1:2, :] * x + w_ref[2:3, :] * x_next
    y = y * _sigmoid(y)
    y = y * jnp.where(j >= 2, MLSTM_HD ** -0.5, 1.0)
    o_ref[...] = y.astype(BF16)


def qk_conv(p, conv_w, blocks_per_batch, ctx_blocks):
    n = p.shape[0]
    tc = 512
    nhalo = n // HALO
    per = ROW_TILE // HALO

    def col(j):
        return jnp.where(j < 2, OFF_Q // tc + j, OFF_K // tc + j - 2)

    return pl.pallas_call(
        functools.partial(_qk_conv_kernel, blocks_per_batch=blocks_per_batch, ctx_blocks=ctx_blocks),
        grid=(n // ROW_TILE, 2 * D_MLSTM // tc),
        in_specs=[pl.BlockSpec((ROW_TILE, tc), lambda i, j: (i, col(j))),
                  pl.BlockSpec((HALO, tc), lambda i, j: (jnp.maximum(i * per - 1, 0), col(j))),
                  pl.BlockSpec((HALO, tc), lambda i, j: (jnp.minimum((i + 1) * per, nhalo - 1), col(j))),
                  pl.BlockSpec((3, tc), lambda i, j: (0, j))],
        out_specs=pl.BlockSpec((ROW_TILE, tc), lambda i, j: (i, j)),
        out_shape=jax.ShapeDtypeStruct((n, 2 * D_MLSTM), BF16),
        compiler_params=_params(("parallel", "parallel")),
    )(p, p, p, conv_w)


def _mlstm_kernel(q_ref, k_ref, v_ref, o_ref, gc_ref, gr_ref, nrm_ref, y_ref,
                  hf_ref, hb_ref, c_ref, n_ref, m_ref, *, nchunks, nctx):
    c_ref[...] = jnp.zeros_like(c_ref)
    n_ref[...] = jnp.zeros_like(n_ref)
    m_ref[...] = jnp.zeros_like(m_ref)
    row = lax.broadcasted_iota(I32, (CHUNK, CHUNK), 0)
    col = lax.broadcasted_iota(I32, (CHUNK, CHUNK), 1)
    seen = (row >= col, row <= col)
    h_refs = (hf_ref, hb_ref)

    def step(d, c):
        r0 = pl.multiple_of(c * CHUNK, CHUNK)
        rows = pl.ds(r0, CHUNK)
        q = q_ref[rows, :]
        k = k_ref[rows, :]
        v = v_ref[rows, :]
        gcol = gc_ref[0, rows, :]
        grow = gr_ref[0, c]
        ig_c, b_c = gcol[:, 2 * d:2 * d + 1], gcol[:, 2 * d + 1:2 * d + 2]
        ig_r, b_r = grow[2 * d:2 * d + 1, :], grow[2 * d + 1:2 * d + 2, :]
        b_last = b_c[CHUNK - 1:CHUNK, :] if d == 0 else b_c[0:1, :]
        m = m_ref[d, 0:1, 0:1]
        g_c = b_last - b_c + ig_c
        m_new = jnp.maximum(b_last + m, jnp.max(g_c, axis=0, keepdims=True))
        wk_c = jnp.exp(g_c - m_new)
        decay = jnp.exp(b_last + m - m_new)

        log_w = jnp.where(seen[d], b_c - b_r + ig_r, -jnp.inf)
        inter = b_c + m
        m_t = jnp.maximum(inter, jnp.max(log_w, axis=1, keepdims=True))
        w_prev = jnp.exp(inter - m_t)
        s = lax.dot_general(q, k, (((1,), (1,)), ((), ())), preferred_element_type=F32)
        s = s * jnp.exp(log_w - m_t)
        ct = c_ref[d]
        nvec = n_ref[d]
        num = (jnp.dot(s.astype(BF16), v, preferred_element_type=F32)
               + w_prev * jnp.dot(q, ct.astype(BF16), preferred_element_type=F32))
        qf = q.astype(F32)
        kf = k.astype(F32)
        den = jnp.sum(s, axis=1, keepdims=True) + w_prev * jnp.sum(qf * nvec, axis=1, keepdims=True)
        h_refs[d][rows, :] = num / jnp.maximum(jnp.abs(den), jnp.exp(-m_t))

        wv = (v.astype(F32) * wk_c).astype(BF16)
        c_ref[d] = decay * ct + lax.dot_general(k, wv, (((0,), (0,)), ((), ())),
                                                preferred_element_type=F32)
        n_ref[d] = decay * nvec + jnp.sum(kf * wk_c, axis=0, keepdims=True)
        m_ref[d] = jnp.broadcast_to(m_new, (1, LANE))

    def body(i, carry):
        step(0, i)
        step(1, jnp.where(i < nctx, nctx - 1 - i, nchunks - 1 - (i - nctx)))
        return carry

    lax.fori_loop(0, nchunks, body, 0)

    def finish(c, carry):
        rows = pl.ds(pl.multiple_of(c * CHUNK, CHUNK), CHUNK)
        h = hf_ref[rows, :] + hb_ref[rows, :]
        hn = h * lax.rsqrt(jnp.mean(h * h, axis=-1, keepdims=True) + EPS) * nrm_ref[...]
        y_ref[rows, :] = (_sigmoid(o_ref[rows, :].astype(F32)) * hn).astype(BF16)
        return carry

    lax.fori_loop(0, nchunks, finish, 0)


def mlstm_mixer(p, qk, gcol, grow, mlstm_norm, batch, tb, nctx):
    n = p.shape[0]
    nchunks = tb // CHUNK
    hd = MLSTM_HD
    return pl.pallas_call(
        functools.partial(_mlstm_kernel, nchunks=nchunks, nctx=nctx),
        grid=(batch, MLSTM_HEADS),
        in_specs=[pl.BlockSpec((tb, hd), lambda b, h: (b, h)),
                  pl.BlockSpec((tb, hd), lambda b, h: (b, MLSTM_HEADS + h)),
                  pl.BlockSpec((tb, hd), lambda b, h: (b, OFF_VM // hd + h)),
                  pl.BlockSpec((tb, hd), lambda b, h: (b, OFF_O // hd + h)),
                  pl.BlockSpec((1, tb, 4), lambda b, h: (h, b, 0)),
                  pl.BlockSpec((1, nchunks, 4, CHUNK), lambda b, h: (h, b, 0, 0)),
                  pl.BlockSpec((1, hd), lambda b, h: (0, h))],
        out_specs=pl.BlockSpec((tb, hd), lambda b, h: (b, h)),
        out_shape=jax.ShapeDtypeStruct((n, D_MLSTM), BF16),
        scratch_shapes=[pltpu.VMEM((tb, hd), F32), pltpu.VMEM((tb, hd), F32),
                        pltpu.VMEM((2, hd, hd), F32), pltpu.VMEM((2, 1, hd), F32),
                        pltpu.VMEM((2, 1, LANE), F32)],
        compiler_params=_params(("parallel", "parallel")),
    )(qk, qk, p, p, gcol, grow, mlstm_norm)


def _pool_kernel(xfull_ref, xrow_ref, w_ref, sc_ref, y_ref, m_ref, inv_ref, *, tb, ctx, seq):
    g = pl.program_id(0)
    i = pl.program_id(1)
    b = pl.program_id(2)
    tm = xrow_ref.shape[0]
    half = jnp.left_shift(1, g)
    win = 2 * half

    def grid_coords(pos):
        is_ctx = pos < ctx
        lat = pos - ctx
        r = jnp.where(is_ctx, 0, jnp.right_shift(lat, GRID_W_LOG2))
        c = jnp.where(is_ctx, pos, jnp.bitwise_and(lat, GRID_W - 1))
        return is_ctx, r, c

    @pl.when(b == 0)
    def _build_window_matrix():
        t = i * tm + lax.broadcasted_iota(I32, (tm, 1), 0)
        s = lax.broadcasted_iota(I32, (1, tb), 1)
        ctx_t, r_t, c_t = grid_coords(t)
        ctx_s, r_s, c_s = grid_coords(s)
        dr = r_s - r_t + half
        dc = c_s - c_t + half
        inside = (ctx_t == ctx_s) & (dr >= 0) & (dr < win) & (dc >= 0) & (dc < win)
        m_ref[...] = jnp.where(inside, 1.0, 0.0).astype(BF16)
        nrows = jnp.where(ctx_t, 1, seq // GRID_W)
        ncols = jnp.where(ctx_t, ctx, GRID_W)
        cnt_r = jnp.minimum(r_t - half + win, nrows) - jnp.maximum(r_t - half, 0)
        cnt_c = jnp.minimum(c_t - half + win, ncols) - jnp.maximum(c_t - half, 0)
        inv_ref[...] = 1.0 / (cnt_r * cnt_c).astype(F32)

    tot = jnp.dot(m_ref[...], xfull_ref[...], preferred_element_type=F32)
    d = (tot * inv_ref[...] - xrow_ref[...].astype(F32)).astype(BF16)
    y = jnp.dot(d, w_ref[0], preferred_element_type=F32) * sc_ref[0]
    y_ref[...] = y.astype(BF16)


def pool_mixer(p, pool_w, pool_scale, batch, tb, ctx, seq):
    n = p.shape[0]
    nrb = tb // ROW_TILE
    c0 = OFF_P // POOL_GC
    return pl.pallas_call(
        functools.partial(_pool_kernel, tb=tb, ctx=ctx, seq=seq),
        grid=(POOL_GROUPS, nrb, batch),
        in_specs=[pl.BlockSpec((tb, POOL_GC), lambda g, i, b: (b, c0 + g)),
                  pl.BlockSpec((ROW_TILE, POOL_GC), lambda g, i, b: (b * nrb + i, c0 + g)),
                  pl.BlockSpec((1, POOL_GC, POOL_GC), lambda g, i, b: (g, 0, 0)),
                  pl.BlockSpec((1, 1, POOL_GC), lambda g, i, b: (g, 0, 0))],
        out_specs=pl.BlockSpec((ROW_TILE, POOL_GC), lambda g, i, b: (b * nrb + i, g)),
        out_shape=jax.ShapeDtypeStruct((n, D_POOL), BF16),
        scratch_shapes=[pltpu.VMEM((ROW_TILE, tb), BF16), pltpu.VMEM((ROW_TILE, 1), F32)],
        compiler_params=_params(("arbitrary", "arbitrary", "arbitrary")),
    )(p, p, pool_w, pool_scale)


def _out_proj_kernel(x_ref, ya_ref, yb_ref, yc_ref, mod_ref, w_ref, o_ref):
    ya, yb, yc = ya_ref[...], yb_ref[...], yc_ref[...]
    k1 = D_SGU
    k2 = D_SGU + D_MLSTM
    for j in range(D_MODEL // 256):
        sl = slice(j * 256, (j + 1) * 256)
        acc = jnp.dot(ya, w_ref[0:k1, sl], preferred_element_type=F32)
        acc += jnp.dot(yb, w_ref[k1:k2, sl], preferred_element_type=F32)
        acc += jnp.dot(yc, w_ref[k2:D_MODEL, sl], preferred_element_type=F32)
        o_ref[:, sl] = x_ref[:, sl] + mod_ref[0, 2:3, sl] * acc


def out_proj(x, ya, yb, yc, mod, w_out, mod_row):
    n, d = x.shape
    return pl.pallas_call(
        _out_proj_kernel,
        grid=(n // ROW_TILE,),
        in_specs=[pl.BlockSpec((ROW_TILE, d), lambda i: (i, 0)),
                  pl.BlockSpec((ROW_TILE, D_SGU), lambda i: (i, 0)),
                  pl.BlockSpec((ROW_TILE, D_MLSTM), lambda i: (i, 0)),
                  pl.BlockSpec((ROW_TILE, D_POOL), lambda i: (i, 0)),
                  pl.BlockSpec((1, 6, d), lambda i: (mod_row(i), 0, 0)),
                  pl.BlockSpec((d, d), lambda i: (0, 0))],
        out_specs=pl.BlockSpec((ROW_TILE, d), lambda i: (i, 0)),
        out_shape=jax.ShapeDtypeStruct((n, d), F32),
        compiler_params=_params(("parallel",)),
    )(x, ya, yb, yc, mod, w_out)


def _topk_rows(vals, k):
    nrows = vals.shape[0]
    iota = lax.broadcasted_iota(I32, vals.shape, 0)
    out_v, out_i = [], []
    for _ in range(k):
        mx = jnp.max(vals, axis=0, keepdims=True)
        am = jnp.min(jnp.where(vals == mx, iota, nrows), axis=0, keepdims=True)
        out_v.append(mx)
        out_i.append(am)
        vals = jnp.where(iota == am, -jnp.inf, vals)
    return out_v, out_i


def _peer_route_kernel(x_ref, mod_ref, g_ref, wq_ref, keys_ref, hf_ref, e_ref, gate_ref,
                       hb_ref, cand_ref, cid_ref):
    h = pl.program_id(1)

    @pl.when(h == 0)
    def _norm():
        hf = _norm_mod(x_ref[...], g_ref[...], mod_ref[0, 3:4, :], mod_ref[0, 4:5, :])
        hf_ref[...] = hf
        hb_ref[...] = hf.astype(BF16)

    q = jnp.dot(hb_ref[...], wq_ref[...], preferred_element_type=F32).astype(BF16)
    half = PEER_DK // 2
    nt = (((1,), (1,)), ((), ()))
    s1, i1 = _topk_rows(lax.dot_general(keys_ref[0], q[:, :half], nt, preferred_element_type=F32),
                        PEER_TOPK)
    s2, i2 = _topk_rows(lax.dot_general(keys_ref[1], q[:, half:], nt, preferred_element_type=F32),
                        PEER_TOPK)
    tt = q.shape[0]
    for r, (a, b) in enumerate(CAND_PAIRS):
        cand_ref[r:r + 1, :] = s1[a] + s2[b]
        cid_ref[r:r + 1, :] = i1[a] * N_KEYS + i2[b]
    npad = CAND_ROWS - len(CAND_PAIRS)
    cand_ref[len(CAND_PAIRS):, :] = jnp.full((npad, tt), -jnp.inf, F32)
    cid_ref[len(CAND_PAIRS):, :] = jnp.zeros((npad, tt), I32)
    sc, slot = _topk_rows(cand_ref[...], PEER_TOPK)
    cid = cid_ref[...]
    riota = lax.broadcasted_iota(I32, cid.shape, 0)
    ex = [jnp.exp(v - sc[0]) for v in sc]
    tot = ex[0]
    for v in ex[1:]:
        tot = tot + v
    for j in range(PEER_TOPK):
        e_ref[0, j:j + 1, :] = jnp.sum(jnp.where(riota == slot[j], cid, 0), axis=0, keepdims=True)
        gate_ref[0, j:j + 1, :] = ex[j] / tot


def peer_route(x, mod, norm_g, wq, keys, mod_row):
    n, d = x.shape
    tt = ROW_TILE
    return pl.pallas_call(
        _peer_route_kernel,
        grid=(n // tt, PEER_HEADS),
        in_specs=[pl.BlockSpec((tt, d), lambda i, h: (i, 0)),
                  pl.BlockSpec((1, 6, d), lambda i, h: (mod_row(i), 0, 0)),
                  pl.BlockSpec((1, d), lambda i, h: (0, 0)),
                  pl.BlockSpec((d, PEER_DK), lambda i, h: (0, h)),
                  pl.BlockSpec((2, N_KEYS, PEER_DK // 2), lambda i, h: (0, 0, 0))],
        out_specs=[pl.BlockSpec((tt, d), lambda i, h: (i, 0)),
                   pl.BlockSpec((1, PEER_TOPK, tt), lambda i, h: (h, 0, i)),
                   pl.BlockSpec((1, PEER_TOPK, tt), lambda i, h: (h, 0, i))],
        out_shape=[jax.ShapeDtypeStruct((n, d), F32),
                   jax.ShapeDtypeStruct((PEER_HEADS, PEER_TOPK, n), I32),
                   jax.ShapeDtypeStruct((PEER_HEADS, PEER_TOPK, n), F32)],
        scratch_shapes=[pltpu.VMEM((tt, d), BF16), pltpu.VMEM((CAND_ROWS, tt), F32),
                        pltpu.VMEM((CAND_ROWS, tt), I32)],
        compiler_params=_params(("parallel", "arbitrary")),
    )(x, mod, norm_g, wq, keys)


ROW_SUB = D_MODEL // LANE
ISSUE_UNROLL = 8
SUBLANES = 8
FOLD_ORDER = (0, 4, 2, 6, 1, 5, 3, 7)


def _fold(a, b, shift, first):
    return (jnp.where(first, a, pltpu.roll(b, shift, 0))
            + jnp.where(first, pltpu.roll(a, SUBLANES - shift, 0), b))


def _unpack_pair(words):
    lo = pltpu.bitcast(jnp.left_shift(words, 16), F32)
    hi = pltpu.bitcast(jnp.bitwise_and(words, jnp.uint32(0xFFFF0000)), F32)
    return lo, hi


def _peer_gather_kernel(ids_ref, idn_ref, hf_ref, gt_ref, xn_ref, gate_ref, tab_hbm, o_ref,
                        tbuf, wbuf, sem, *, ngroups):
    i = pl.program_id(0)
    slot = i % 2
    nslot = 1 - slot
    nrows = GATHER_TOKENS * PEER_PAIRS

    def start_row(ids, r, s, priority):
        pltpu.make_async_copy(tab_hbm.at[ids[0, 0, r]], tbuf.at[s, r], sem.at[s]).start(priority=priority)

    def wait_slot(s):
        pltpu.make_async_copy(tab_hbm.at[pl.ds(0, nrows)], tbuf.at[s], sem.at[s]).wait()

    @pl.when(i == 0)
    def _first_group():
        def body(g, carry):
            for j in range(ISSUE_UNROLL):
                start_row(ids_ref, g * ISSUE_UNROLL + j, slot, j % 2)
            return carry
        lax.fori_loop(0, nrows // ISSUE_UNROLL, body, 0)

    wait_slot(slot)

    sub = lax.broadcasted_iota(I32, (SUBLANES, LANE), 0)
    first4, first2, first1 = sub < 4, sub % 4 < 2, sub % 2 < 1
    ngrp = PEER_PAIRS // SUBLANES
    early = 4
    for t in range(GATHER_TOKENS):
        x_lo = hf_ref[t, :SUBLANES]
        x_hi = hf_ref[t, SUBLANES:]
        base = t * PEER_PAIRS
        for j in range(ngrp):
            for q in range(early):
                start_row(idn_ref, base + j * early + q, nslot, q % 2)
            r = []
            for p in FOLD_ORDER:
                lo, hi = _unpack_pair(tbuf[slot, base + j * SUBLANES + p, 0])
                r.append(lo * x_lo + hi * x_hi)
            l1 = [_fold(r[0], r[1], 4, first4), _fold(r[2], r[3], 4, first4),
                  _fold(r[4], r[5], 4, first4), _fold(r[6], r[7], 4, first4)]
            l2 = [_fold(l1[0], l1[1], 2, first2), _fold(l1[2], l1[3], 2, first2)]
            y = _fold(l2[0], l2[1], 1, first1)
            rows = slice(j * SUBLANES, (j + 1) * SUBLANES)
            a = jnp.sum(y, axis=1, keepdims=True)
            w = gt_ref[0, rows, t:t + 1] * _gelu(a)
            wbuf[rows, :] = jnp.broadcast_to(w, (SUBLANES, LANE))
        nacc = 2
        acc_lo = [jnp.zeros((SUBLANES, LANE), F32) for _ in range(nacc)]
        acc_hi = [jnp.zeros((SUBLANES, LANE), F32) for _ in range(nacc)]
        late0 = ngrp * early
        for k in range(PEER_PAIRS):
            if k % 2 == 0 and late0 + k // 2 < PEER_PAIRS:
                start_row(idn_ref, base + late0 + k // 2, nslot, (k // 2) % 2)
            lo, hi = _unpack_pair(tbuf[slot, base + k, 1])
            wk = wbuf[k:k + 1, :]
            acc_lo[k % nacc] = acc_lo[k % nacc] + lo * wk
            acc_hi[k % nacc] = acc_hi[k % nacc] + hi * wk
        o_ref[t, :SUBLANES] = xn_ref[t, :SUBLANES] + gate_ref[0, :SUBLANES] * (acc_lo[0] + acc_lo[1])
        o_ref[t, SUBLANES:] = xn_ref[t, SUBLANES:] + gate_ref[0, SUBLANES:] * (acc_hi[0] + acc_hi[1])

    @pl.when(i == ngroups - 1)
    def _drain():
        wait_slot(nslot)


def pack_expert_tables(u_tab, v_tab):
    def pack(tab):
        bits = lax.bitcast_convert_type(tab.astype(BF16), jnp.uint16).astype(jnp.uint32)
        half = tab.shape[1] // 2
        return jnp.bitwise_or(bits[:, :half], jnp.left_shift(bits[:, half:], 16))
    words = jnp.stack([pack(u_tab), pack(v_tab)], axis=1)
    return words.reshape(u_tab.shape[0], 2, SUBLANES, LANE)


def peer_gather(ids, hf, gates_t, xn, gate_ffn, table, mod_row_group):
    n, d = xn.shape
    gt = GATHER_TOKENS
    ngroups = n // gt
    nrows = gt * PEER_PAIRS
    slab = (ROW_SUB, LANE)
    tok_spec = pl.BlockSpec((gt,) + slab, lambda i: (i, 0, 0))
    out = pl.pallas_call(
        functools.partial(_peer_gather_kernel, ngroups=ngroups),
        grid=(ngroups,),
        in_specs=[pl.BlockSpec((1, 1, nrows), lambda i: (i, 0, 0), memory_space=pltpu.SMEM),
                  pl.BlockSpec((1, 1, nrows), lambda i: (jnp.minimum(i + 1, ngroups - 1), 0, 0),
                               memory_space=pltpu.SMEM),
                  tok_spec,
                  pl.BlockSpec((1, PEER_PAIRS, gt), lambda i: (i, 0, 0)),
                  tok_spec,
                  pl.BlockSpec((1,) + slab, lambda i: (mod_row_group(i), 0, 0)),
                  pl.BlockSpec(memory_space=pl.ANY)],
        out_specs=tok_spec,
        out_shape=jax.ShapeDtypeStruct((n,) + slab, F32),
        scratch_shapes=[pltpu.VMEM((2, nrows, 2, SUBLANES, LANE), jnp.uint32),
                        pltpu.VMEM((PEER_PAIRS, LANE), F32), pltpu.SemaphoreType.DMA((2,))],
        compiler_params=_params(("arbitrary",)),
    )(ids, ids, hf.reshape((n,) + slab), gates_t, xn.reshape((n,) + slab),
      gate_ffn.reshape((-1,) + slab), table)
    return out.reshape(n, d)


def _final_norm_kernel(x_ref, g_ref, o_ref):
    x = x_ref[...]
    o_ref[0] = x * lax.rsqrt(jnp.mean(x * x, axis=-1, keepdims=True) + EPS) * g_ref[...]


def final_norm(x, g, batch, seq, ctx):
    n, d = x.shape
    tb = ctx + seq
    nrb = tb // ROW_TILE
    cb = ctx // ROW_TILE
    return pl.pallas_call(
        _final_norm_kernel,
        grid=(batch, seq // ROW_TILE),
        in_specs=[pl.BlockSpec((ROW_TILE, d), lambda b, i: (b * nrb + cb + i, 0)),
                  pl.BlockSpec((1, d), lambda b, i: (0, 0))],
        out_specs=pl.BlockSpec((1, ROW_TILE, d), lambda b, i: (b, i, 0)),
        out_shape=jax.ShapeDtypeStruct((batch, seq, d), F32),
        compiler_params=_params(("parallel", "parallel")),
    )(x, g)


def kernel(x, c, ctx, c_ctx, ada_w, ada_b, norm_mix, norm_ffn, w_in, b_gate, sgu_norm, sgu_w, sgu_b,
           qk_conv_w, mlstm_norm, pool_w, pool_scale, w_out, peer_wq, peer_keys, peer_u, peer_v,
           norm_final):
    batch, seq, d = x.shape
    ctx_len = ctx.shape[1]
    depth = ada_w.shape[0]
    tb = ctx_len + seq
    n = batch * tb
    assert d == D_MODEL and w_in.shape[2] == OFF_G + N_GATES
    assert ctx_len % ROW_TILE == 0 and seq % ROW_TILE == 0 and seq % GRID_W == 0
    assert n % GATHER_TOKENS == 0 and ROW_TILE % GATHER_TOKENS == 0
    blocks_per_batch = tb // ROW_TILE
    ctx_blocks = ctx_len // ROW_TILE
    nchunks = tb // CHUNK

    def mod_row(i):
        return jnp.where(i % blocks_per_batch < ctx_blocks, batch, i // blocks_per_batch)

    def mod_row_group(i):
        return mod_row(i // (ROW_TILE // GATHER_TOKENS))

    cond_rows = -(-(batch + 1) // 8) * 8
    cond = jnp.zeros((cond_rows, d), F32).at[:batch].set(c).at[batch].set(c_ctx)
    mods = ada_modulation_all(cond, ada_w, ada_b).reshape(depth, cond_rows, 6, d)

    xs = jnp.concatenate([ctx, x], axis=1).reshape(n, d)
    lane_pad = LANE - N_GATES
    for l in range(depth):
        mod = mods[l]
        w_main = w_in[l, :, :OFF_G].astype(BF16)
        w_gate = jnp.pad(w_in[l, :, OFF_G:], ((0, 0), (0, lane_pad))).astype(BF16)
        p, graw = in_proj(xs, mod, norm_mix[l][None], w_main, w_gate, mod_row)

        y_a = sgu_mixer(p, sgu_norm[l][None], sgu_w[l].astype(BF16), sgu_b[l].T)

        gp = gate_prep(graw[0], jnp.pad(b_gate[l], (0, lane_pad))[None])
        gp = gp[:, :N_GATES].reshape(n, 2, 2, MLSTM_HEADS)
        gcol = gp.transpose(3, 0, 1, 2).reshape(MLSTM_HEADS, n, 4)
        grow = gcol.reshape(MLSTM_HEADS, n // CHUNK, CHUNK, 4).transpose(0, 1, 3, 2)
        qk = qk_conv(p, qk_conv_w[l], blocks_per_batch, ctx_blocks)
        y_b = mlstm_mixer(p, qk, gcol, grow, mlstm_norm[l][None], batch, tb, ctx_len // CHUNK)

        y_c = pool_mixer(p, pool_w[l].astype(BF16), pool_scale[l].reshape(POOL_GROUPS, 1, POOL_GC),
                         batch, tb, ctx_len, seq)

        xn = out_proj(xs, y_a, y_b, y_c, mod, w_out[l].astype(BF16), mod_row)

        hf, e, gates = peer_route(xn, mod, norm_ffn[l][None], peer_wq[l].astype(BF16),
                                  peer_keys[l].astype(BF16), mod_row)
        ids = e.reshape(PEER_PAIRS, n).T.reshape(n // GATHER_TOKENS, 1, GATHER_TOKENS * PEER_PAIRS)
        gates_t = gates.reshape(PEER_PAIRS, n // GATHER_TOKENS, GATHER_TOKENS).transpose(1, 0, 2)
        xs = peer_gather(ids, hf, gates_t, xn, mod[:, 5], pack_expert_tables(peer_u[l], peer_v[l]),
                         mod_row_group)

    return final_norm(xs, norm_final[None], batch, seq, ctx_len)
```

```python
import functools

import jax
import jax.numpy as jnp
from jax import lax
from jax.experimental import pallas as pl
from jax.experimental.pallas import tpu as pltpu

F32 = jnp.float32
BF16 = jnp.bfloat16
I32 = jnp.int32

EPS = 1e-6
GRID_W = 64
GRID_W_LOG2 = 6
D_MODEL = 2048
D_SGU = D_MODEL // 4
SGU_HEADS = 4
SGU_HD = D_SGU // SGU_HEADS
CHUNK = 128
D_MLSTM = D_MODEL // 2
MLSTM_HEADS = 4
MLSTM_HD = D_MLSTM // MLSTM_HEADS
D_POOL = D_MODEL // 4
POOL_GROUPS = 4
POOL_GC = D_POOL // POOL_GROUPS
N_GATES = 2 * 2 * MLSTM_HEADS
OFF_U = 0
OFF_V = OFF_U + D_SGU
OFF_P = OFF_V + D_SGU
OFF_Q = OFF_P + D_POOL
OFF_O = OFF_Q + D_MLSTM
OFF_K = OFF_O + D_MLSTM
OFF_VM = OFF_K + D_MLSTM
OFF_G = OFF_VM + D_MLSTM
N_KEYS = 128
PEER_HEADS = 8
PEER_TOPK = 16
PEER_DK = 256
PEER_PAIRS = PEER_HEADS * PEER_TOPK

LANE = 128
ROW_TILE = 256
POOL_HALO_TILES = -(-(GRID_W << (POOL_GROUPS - 1)) // ROW_TILE)
GATHER_TOKENS = 8
GATHER_SLOTS = 3
VMEM_LIMIT = 56 * 1024 * 1024

CAND_PAIRS = [(a, b) for a in range(PEER_TOPK) for b in range(PEER_TOPK)
              if (a + 1) * (b + 1) <= PEER_TOPK]
CAND_ROWS = 56


def _params(sem, vmem=VMEM_LIMIT):
    return pltpu.CompilerParams(dimension_semantics=sem, vmem_limit_bytes=vmem)


def _norm_mod(x, g, shift, scale):
    ms = jnp.mean(x * x, axis=-1, keepdims=True)
    y = x * lax.rsqrt(ms + EPS)
    return (y * g) * (1.0 + scale) + shift


def _gelu(x):
    return jax.nn.gelu(x, approximate=True)


def _sigmoid(x):
    return 1.0 / (1.0 + jnp.exp(-x))


def _ada_kernel(cond_ref, w_ref, b_ref, o_ref):
    a = cond_ref[...]
    a = (a * _sigmoid(a)).astype(BF16)
    o_ref[0] = jnp.dot(a, w_ref[0].astype(BF16), preferred_element_type=F32) + b_ref[0]


def ada_modulation_all(cond, ada_w, ada_b):
    depth, d, d6 = ada_w.shape
    rows = cond.shape[0]
    tn = 1024
    return pl.pallas_call(
        _ada_kernel,
        grid=(depth, d6 // tn),
        in_specs=[pl.BlockSpec((rows, d), lambda l, j: (0, 0)),
                  pl.BlockSpec((1, d, tn), lambda l, j: (l, 0, j)),
                  pl.BlockSpec((1, 1, tn), lambda l, j: (l, 0, j))],
        out_specs=pl.BlockSpec((1, rows, tn), lambda l, j: (l, 0, j)),
        out_shape=jax.ShapeDtypeStruct((depth, rows, d6), F32),
        compiler_params=_params(("parallel", "parallel")),
    )(cond, ada_w, ada_b.reshape(depth, 1, d6))


def _in_proj_kernel(x_ref, mod_ref, g_ref, w_ref, wg_ref, p_ref, gate_ref, *, ncols):
    h = _norm_mod(x_ref[...], g_ref[...], mod_ref[0, 0:1, :], mod_ref[0, 1:2, :]).astype(BF16)
    for j in range(ncols // 256):
        sl = slice(j * 256, (j + 1) * 256)
        p_ref[:, sl] = jnp.dot(h, w_ref[:, sl], preferred_element_type=F32).astype(BF16)
    gate_ref[0] = jnp.dot(h, wg_ref[...], preferred_element_type=F32)


def in_proj(x, mod, norm_g, w_main, w_gate, mod_row):
    n, d = x.shape
    ncol_blocks = 2
    ncols = OFF_G // ncol_blocks
    return pl.pallas_call(
        functools.partial(_in_proj_kernel, ncols=ncols),
        grid=(ncol_blocks, n // ROW_TILE),
        in_specs=[pl.BlockSpec((ROW_TILE, d), lambda c, i: (i, 0)),
                  pl.BlockSpec((1, 6, d), lambda c, i: (mod_row(i), 0, 0)),
                  pl.BlockSpec((1, d), lambda c, i: (0, 0)),
                  pl.BlockSpec((d, ncols), lambda c, i: (0, c)),
                  pl.BlockSpec((d, LANE), lambda c, i: (0, 0))],
        out_specs=[pl.BlockSpec((ROW_TILE, ncols), lambda c, i: (i, c)),
                   pl.BlockSpec((1, ROW_TILE, LANE), lambda c, i: (c, i, 0))],
        out_shape=[jax.ShapeDtypeStruct((n, OFF_G), BF16),
                   jax.ShapeDtypeStruct((ncol_blocks, n, LANE), F32)],
        compiler_params=_params(("arbitrary", "arbitrary")),
    )(x, mod, norm_g, w_main, w_gate)


def _sgu_kernel(u_ref, v_ref, g_ref, ws_ref, bs_ref, y_ref):
    u = _gelu(u_ref[...].astype(F32))
    v = _gelu(v_ref[...].astype(F32))
    ms = jnp.mean(v * v, axis=-1, keepdims=True)
    v = (v * lax.rsqrt(ms + EPS) * g_ref[...]).astype(BF16)
    for c in range(ROW_TILE // CHUNK):
        rows = slice(c * CHUNK, (c + 1) * CHUNK)
        for h in range(SGU_HEADS):
            cols = slice(h * SGU_HD, (h + 1) * SGU_HD)
            mixed = jnp.dot(ws_ref[h], v[rows, cols], preferred_element_type=F32) + bs_ref[:, h:h + 1]
            y_ref[rows, cols] = (u[rows, cols] * mixed).astype(BF16)


def sgu_mixer(p, sgu_norm, sgu_w, sgu_bt):
    n = p.shape[0]
    return pl.pallas_call(
        _sgu_kernel,
        grid=(n // ROW_TILE,),
        in_specs=[pl.BlockSpec((ROW_TILE, D_SGU), lambda i: (i, OFF_U // D_SGU)),
                  pl.BlockSpec((ROW_TILE, D_SGU), lambda i: (i, OFF_V // D_SGU)),
                  pl.BlockSpec((1, D_SGU), lambda i: (0, 0)),
                  pl.BlockSpec((SGU_HEADS, CHUNK, CHUNK), lambda i: (0, 0, 0)),
                  pl.BlockSpec((CHUNK, SGU_HEADS), lambda i: (0, 0))],
        out_specs=pl.BlockSpec((ROW_TILE, D_SGU), lambda i: (i, 0)),
        out_shape=jax.ShapeDtypeStruct((n, D_SGU), BF16),
        compiler_params=_params(("parallel",)),
    )(p, p, sgu_norm, sgu_w, sgu_bt)


def _gate_prep_kernel(g_ref, bias_ref, o_ref):
    g = g_ref[...] + bias_ref[...]
    lf = jnp.minimum(g, 0.0) - jnp.log(1.0 + jnp.exp(-jnp.abs(g)))
    row = lax.broadcasted_iota(I32, (CHUNK, CHUNK), 0)
    col = lax.broadcasted_iota(I32, (CHUNK, CHUNK), 1)
    lower = (col <= row).astype(F32)
    upper = (col >= row).astype(F32)
    b_fwd = jnp.dot(lower, lf, preferred_element_type=F32, precision=lax.Precision.HIGHEST)
    b_bwd = jnp.dot(upper, lf, preferred_element_type=F32, precision=lax.Precision.HIGHEST)
    lane = lax.broadcasted_iota(I32, (1, LANE), 1)
    is_input_gate = (lane // MLSTM_HEADS) % 2 == 0
    is_fwd = lane < 2 * MLSTM_HEADS
    o_ref[...] = jnp.where(is_input_gate, g, jnp.where(is_fwd, b_fwd, b_bwd))


def gate_prep(graw, bias):
    n = graw.shape[0]
    return pl.pallas_call(
        _gate_prep_kernel,
        grid=(n // CHUNK,),
        in_specs=[pl.BlockSpec((CHUNK, LANE), lambda i: (i, 0)),
                  pl.BlockSpec((1, LANE), lambda i: (0, 0))],
        out_specs=pl.BlockSpec((CHUNK, LANE), lambda i: (i, 0)),
        out_shape=jax.ShapeDtypeStruct((n, LANE), F32),
        compiler_params=_params(("parallel",)),
    )(graw, bias)


HALO = 16


def _qk_conv_kernel(xm_ref, xp_ref, xn_ref, w_ref, o_ref, *, blocks_per_batch, ctx_blocks):
    i = pl.program_id(0)
    j = pl.program_id(1)
    x = xm_ref[...].astype(F32)
    tm = x.shape[0]
    prev_row = xp_ref[...].astype(F32)[HALO - 1:HALO, :]
    next_row = xn_ref[...].astype(F32)[0:1, :]
    ib = i % blocks_per_batch
    at_start = jnp.logical_or(ib == 0, ib == ctx_blocks)
    at_end = jnp.logical_or(ib == ctx_blocks - 1, ib == blocks_per_batch - 1)
    prev_row = jnp.where(at_start, 0.0, prev_row)
    next_row = jnp.where(at_end, 0.0, next_row)
    rows = lax.broadcasted_iota(I32, (tm, 1), 0)
    x_prev = jnp.where(rows == 0, prev_row, pltpu.roll(x, 1, 0))
    x_next = jnp.where(rows == tm - 1, next_row, pltpu.roll(x, tm - 1, 0))
    y = w_ref[0:1, :] * x_prev + w_ref[1:2, :] * x + w_ref[2:3, :] * x_next
    y = y * _sigmoid(y)
    y = y * jnp.where(j >= 2, MLSTM_HD ** -0.5, 1.0)
    o_ref[...] = y.astype(BF16)


def qk_conv(p, conv_w, blocks_per_batch, ctx_blocks):
    n = p.shape[0]
    tc = 512
    nhalo = n // HALO
    per = ROW_TILE // HALO

    def col(j):
        return jnp.where(j < 2, OFF_Q // tc + j, OFF_K // tc + j - 2)

    return pl.pallas_call(
        functools.partial(_qk_conv_kernel, blocks_per_batch=blocks_per_batch, ctx_blocks=ctx_blocks),
        grid=(n // ROW_TILE, 2 * D_MLSTM // tc),
        in_specs=[pl.BlockSpec((ROW_TILE, tc), lambda i, j: (i, col(j))),
                  pl.BlockSpec((HALO, tc), lambda i, j: (jnp.maximum(i * per - 1, 0), col(j))),
                  pl.BlockSpec((HALO, tc), lambda i, j: (jnp.minimum((i + 1) * per, nhalo - 1), col(j))),
                  pl.BlockSpec((3, tc), lambda i, j: (0, j))],
        out_specs=pl.BlockSpec((ROW_TILE, tc), lambda i, j: (i, j)),
        out_shape=jax.ShapeDtypeStruct((n, 2 * D_MLSTM), BF16),
        compiler_params=_params(("parallel", "parallel")),
    )(p, p, p, conv_w)


def _mlstm_kernel(q_ref, k_ref, v_ref, o_ref, gc_ref, gr_ref, nrm_ref, y_ref,
                  hf_ref, hb_ref, c_ref, n_ref, m_ref, *, nchunks, nctx):
    c_ref[...] = jnp.zeros_like(c_ref)
    n_ref[...] = jnp.zeros_like(n_ref)
    m_ref[...] = jnp.zeros_like(m_ref)
    row = lax.broadcasted_iota(I32, (CHUNK, CHUNK), 0)
    col = lax.broadcasted_iota(I32, (CHUNK, CHUNK), 1)
    seen = (row >= col, row <= col)
    h_refs = (hf_ref, hb_ref)

    def step(d, c):
        r0 = pl.multiple_of(c * CHUNK, CHUNK)
        rows = pl.ds(r0, CHUNK)
        q = q_ref[rows, :]
        k = k_ref[rows, :]
        v = v_ref[rows, :]
        gcol = gc_ref[0, rows, :]
        grow = gr_ref[0, c]
        ig_c, b_c = gcol[:, 2 * d:2 * d + 1], gcol[:, 2 * d + 1:2 * d + 2]
        ig_r, b_r = grow[2 * d:2 * d + 1, :], grow[2 * d + 1:2 * d + 2, :]
        b_last = b_c[CHUNK - 1:CHUNK, :] if d == 0 else b_c[0:1, :]
        m = m_ref[d, 0:1, 0:1]
        g_c = b_last - b_c + ig_c
        m_new = jnp.maximum(b_last + m, jnp.max(g_c, axis=0, keepdims=True))
        wk_c = jnp.exp(g_c - m_new)
        decay = jnp.exp(b_last + m - m_new)

        log_w = jnp.where(seen[d], b_c - b_r + ig_r, -jnp.inf)
        inter = b_c + m
        m_t = jnp.maximum(inter, jnp.max(log_w, axis=1, keepdims=True))
        w_prev = jnp.exp(inter - m_t)
        s = lax.dot_general(q, k, (((1,), (1,)), ((), ())), preferred_element_type=F32)
        s = s * jnp.exp(log_w - m_t)
        ct = c_ref[d]
        nvec = n_ref[d]
        num = (jnp.dot(s.astype(BF16), v, preferred_element_type=F32)
               + w_prev * jnp.dot(q, ct.astype(BF16), preferred_element_type=F32))
        qf = q.astype(F32)
        kf = k.astype(F32)
        den = jnp.sum(s, axis=1, keepdims=True) + w_prev * jnp.sum(qf * nvec, axis=1, keepdims=True)
        h_refs[d][rows, :] = num / jnp.maximum(jnp.abs(den), jnp.exp(-m_t))

        wv = (v.astype(F32) * wk_c).astype(BF16)
        c_ref[d] = decay * ct + lax.dot_general(k, wv, (((0,), (0,)), ((), ())),
                                                preferred_element_type=F32)
        n_ref[d] = decay * nvec + jnp.sum(kf * wk_c, axis=0, keepdims=True)
        m_ref[d] = jnp.broadcast_to(m_new, (1, LANE))

    def body(i, carry):
        step(0, i)
        step(1, jnp.where(i < nctx, nctx - 1 - i, nchunks - 1 - (i - nctx)))
        return carry

    lax.fori_loop(0, nchunks, body, 0)

    def finish(c, carry):
        rows = pl.ds(pl.multiple_of(c * CHUNK, CHUNK), CHUNK)
        h = hf_ref[rows, :] + hb_ref[rows, :]
        hn = h * lax.rsqrt(jnp.mean(h * h, axis=-1, keepdims=True) + EPS) * nrm_ref[...]
        y_ref[rows, :] = (_sigmoid(o_ref[rows, :].astype(F32)) * hn).astype(BF16)
        return carry

    lax.fori_loop(0, nchunks, finish, 0)


def mlstm_mixer(p, qk, gcol, grow, mlstm_norm, batch, tb, nctx):
    n = p.shape[0]
    nchunks = tb // CHUNK
    hd = MLSTM_HD
    return pl.pallas_call(
        functools.partial(_mlstm_kernel, nchunks=nchunks, nctx=nctx),
        grid=(batch, MLSTM_HEADS),
        in_specs=[pl.BlockSpec((tb, hd), lambda b, h: (b, h)),
                  pl.BlockSpec((tb, hd), lambda b, h: (b, MLSTM_HEADS + h)),
                  pl.BlockSpec((tb, hd), lambda b, h: (b, OFF_VM // hd + h)),
                  pl.BlockSpec((tb, hd), lambda b, h: (b, OFF_O // hd + h)),
                  pl.BlockSpec((1, tb, 4), lambda b, h: (h, b, 0)),
                  pl.BlockSpec((1, nchunks, 4, CHUNK), lambda b, h: (h, b, 0, 0)),
                  pl.BlockSpec((1, hd), lambda b, h: (0, h))],
        out_specs=pl.BlockSpec((tb, hd), lambda b, h: (b, h)),
        out_shape=jax.ShapeDtypeStruct((n, D_MLSTM), BF16),
        scratch_shapes=[pltpu.VMEM((tb, hd), F32), pltpu.VMEM((tb, hd), F32),
                        pltpu.VMEM((2, hd, hd), F32), pltpu.VMEM((2, 1, hd), F32),
                        pltpu.VMEM((2, 1, LANE), F32)],
        compiler_params=_params(("parallel", "parallel")),
    )(qk, qk, p, p, gcol, grow, mlstm_norm)


def _pool_kernel(xfull_ref, xrow_ref, w_ref, sc_ref, y_ref, m_ref, inv_ref, *, tb, ctx, seq):
    g = pl.program_id(0)
    i = pl.program_id(1)
    b = pl.program_id(2)
    tm = xrow_ref.shape[0]
    half = jnp.left_shift(1, g)
    win = 2 * half

    def grid_coords(pos):
        is_ctx = pos < ctx
        lat = pos - ctx
        r = jnp.where(is_ctx, 0, jnp.right_shift(lat, GRID_W_LOG2))
        c = jnp.where(is_ctx, pos, jnp.bitwise_and(lat, GRID_W - 1))
        return is_ctx, r, c

    band = m_ref.shape[1]
    first_tile = jnp.clip(i - POOL_HALO_TILES, 0, (tb - band) // tm)
    s0 = pl.multiple_of(first_tile * tm, tm)

    @pl.when(b == 0)
    def _build_window_matrix():
        t = i * tm + lax.broadcasted_iota(I32, (tm, 1), 0)
        s = s0 + lax.broadcasted_iota(I32, (1, band), 1)
        ctx_t, r_t, c_t = grid_coords(t)
        ctx_s, r_s, c_s = grid_coords(s)
        dr = r_s - r_t + half
        dc = c_s - c_t + half
        inside = (ctx_t == ctx_s) & (dr >= 0) & (dr < win) & (dc >= 0) & (dc < win)
        m_ref[...] = jnp.where(inside, 1.0, 0.0).astype(BF16)
        nrows = jnp.where(ctx_t, 1, seq // GRID_W)
        ncols = jnp.where(ctx_t, ctx, GRID_W)
        cnt_r = jnp.minimum(r_t - half + win, nrows) - jnp.maximum(r_t - half, 0)
        cnt_c = jnp.minimum(c_t - half + win, ncols) - jnp.maximum(c_t - half, 0)
        inv_ref[...] = 1.0 / (cnt_r * cnt_c).astype(F32)

    tot = jnp.dot(m_ref[...], xfull_ref[pl.ds(s0, band), :], preferred_element_type=F32)
    d = (tot * inv_ref[...] - xrow_ref[...].astype(F32)).astype(BF16)
    y = jnp.dot(d, w_ref[0], preferred_element_type=F32) * sc_ref[0]
    y_ref[...] = y.astype(BF16)


def pool_mixer(p, pool_w, pool_scale, batch, tb, ctx, seq):
    n = p.shape[0]
    nrb = tb // ROW_TILE
    c0 = OFF_P // POOL_GC
    band = min(2 * POOL_HALO_TILES + 1, nrb) * ROW_TILE
    return pl.pallas_call(
        functools.partial(_pool_kernel, tb=tb, ctx=ctx, seq=seq),
        grid=(POOL_GROUPS, nrb, batch),
        in_specs=[pl.BlockSpec((tb, POOL_GC), lambda g, i, b: (b, c0 + g)),
                  pl.BlockSpec((ROW_TILE, POOL_GC), lambda g, i, b: (b * nrb + i, c0 + g)),
                  pl.BlockSpec((1, POOL_GC, POOL_GC), lambda g, i, b: (g, 0, 0)),
                  pl.BlockSpec((1, 1, POOL_GC), lambda g, i, b: (g, 0, 0))],
        out_specs=pl.BlockSpec((ROW_TILE, POOL_GC), lambda g, i, b: (b * nrb + i, g)),
        out_shape=jax.ShapeDtypeStruct((n, D_POOL), BF16),
        scratch_shapes=[pltpu.VMEM((ROW_TILE, band), BF16), pltpu.VMEM((ROW_TILE, 1), F32)],
        compiler_params=_params(("arbitrary", "arbitrary", "arbitrary")),
    )(p, p, pool_w, pool_scale)


def _out_proj_kernel(x_ref, ya_ref, yb_ref, yc_ref, mod_ref, w_ref, o_ref):
    ya, yb, yc = ya_ref[...], yb_ref[...], yc_ref[...]
    k1 = D_SGU
    k2 = D_SGU + D_MLSTM
    for j in range(D_MODEL // 256):
        sl = slice(j * 256, (j + 1) * 256)
        acc = jnp.dot(ya, w_ref[0:k1, sl], preferred_element_type=F32)
        acc += jnp.dot(yb, w_ref[k1:k2, sl], preferred_element_type=F32)
        acc += jnp.dot(yc, w_ref[k2:D_MODEL, sl], preferred_element_type=F32)
        o_ref[:, sl] = x_ref[:, sl] + mod_ref[0, 2:3, sl] * acc


def out_proj(x, ya, yb, yc, mod, w_out, mod_row):
    n, d = x.shape
    return pl.pallas_call(
        _out_proj_kernel,
        grid=(n // ROW_TILE,),
        in_specs=[pl.BlockSpec((ROW_TILE, d), lambda i: (i, 0)),
                  pl.BlockSpec((ROW_TILE, D_SGU), lambda i: (i, 0)),
                  pl.BlockSpec((ROW_TILE, D_MLSTM), lambda i: (i, 0)),
                  pl.BlockSpec((ROW_TILE, D_POOL), lambda i: (i, 0)),
                  pl.BlockSpec((1, 6, d), lambda i: (mod_row(i), 0, 0)),
                  pl.BlockSpec((d, d), lambda i: (0, 0))],
        out_specs=pl.BlockSpec((ROW_TILE, d), lambda i: (i, 0)),
        out_shape=jax.ShapeDtypeStruct((n, d), F32),
        compiler_params=_params(("parallel",)),
    )(x, ya, yb, yc, mod, w_out)


def _topk_rows(vals, k):
    nrows = vals.shape[0]
    iota = lax.broadcasted_iota(I32, vals.shape, 0)
    out_v, out_i = [], []
    for _ in range(k):
        mx = jnp.max(vals, axis=0, keepdims=True)
        am = jnp.min(jnp.where(vals == mx, iota, nrows), axis=0, keepdims=True)
        out_v.append(mx)
        out_i.append(am)
        vals = jnp.where(iota == am, -jnp.inf, vals)
    return out_v, out_i


def _peer_route_kernel(x_ref, mod_ref, g_ref, wq_ref, keys_ref, hf_ref, e_ref, gate_ref,
                       hb_ref, cand_ref, cid_ref):
    h = pl.program_id(1)

    @pl.when(h == 0)
    def _norm():
        hf = _norm_mod(x_ref[...], g_ref[...], mod_ref[0, 3:4, :], mod_ref[0, 4:5, :])
        hf_ref[...] = hf
        hb_ref[...] = hf.astype(BF16)

    q = jnp.dot(hb_ref[...], wq_ref[...], preferred_element_type=F32).astype(BF16)
    half = PEER_DK // 2
    nt = (((1,), (1,)), ((), ()))
    scores1 = lax.dot_general(keys_ref[0], q[:, :half], nt, preferred_element_type=F32)
    scores2 = lax.dot_general(keys_ref[1], q[:, half:], nt, preferred_element_type=F32)
    npad = CAND_ROWS - len(CAND_PAIRS)
    for c in range(q.shape[0] // LANE):
        lanes = slice(c * LANE, (c + 1) * LANE)
        s1, i1 = _topk_rows(scores1[:, lanes], PEER_TOPK)
        s2, i2 = _topk_rows(scores2[:, lanes], PEER_TOPK)
        for r, (a, b) in enumerate(CAND_PAIRS):
            cand_ref[r:r + 1, lanes] = s1[a] + s2[b]
            cid_ref[r:r + 1, lanes] = i1[a] * N_KEYS + i2[b]
        cand_ref[len(CAND_PAIRS):, lanes] = jnp.full((npad, LANE), -jnp.inf, F32)
        cid_ref[len(CAND_PAIRS):, lanes] = jnp.zeros((npad, LANE), I32)
        sc, slot = _topk_rows(cand_ref[:, lanes], PEER_TOPK)
        cid = cid_ref[:, lanes]
        riota = lax.broadcasted_iota(I32, cid.shape, 0)
        ex = [jnp.exp(v - sc[0]) for v in sc]
        tot = ex[0]
        for v in ex[1:]:
            tot = tot + v
        for j in range(PEER_TOPK):
            e_ref[0, j:j + 1, lanes] = jnp.sum(jnp.where(riota == slot[j], cid, 0), axis=0, keepdims=True)
            gate_ref[0, j:j + 1, lanes] = ex[j] / tot


def peer_route(x, mod, norm_g, wq, keys, mod_row):
    n, d = x.shape
    tt = ROW_TILE
    return pl.pallas_call(
        _peer_route_kernel,
        grid=(n // tt, PEER_HEADS),
        in_specs=[pl.BlockSpec((tt, d), lambda i, h: (i, 0)),
                  pl.BlockSpec((1, 6, d), lambda i, h: (mod_row(i), 0, 0)),
                  pl.BlockSpec((1, d), lambda i, h: (0, 0)),
                  pl.BlockSpec((d, PEER_DK), lambda i, h: (0, h)),
                  pl.BlockSpec((2, N_KEYS, PEER_DK // 2), lambda i, h: (0, 0, 0))],
        out_specs=[pl.BlockSpec((tt, d), lambda i, h: (i, 0)),
                   pl.BlockSpec((1, PEER_TOPK, tt), lambda i, h: (h, 0, i)),
                   pl.BlockSpec((1, PEER_TOPK, tt), lambda i, h: (h, 0, i))],
        out_shape=[jax.ShapeDtypeStruct((n, d), F32),
                   jax.ShapeDtypeStruct((PEER_HEADS, PEER_TOPK, n), I32),
                   jax.ShapeDtypeStruct((PEER_HEADS, PEER_TOPK, n), F32)],
        scratch_shapes=[pltpu.VMEM((tt, d), BF16), pltpu.VMEM((CAND_ROWS, tt), F32),
                        pltpu.VMEM((CAND_ROWS, tt), I32)],
        compiler_params=_params(("parallel", "arbitrary")),
    )(x, mod, norm_g, wq, keys)


ROW_SUB = D_MODEL // LANE
ISSUE_UNROLL = 8
SUBLANES = 8
FOLD_ORDER = (0, 4, 2, 6, 1, 5, 3, 7)


def _fold(a, b, shift, first):
    return (jnp.where(first, a, pltpu.roll(b, shift, 0))
            + jnp.where(first, pltpu.roll(a, SUBLANES - shift, 0), b))


def _unpack_pair(words):
    lo = pltpu.bitcast(jnp.left_shift(words, 16), F32)
    hi = pltpu.bitcast(jnp.bitwise_and(words, jnp.uint32(0xFFFF0000)), F32)
    return lo, hi


def _peer_gather_kernel(ids_ref, id1_ref, idn_ref, hf_ref, gt_ref, xn_ref, gate_ref, tab_hbm, o_ref,
                        tbuf, wbuf, sem, *, ngroups):
    i = pl.program_id(0)
    slot = lax.rem(i, GATHER_SLOTS)
    nslot = lax.rem(i + GATHER_SLOTS - 1, GATHER_SLOTS)
    nrows = GATHER_TOKENS * PEER_PAIRS

    def start_row(ids, r, s, priority):
        pltpu.make_async_copy(tab_hbm.at[ids[0, 0, r]], tbuf.at[s, r], sem.at[s]).start(priority=priority)

    def wait_slot(s):
        pltpu.make_async_copy(tab_hbm.at[pl.ds(0, nrows)], tbuf.at[s], sem.at[s]).wait()

    @pl.when(i == 0)
    def _first_groups():
        def body(g, carry):
            for j in range(ISSUE_UNROLL):
                start_row(ids_ref, g * ISSUE_UNROLL + j, 0, j % 2)
                start_row(id1_ref, g * ISSUE_UNROLL + j, 1, j % 2)
            return carry
        lax.fori_loop(0, nrows // ISSUE_UNROLL, body, 0)

    wait_slot(slot)

    sub = lax.broadcasted_iota(I32, (SUBLANES, LANE), 0)
    first4, first2, first1 = sub < 4, sub % 4 < 2, sub % 2 < 1
    ngrp = PEER_PAIRS // SUBLANES
    early = 4
    for t in range(GATHER_TOKENS):
        x_lo = hf_ref[t, :SUBLANES]
        x_hi = hf_ref[t, SUBLANES:]
        base = t * PEER_PAIRS
        for j in range(ngrp):
            for q in range(early):
                start_row(idn_ref, base + j * early + q, nslot, q % 2)
            r = []
            for p in FOLD_ORDER:
                lo, hi = _unpack_pair(tbuf[slot, base + j * SUBLANES + p, 0])
                r.append(lo * x_lo + hi * x_hi)
            l1 = [_fold(r[0], r[1], 4, first4), _fold(r[2], r[3], 4, first4),
                  _fold(r[4], r[5], 4, first4), _fold(r[6], r[7], 4, first4)]
            l2 = [_fold(l1[0], l1[1], 2, first2), _fold(l1[2], l1[3], 2, first2)]
            y = _fold(l2[0], l2[1], 1, first1)
            rows = slice(j * SUBLANES, (j + 1) * SUBLANES)
            a = jnp.sum(y, axis=1, keepdims=True)
            w = gt_ref[0, rows, t:t + 1] * _gelu(a)
            wbuf[rows, :] = jnp.broadcast_to(w, (SUBLANES, LANE))
        nacc = 2
        acc_lo = [jnp.zeros((SUBLANES, LANE), F32) for _ in range(nacc)]
        acc_hi = [jnp.zeros((SUBLANES, LANE), F32) for _ in range(nacc)]
        late0 = ngrp * early
        for k in range(PEER_PAIRS):
            if k % 2 == 0 and late0 + k // 2 < PEER_PAIRS:
                start_row(idn_ref, base + late0 + k // 2, nslot, (k // 2) % 2)
            lo, hi = _unpack_pair(tbuf[slot, base + k, 1])
            wk = wbuf[k:k + 1, :]
            acc_lo[k % nacc] = acc_lo[k % nacc] + lo * wk
            acc_hi[k % nacc] = acc_hi[k % nacc] + hi * wk
        o_ref[t, :SUBLANES] = xn_ref[t, :SUBLANES] + gate_ref[0, :SUBLANES] * (acc_lo[0] + acc_lo[1])
        o_ref[t, SUBLANES:] = xn_ref[t, SUBLANES:] + gate_ref[0, SUBLANES:] * (acc_hi[0] + acc_hi[1])

    @pl.when(i == ngroups - 1)
    def _drain():
        wait_slot(lax.rem(i + 1, GATHER_SLOTS))
        wait_slot(nslot)


def pack_expert_tables(u_tab, v_tab):
    def pack(tab):
        bits = lax.bitcast_convert_type(tab.astype(BF16), jnp.uint16).astype(jnp.uint32)
        half = tab.shape[1] // 2
        return jnp.bitwise_or(bits[:, :half], jnp.left_shift(bits[:, half:], 16))
    words = jnp.stack([pack(u_tab), pack(v_tab)], axis=1)
    return words.reshape(u_tab.shape[0], 2, SUBLANES, LANE)


def peer_gather(ids, hf, gates_t, xn, gate_ffn, table, mod_row_group):
    n, d = xn.shape
    gt = GATHER_TOKENS
    ngroups = n // gt
    assert ngroups >= GATHER_SLOTS - 1
    nrows = gt * PEER_PAIRS
    slab = (ROW_SUB, LANE)
    tok_spec = pl.BlockSpec((gt,) + slab, lambda i: (i, 0, 0))

    def ids_spec(ahead):
        return pl.BlockSpec((1, 1, nrows), lambda i: (jnp.minimum(i + ahead, ngroups - 1), 0, 0),
                            memory_space=pltpu.SMEM)

    out = pl.pallas_call(
        functools.partial(_peer_gather_kernel, ngroups=ngroups),
        grid=(ngroups,),
        in_specs=[ids_spec(0), ids_spec(1), ids_spec(GATHER_SLOTS - 1),
                  tok_spec,
                  pl.BlockSpec((1, PEER_PAIRS, gt), lambda i: (i, 0, 0)),
                  tok_spec,
                  pl.BlockSpec((1,) + slab, lambda i: (mod_row_group(i), 0, 0)),
                  pl.BlockSpec(memory_space=pl.ANY)],
        out_specs=tok_spec,
        out_shape=jax.ShapeDtypeStruct((n,) + slab, F32),
        scratch_shapes=[pltpu.VMEM((GATHER_SLOTS, nrows, 2, SUBLANES, LANE), jnp.uint32),
                        pltpu.VMEM((PEER_PAIRS, LANE), F32),
                        pltpu.SemaphoreType.DMA((GATHER_SLOTS,))],
        compiler_params=_params(("arbitrary",)),
    )(ids, ids, ids, hf.reshape((n,) + slab), gates_t, xn.reshape((n,) + slab),
      gate_ffn.reshape((-1,) + slab), table)
    return out.reshape(n, d)


def _final_norm_kernel(x_ref, g_ref, o_ref):
    x = x_ref[...]
    o_ref[0] = x * lax.rsqrt(jnp.mean(x * x, axis=-1, keepdims=True) + EPS) * g_ref[...]


def final_norm(x, g, batch, seq, ctx):
    n, d = x.shape
    tb = ctx + seq
    nrb = tb // ROW_TILE
    cb = ctx // ROW_TILE
    return pl.pallas_call(
        _final_norm_kernel,
        grid=(batch, seq // ROW_TILE),
        in_specs=[pl.BlockSpec((ROW_TILE, d), lambda b, i: (b * nrb + cb + i, 0)),
                  pl.BlockSpec((1, d), lambda b, i: (0, 0))],
        out_specs=pl.BlockSpec((1, ROW_TILE, d), lambda b, i: (b, i, 0)),
        out_shape=jax.ShapeDtypeStruct((batch, seq, d), F32),
        compiler_params=_params(("parallel", "parallel")),
    )(x, g)


def kernel(x, c, ctx, c_ctx, ada_w, ada_b, norm_mix, norm_ffn, w_in, b_gate, sgu_norm, sgu_w, sgu_b,
           qk_conv_w, mlstm_norm, pool_w, pool_scale, w_out, peer_wq, peer_keys, peer_u, peer_v,
           norm_final):
    batch, seq, d = x.shape
    ctx_len = ctx.shape[1]
    depth = ada_w.shape[0]
    tb = ctx_len + seq
    n = batch * tb
    assert d == D_MODEL and w_in.shape[2] == OFF_G + N_GATES
    assert ctx_len % ROW_TILE == 0 and seq % ROW_TILE == 0 and seq % GRID_W == 0
    assert n % GATHER_TOKENS == 0 and ROW_TILE % GATHER_TOKENS == 0
    blocks_per_batch = tb // ROW_TILE
    ctx_blocks = ctx_len // ROW_TILE
    nchunks = tb // CHUNK

    def mod_row(i):
        return jnp.where(i % blocks_per_batch < ctx_blocks, batch, i // blocks_per_batch)

    def mod_row_group(i):
        return mod_row(i // (ROW_TILE // GATHER_TOKENS))

    cond_rows = -(-(batch + 1) // 8) * 8
    cond = jnp.zeros((cond_rows, d), F32).at[:batch].set(c).at[batch].set(c_ctx)
    mods = ada_modulation_all(cond, ada_w, ada_b).reshape(depth, cond_rows, 6, d)

    xs = jnp.concatenate([ctx, x], axis=1).reshape(n, d)
    lane_pad = LANE - N_GATES
    for l in range(depth):
        mod = mods[l]
        w_main = w_in[l, :, :OFF_G].astype(BF16)
        w_gate = jnp.pad(w_in[l, :, OFF_G:], ((0, 0), (0, lane_pad))).astype(BF16)
        p, graw = in_proj(xs, mod, norm_mix[l][None], w_main, w_gate, mod_row)

        y_a = sgu_mixer(p, sgu_norm[l][None], sgu_w[l].astype(BF16), sgu_b[l].T)

        gp = gate_prep(graw[0], jnp.pad(b_gate[l], (0, lane_pad))[None])
        gp = gp[:, :N_GATES].reshape(n, 2, 2, MLSTM_HEADS)
        gcol = gp.transpose(3, 0, 1, 2).reshape(MLSTM_HEADS, n, 4)
        grow = gcol.reshape(MLSTM_HEADS, n // CHUNK, CHUNK, 4).transpose(0, 1, 3, 2)
        qk = qk_conv(p, qk_conv_w[l], blocks_per_batch, ctx_blocks)
        y_b = mlstm_mixer(p, qk, gcol, grow, mlstm_norm[l][None], batch, tb, ctx_len // CHUNK)

        y_c = pool_mixer(p, pool_w[l].astype(BF16), pool_scale[l].reshape(POOL_GROUPS, 1, POOL_GC),
                         batch, tb, ctx_len, seq)

        xn = out_proj(xs, y_a, y_b, y_c, mod, w_out[l].astype(BF16), mod_row)

        hf, e, gates = peer_route(xn, mod, norm_ffn[l][None], peer_wq[l].astype(BF16),
                                  peer_keys[l].astype(BF16), mod_row)
        ids = e.reshape(PEER_PAIRS, n).T.reshape(n // GATHER_TOKENS, 1, GATHER_TOKENS * PEER_PAIRS)
        gates_t = gates.reshape(PEER_PAIRS, n // GATHER_TOKENS, GATHER_TOKENS).transpose(1, 0, 2)
        xs = peer_gather(ids, hf, gates_t, xn, mod[:, 5], pack_expert_tables(peer_u[l], peer_v[l]),
                         mod_row_group)

    return final_norm(xs, norm_final[None], batch, seq, ctx_len)
```

```python
import functools

import jax
import jax.numpy as jnp
from jax import lax
from jax.experimental import pallas as pl
from jax.experimental.pallas import tpu as pltpu

F32 = jnp.float32
BF16 = jnp.bfloat16
I32 = jnp.int32

EPS = 1e-6
GRID_W = 64
GRID_W_LOG2 = 6
D_MODEL = 2048
D_SGU = D_MODEL // 4
SGU_HEADS = 4
SGU_HD = D_SGU // SGU_HEADS
CHUNK = 128
D_MLSTM = D_MODEL // 2
MLSTM_HEADS = 4
MLSTM_HD = D_MLSTM // MLSTM_HEADS
D_POOL = D_MODEL // 4
POOL_GROUPS = 4
POOL_GC = D_POOL // POOL_GROUPS
N_GATES = 2 * 2 * MLSTM_HEADS
OFF_U = 0
OFF_V = OFF_U + D_SGU
OFF_P = OFF_V + D_SGU
OFF_Q = OFF_P + D_POOL
OFF_O = OFF_Q + D_MLSTM
OFF_K = OFF_O + D_MLSTM
OFF_VM = OFF_K + D_MLSTM
OFF_G = OFF_VM + D_MLSTM
N_KEYS = 128
PEER_HEADS = 8
PEER_TOPK = 16
PEER_DK = 256
PEER_PAIRS = PEER_HEADS * PEER_TOPK

LANE = 128
ROW_TILE = 256
POOL_HALO_TILES = -(-(GRID_W << (POOL_GROUPS - 1)) // ROW_TILE)
GATHER_TOKENS = 8
GATHER_SLOTS = 3
VMEM_LIMIT = 56 * 1024 * 1024

CAND_PAIRS = [(a, b) for a in range(PEER_TOPK) for b in range(PEER_TOPK)
              if (a + 1) * (b + 1) <= PEER_TOPK]
CAND_ROWS = 56


def _params(sem, vmem=VMEM_LIMIT):
    return pltpu.CompilerParams(dimension_semantics=sem, vmem_limit_bytes=vmem)


def _norm_mod(x, g, shift, scale):
    ms = jnp.mean(x * x, axis=-1, keepdims=True)
    y = x * lax.rsqrt(ms + EPS)
    return (y * g) * (1.0 + scale) + shift


def _gelu(x):
    return jax.nn.gelu(x, approximate=True)


def _sigmoid(x):
    return 1.0 / (1.0 + jnp.exp(-x))


def _ada_kernel(cond_ref, w_ref, b_ref, o_ref):
    a = cond_ref[...]
    a = (a * _sigmoid(a)).astype(BF16)
    o_ref[0] = jnp.dot(a, w_ref[0].astype(BF16), preferred_element_type=F32) + b_ref[0]


def ada_modulation_all(cond, ada_w, ada_b):
    depth, d, d6 = ada_w.shape
    rows = cond.shape[0]
    tn = 1024
    return pl.pallas_call(
        _ada_kernel,
        grid=(depth, d6 // tn),
        in_specs=[pl.BlockSpec((rows, d), lambda l, j: (0, 0)),
                  pl.BlockSpec((1, d, tn), lambda l, j: (l, 0, j)),
                  pl.BlockSpec((1, 1, tn), lambda l, j: (l, 0, j))],
        out_specs=pl.BlockSpec((1, rows, tn), lambda l, j: (l, 0, j)),
        out_shape=jax.ShapeDtypeStruct((depth, rows, d6), F32),
        compiler_params=_params(("parallel", "parallel")),
    )(cond, ada_w, ada_b.reshape(depth, 1, d6))


def _in_proj_kernel(x_ref, mod_ref, g_ref, w_ref, wg_ref, p_ref, gate_ref, *, ncols):
    h = _norm_mod(x_ref[...], g_ref[...], mod_ref[0, 0:1, :], mod_ref[0, 1:2, :]).astype(BF16)
    for j in range(ncols // 256):
        sl = slice(j * 256, (j + 1) * 256)
        p_ref[:, sl] = jnp.dot(h, w_ref[:, sl], preferred_element_type=F32).astype(BF16)
    gate_ref[0] = jnp.dot(h, wg_ref[...], preferred_element_type=F32)


def in_proj(x, mod, norm_g, w_main, w_gate, mod_row):
    n, d = x.shape
    ncol_blocks = 2
    ncols = OFF_G // ncol_blocks
    return pl.pallas_call(
        functools.partial(_in_proj_kernel, ncols=ncols),
        grid=(ncol_blocks, n // ROW_TILE),
        in_specs=[pl.BlockSpec((ROW_TILE, d), lambda c, i: (i, 0)),
                  pl.BlockSpec((1, 6, d), lambda c, i: (mod_row(i), 0, 0)),
                  pl.BlockSpec((1, d), lambda c, i: (0, 0)),
                  pl.BlockSpec((d, ncols), lambda c, i: (0, c)),
                  pl.BlockSpec((d, LANE), lambda c, i: (0, 0))],
        out_specs=[pl.BlockSpec((ROW_TILE, ncols), lambda c, i: (i, c)),
                   pl.BlockSpec((1, ROW_TILE, LANE), lambda c, i: (c, i, 0))],
        out_shape=[jax.ShapeDtypeStruct((n, OFF_G), BF16),
                   jax.ShapeDtypeStruct((ncol_blocks, n, LANE), F32)],
        compiler_params=_params(("arbitrary", "arbitrary")),
    )(x, mod, norm_g, w_main, w_gate)


def _sgu_kernel(u_ref, v_ref, g_ref, ws_ref, bs_ref, y_ref):
    u = _gelu(u_ref[...].astype(F32))
    v = _gelu(v_ref[...].astype(F32))
    ms = jnp.mean(v * v, axis=-1, keepdims=True)
    v = (v * lax.rsqrt(ms + EPS) * g_ref[...]).astype(BF16)
    for c in range(ROW_TILE // CHUNK):
        rows = slice(c * CHUNK, (c + 1) * CHUNK)
        for h in range(SGU_HEADS):
            cols = slice(h * SGU_HD, (h + 1) * SGU_HD)
            mixed = jnp.dot(ws_ref[h], v[rows, cols], preferred_element_type=F32) + bs_ref[:, h:h + 1]
            y_ref[rows, cols] = (u[rows, cols] * mixed).astype(BF16)


def sgu_mixer(p, sgu_norm, sgu_w, sgu_bt):
    n = p.shape[0]
    return pl.pallas_call(
        _sgu_kernel,
        grid=(n // ROW_TILE,),
        in_specs=[pl.BlockSpec((ROW_TILE, D_SGU), lambda i: (i, OFF_U // D_SGU)),
                  pl.BlockSpec((ROW_TILE, D_SGU), lambda i: (i, OFF_V // D_SGU)),
                  pl.BlockSpec((1, D_SGU), lambda i: (0, 0)),
                  pl.BlockSpec((SGU_HEADS, CHUNK, CHUNK), lambda i: (0, 0, 0)),
                  pl.BlockSpec((CHUNK, SGU_HEADS), lambda i: (0, 0))],
        out_specs=pl.BlockSpec((ROW_TILE, D_SGU), lambda i: (i, 0)),
        out_shape=jax.ShapeDtypeStruct((n, D_SGU), BF16),
        compiler_params=_params(("parallel",)),
    )(p, p, sgu_norm, sgu_w, sgu_bt)


def _gate_prep_kernel(g_ref, bias_ref, o_ref):
    g = g_ref[...] + bias_ref[...]
    lf = jnp.minimum(g, 0.0) - jnp.log(1.0 + jnp.exp(-jnp.abs(g)))
    row = lax.broadcasted_iota(I32, (CHUNK, CHUNK), 0)
    col = lax.broadcasted_iota(I32, (CHUNK, CHUNK), 1)
    lower = (col <= row).astype(F32)
    upper = (col >= row).astype(F32)
    b_fwd = jnp.dot(lower, lf, preferred_element_type=F32, precision=lax.Precision.HIGHEST)
    b_bwd = jnp.dot(upper, lf, preferred_element_type=F32, precision=lax.Precision.HIGHEST)
    lane = lax.broadcasted_iota(I32, (1, LANE), 1)
    is_input_gate = (lane // MLSTM_HEADS) % 2 == 0
    is_fwd = lane < 2 * MLSTM_HEADS
    o_ref[...] = jnp.where(is_input_gate, g, jnp.where(is_fwd, b_fwd, b_bwd))


def gate_prep(graw, bias):
    n = graw.shape[0]
    return pl.pallas_call(
        _gate_prep_kernel,
        grid=(n // CHUNK,),
        in_specs=[pl.BlockSpec((CHUNK, LANE), lambda i: (i, 0)),
                  pl.BlockSpec((1, LANE), lambda i: (0, 0))],
        out_specs=pl.BlockSpec((CHUNK, LANE), lambda i: (i, 0)),
        out_shape=jax.ShapeDtypeStruct((n, LANE), F32),
        compiler_params=_params(("parallel",)),
    )(graw, bias)


HALO = 16


def _qk_conv_kernel(xm_ref, xp_ref, xn_ref, w_ref, o_ref, *, blocks_per_batch, ctx_blocks):
    i = pl.program_id(0)
    j = pl.program_id(1)
    x = xm_ref[...].astype(F32)
    tm = x.shape[0]
    prev_row = xp_ref[...].astype(F32)[HALO - 1:HALO, :]
    next_row = xn_ref[...].astype(F32)[0:1, :]
    ib = i % blocks_per_batch
    at_start = jnp.logical_or(ib == 0, ib == ctx_blocks)
    at_end = jnp.logical_or(ib == ctx_blocks - 1, ib == blocks_per_batch - 1)
    prev_row = jnp.where(at_start, 0.0, prev_row)
    next_row = jnp.where(at_end, 0.0, next_row)
    rows = lax.broadcasted_iota(I32, (tm, 1), 0)
    x_prev = jnp.where(rows == 0, prev_row, pltpu.roll(x, 1, 0))
    x_next = jnp.where(rows == tm - 1, next_row, pltpu.roll(x, tm - 1, 0))
    y = w_ref[0:1, :] * x_prev + w_ref[1:2, :] * x + w_ref[2:3, :] * x_next
    y = y * _sigmoid(y)
    y = y * jnp.where(j >= 2, MLSTM_HD ** -0.5, 1.0)
    o_ref[...] = y.astype(BF16)


def qk_conv(p, conv_w, blocks_per_batch, ctx_blocks):
    n = p.shape[0]
    tc = 512
    nhalo = n // HALO
    per = ROW_TILE // HALO

    def col(j):
        return jnp.where(j < 2, OFF_Q // tc + j, OFF_K // tc + j - 2)

    return pl.pallas_call(
        functools.partial(_qk_conv_kernel, blocks_per_batch=blocks_per_batch, ctx_blocks=ctx_blocks),
        grid=(n // ROW_TILE, 2 * D_MLSTM // tc),
        in_specs=[pl.BlockSpec((ROW_TILE, tc), lambda i, j: (i, col(j))),
                  pl.BlockSpec((HALO, tc), lambda i, j: (jnp.maximum(i * per - 1, 0), col(j))),
                  pl.BlockSpec((HALO, tc), lambda i, j: (jnp.minimum((i + 1) * per, nhalo - 1), col(j))),
                  pl.BlockSpec((3, tc), lambda i, j: (0, j))],
        out_specs=pl.BlockSpec((ROW_TILE, tc), lambda i, j: (i, j)),
        out_shape=jax.ShapeDtypeStruct((n, 2 * D_MLSTM), BF16),
        compiler_params=_params(("parallel", "parallel")),
    )(p, p, p, conv_w)


def _mlstm_kernel(q_ref, k_ref, v_ref, o_ref, gc_ref, gr_ref, nrm_ref, y_ref,
                  hf_ref, hb_ref, c_ref, n_ref, m_ref, *, nchunks, nctx):
    c_ref[...] = jnp.zeros_like(c_ref)
    n_ref[...] = jnp.zeros_like(n_ref)
    m_ref[...] = jnp.zeros_like(m_ref)
    row = lax.broadcasted_iota(I32, (CHUNK, CHUNK), 0)
    col = lax.broadcasted_iota(I32, (CHUNK, CHUNK), 1)
    seen = (row >= col, row <= col)
    h_refs = (hf_ref, hb_ref)

    def step(d, c):
        r0 = pl.multiple_of(c * CHUNK, CHUNK)
        rows = pl.ds(r0, CHUNK)
        q = q_ref[rows, :]
        k = k_ref[rows, :]
        v = v_ref[rows, :]
        gcol = gc_ref[0, rows, :]
        grow = gr_ref[0, c]
        ig_c, b_c = gcol[:, 2 * d:2 * d + 1], gcol[:, 2 * d + 1:2 * d + 2]
        ig_r, b_r = grow[2 * d:2 * d + 1, :], grow[2 * d + 1:2 * d + 2, :]
        b_last = b_c[CHUNK - 1:CHUNK, :] if d == 0 else b_c[0:1, :]
        m = m_ref[d, 0:1, 0:1]
        g_c = b_last - b_c + ig_c
        m_new = jnp.maximum(b_last + m, jnp.max(g_c, axis=0, keepdims=True))
        wk_c = jnp.exp(g_c - m_new)
        decay = jnp.exp(b_last + m - m_new)

        log_w = jnp.where(seen[d], b_c - b_r + ig_r, -jnp.inf)
        inter = b_c + m
        m_t = jnp.maximum(inter, jnp.max(log_w, axis=1, keepdims=True))
        w_prev = jnp.exp(inter - m_t)
        s = lax.dot_general(q, k, (((1,), (1,)), ((), ())), preferred_element_type=F32)
        s = s * jnp.exp(log_w - m_t)
        ct = c_ref[d]
        nvec = n_ref[d]
        num = (jnp.dot(s.astype(BF16), v, preferred_element_type=F32)
               + w_prev * jnp.dot(q, ct.astype(BF16), preferred_element_type=F32))
        qf = q.astype(F32)
        kf = k.astype(F32)
        den = jnp.sum(s, axis=1, keepdims=True) + w_prev * jnp.sum(qf * nvec, axis=1, keepdims=True)
        h_refs[d][rows, :] = num / jnp.maximum(jnp.abs(den), jnp.exp(-m_t))

        wv = (v.astype(F32) * wk_c).astype(BF16)
        c_ref[d] = decay * ct + lax.dot_general(k, wv, (((0,), (0,)), ((), ())),
                                                preferred_element_type=F32)
        n_ref[d] = decay * nvec + jnp.sum(kf * wk_c, axis=0, keepdims=True)
        m_ref[d] = jnp.broadcast_to(m_new, (1, LANE))

    def body(i, carry):
        step(0, i)
        step(1, jnp.where(i < nctx, nctx - 1 - i, nchunks - 1 - (i - nctx)))
        return carry

    lax.fori_loop(0, nchunks, body, 0)

    def finish(c, carry):
        rows = pl.ds(pl.multiple_of(c * CHUNK, CHUNK), CHUNK)
        h = hf_ref[rows, :] + hb_ref[rows, :]
        hn = h * lax.rsqrt(jnp.mean(h * h, axis=-1, keepdims=True) + EPS) * nrm_ref[...]
        y_ref[rows, :] = (_sigmoid(o_ref[rows, :].astype(F32)) * hn).astype(BF16)
        return carry

    lax.fori_loop(0, nchunks, finish, 0)


def mlstm_mixer(p, qk, gcol, grow, mlstm_norm, batch, tb, nctx):
    n = p.shape[0]
    nchunks = tb // CHUNK
    hd = MLSTM_HD
    return pl.pallas_call(
        functools.partial(_mlstm_kernel, nchunks=nchunks, nctx=nctx),
        grid=(batch, MLSTM_HEADS),
        in_specs=[pl.BlockSpec((tb, hd), lambda b, h: (b, h)),
                  pl.BlockSpec((tb, hd), lambda b, h: (b, MLSTM_HEADS + h)),
                  pl.BlockSpec((tb, hd), lambda b, h: (b, OFF_VM // hd + h)),
                  pl.BlockSpec((tb, hd), lambda b, h: (b, OFF_O // hd + h)),
                  pl.BlockSpec((1, tb, 4), lambda b, h: (h, b, 0)),
                  pl.BlockSpec((1, nchunks, 4, CHUNK), lambda b, h: (h, b, 0, 0)),
                  pl.BlockSpec((1, hd), lambda b, h: (0, h))],
        out_specs=pl.BlockSpec((tb, hd), lambda b, h: (b, h)),
        out_shape=jax.ShapeDtypeStruct((n, D_MLSTM), BF16),
        scratch_shapes=[pltpu.VMEM((tb, hd), F32), pltpu.VMEM((tb, hd), F32),
                        pltpu.VMEM((2, hd, hd), F32), pltpu.VMEM((2, 1, hd), F32),
                        pltpu.VMEM((2, 1, LANE), F32)],
        compiler_params=_params(("parallel", "parallel")),
    )(qk, qk, p, p, gcol, grow, mlstm_norm)


def _pool_kernel(xfull_ref, xrow_ref, w_ref, sc_ref, y_ref, m_ref, inv_ref, *, tb, ctx, seq):
    g = pl.program_id(0)
    i = pl.program_id(1)
    b = pl.program_id(2)
    tm = xrow_ref.shape[0]
    half = jnp.left_shift(1, g)
    win = 2 * half

    def grid_coords(pos):
        is_ctx = pos < ctx
        lat = pos - ctx
        r = jnp.where(is_ctx, 0, jnp.right_shift(lat, GRID_W_LOG2))
        c = jnp.where(is_ctx, pos, jnp.bitwise_and(lat, GRID_W - 1))
        return is_ctx, r, c

    band = m_ref.shape[1]
    first_tile = jnp.clip(i - POOL_HALO_TILES, 0, (tb - band) // tm)
    s0 = pl.multiple_of(first_tile * tm, tm)

    @pl.when(b == 0)
    def _build_window_matrix():
        t = i * tm + lax.broadcasted_iota(I32, (tm, 1), 0)
        s = s0 + lax.broadcasted_iota(I32, (1, band), 1)
        ctx_t, r_t, c_t = grid_coords(t)
        ctx_s, r_s, c_s = grid_coords(s)
        dr = r_s - r_t + half
        dc = c_s - c_t + half
        inside = (ctx_t == ctx_s) & (dr >= 0) & (dr < win) & (dc >= 0) & (dc < win)
        m_ref[...] = jnp.where(inside, 1.0, 0.0).astype(BF16)
        nrows = jnp.where(ctx_t, 1, seq // GRID_W)
        ncols = jnp.where(ctx_t, ctx, GRID_W)
        cnt_r = jnp.minimum(r_t - half + win, nrows) - jnp.maximum(r_t - half, 0)
        cnt_c = jnp.minimum(c_t - half + win, ncols) - jnp.maximum(c_t - half, 0)
        inv_ref[...] = 1.0 / (cnt_r * cnt_c).astype(F32)

    tot = jnp.dot(m_ref[...], xfull_ref[pl.ds(s0, band), :], preferred_element_type=F32)
    d = (tot * inv_ref[...] - xrow_ref[...].astype(F32)).astype(BF16)
    y = jnp.dot(d, w_ref[0], preferred_element_type=F32) * sc_ref[0]
    y_ref[...] = y.astype(BF16)


def pool_mixer(p, pool_w, pool_scale, batch, tb, ctx, seq):
    n = p.shape[0]
    nrb = tb // ROW_TILE
    c0 = OFF_P // POOL_GC
    band = min(2 * POOL_HALO_TILES + 1, nrb) * ROW_TILE
    return pl.pallas_call(
        functools.partial(_pool_kernel, tb=tb, ctx=ctx, seq=seq),
        grid=(POOL_GROUPS, nrb, batch),
        in_specs=[pl.BlockSpec((tb, POOL_GC), lambda g, i, b: (b, c0 + g)),
                  pl.BlockSpec((ROW_TILE, POOL_GC), lambda g, i, b: (b * nrb + i, c0 + g)),
                  pl.BlockSpec((1, POOL_GC, POOL_GC), lambda g, i, b: (g, 0, 0)),
                  pl.BlockSpec((1, 1, POOL_GC), lambda g, i, b: (g, 0, 0))],
        out_specs=pl.BlockSpec((ROW_TILE, POOL_GC), lambda g, i, b: (b * nrb + i, g)),
        out_shape=jax.ShapeDtypeStruct((n, D_POOL), BF16),
        scratch_shapes=[pltpu.VMEM((ROW_TILE, band), BF16), pltpu.VMEM((ROW_TILE, 1), F32)],
        compiler_params=_params(("arbitrary", "arbitrary", "arbitrary")),
    )(p, p, pool_w, pool_scale)


def _out_proj_kernel(x_ref, ya_ref, yb_ref, yc_ref, mod_ref, w_ref, o_ref):
    ya, yb, yc = ya_ref[...], yb_ref[...], yc_ref[...]
    k1 = D_SGU
    k2 = D_SGU + D_MLSTM
    for j in range(D_MODEL // 256):
        sl = slice(j * 256, (j + 1) * 256)
        acc = jnp.dot(ya, w_ref[0:k1, sl], preferred_element_type=F32)
        acc += jnp.dot(yb, w_ref[k1:k2, sl], preferred_element_type=F32)
        acc += jnp.dot(yc, w_ref[k2:D_MODEL, sl], preferred_element_type=F32)
        o_ref[:, sl] = x_ref[:, sl] + mod_ref[0, 2:3, sl] * acc


def out_proj(x, ya, yb, yc, mod, w_out, mod_row):
    n, d = x.shape
    return pl.pallas_call(
        _out_proj_kernel,
        grid=(n // ROW_TILE,),
        in_specs=[pl.BlockSpec((ROW_TILE, d), lambda i: (i, 0)),
                  pl.BlockSpec((ROW_TILE, D_SGU), lambda i: (i, 0)),
                  pl.BlockSpec((ROW_TILE, D_MLSTM), lambda i: (i, 0)),
                  pl.BlockSpec((ROW_TILE, D_POOL), lambda i: (i, 0)),
                  pl.BlockSpec((1, 6, d), lambda i: (mod_row(i), 0, 0)),
                  pl.BlockSpec((d, d), lambda i: (0, 0))],
        out_specs=pl.BlockSpec((ROW_TILE, d), lambda i: (i, 0)),
        out_shape=jax.ShapeDtypeStruct((n, d), F32),
        compiler_params=_params(("parallel",)),
    )(x, ya, yb, yc, mod, w_out)


def _topk_rows(vals, k):
    nrows = vals.shape[0]
    iota = lax.broadcasted_iota(I32, vals.shape, 0)
    out_v, out_i = [], []
    for _ in range(k):
        mx = jnp.max(vals, axis=0, keepdims=True)
        am = jnp.min(jnp.where(vals == mx, iota, nrows), axis=0, keepdims=True)
        out_v.append(mx)
        out_i.append(am)
        vals = jnp.where(iota == am, -jnp.inf, vals)
    return out_v, out_i


def _peer_route_kernel(x_ref, mod_ref, g_ref, wq_ref, keys_ref, hf_ref, e_ref, gate_ref,
                       hb_ref, cand_ref, cid_ref):
    h = pl.program_id(1)

    @pl.when(h == 0)
    def _norm():
        hf = _norm_mod(x_ref[...], g_ref[...], mod_ref[0, 3:4, :], mod_ref[0, 4:5, :])
        hf_ref[...] = hf
        hb_ref[...] = hf.astype(BF16)

    q = jnp.dot(hb_ref[...], wq_ref[...], preferred_element_type=F32).astype(BF16)
    half = PEER_DK // 2
    nt = (((1,), (1,)), ((), ()))
    scores1 = lax.dot_general(keys_ref[0], q[:, :half], nt, preferred_element_type=F32)
    scores2 = lax.dot_general(keys_ref[1], q[:, half:], nt, preferred_element_type=F32)
    npad = CAND_ROWS - len(CAND_PAIRS)
    for c in range(q.shape[0] // LANE):
        lanes = slice(c * LANE, (c + 1) * LANE)
        s1, i1 = _topk_rows(scores1[:, lanes], PEER_TOPK)
        s2, i2 = _topk_rows(scores2[:, lanes], PEER_TOPK)
        for r, (a, b) in enumerate(CAND_PAIRS):
            cand_ref[r:r + 1, lanes] = s1[a] + s2[b]
            cid_ref[r:r + 1, lanes] = i1[a] * N_KEYS + i2[b]
        cand_ref[len(CAND_PAIRS):, lanes] = jnp.full((npad, LANE), -jnp.inf, F32)
        cid_ref[len(CAND_PAIRS):, lanes] = jnp.zeros((npad, LANE), I32)
        sc, slot = _topk_rows(cand_ref[:, lanes], PEER_TOPK)
        cid = cid_ref[:, lanes]
        riota = lax.broadcasted_iota(I32, cid.shape, 0)
        ex = [jnp.exp(v - sc[0]) for v in sc]
        tot = ex[0]
        for v in ex[1:]:
            tot = tot + v
        for j in range(PEER_TOPK):
            e_ref[0, j:j + 1, lanes] = jnp.sum(jnp.where(riota == slot[j], cid, 0), axis=0, keepdims=True)
            gate_ref[0, j:j + 1, lanes] = ex[j] / tot


def peer_route(x, mod, norm_g, wq, keys, mod_row):
    n, d = x.shape
    tt = ROW_TILE
    return pl.pallas_call(
        _peer_route_kernel,
        grid=(n // tt, PEER_HEADS),
        in_specs=[pl.BlockSpec((tt, d), lambda i, h: (i, 0)),
                  pl.BlockSpec((1, 6, d), lambda i, h: (mod_row(i), 0, 0)),
                  pl.BlockSpec((1, d), lambda i, h: (0, 0)),
                  pl.BlockSpec((d, PEER_DK), lambda i, h: (0, h)),
                  pl.BlockSpec((2, N_KEYS, PEER_DK // 2), lambda i, h: (0, 0, 0))],
        out_specs=[pl.BlockSpec((tt, d), lambda i, h: (i, 0)),
                   pl.BlockSpec((1, PEER_TOPK, tt), lambda i, h: (h, 0, i)),
                   pl.BlockSpec((1, PEER_TOPK, tt), lambda i, h: (h, 0, i))],
        out_shape=[jax.ShapeDtypeStruct((n, d), F32),
                   jax.ShapeDtypeStruct((PEER_HEADS, PEER_TOPK, n), I32),
                   jax.ShapeDtypeStruct((PEER_HEADS, PEER_TOPK, n), F32)],
        scratch_shapes=[pltpu.VMEM((tt, d), BF16), pltpu.VMEM((CAND_ROWS, tt), F32),
                        pltpu.VMEM((CAND_ROWS, tt), I32)],
        compiler_params=_params(("parallel", "arbitrary")),
    )(x, mod, norm_g, wq, keys)


ROW_SUB = D_MODEL // LANE
ISSUE_UNROLL = 8
SUBLANES = 8
FOLD_ORDER = (0, 4, 2, 6, 1, 5, 3, 7)


def _fold(a, b, shift, first):
    return (jnp.where(first, a, pltpu.roll(b, shift, 0))
            + jnp.where(first, pltpu.roll(a, SUBLANES - shift, 0), b))


def _unpack_pair(words):
    lo = pltpu.bitcast(jnp.left_shift(words, 16), F32)
    hi = pltpu.bitcast(jnp.bitwise_and(words, jnp.uint32(0xFFFF0000)), F32)
    return lo, hi


def _peer_gather_kernel(ids_ref, id1_ref, idn_ref, hf_ref, gt_ref, xn_ref, gate_ref, tab_hbm, o_ref,
                        tbuf, wbuf, sem, *, ngroups):
    i = pl.program_id(0)
    slot = lax.rem(i, GATHER_SLOTS)
    nslot = lax.rem(i + GATHER_SLOTS - 1, GATHER_SLOTS)
    nrows = GATHER_TOKENS * PEER_PAIRS

    def start_row(ids, r, s, priority):
        pltpu.make_async_copy(tab_hbm.at[ids[0, 0, r]], tbuf.at[s, r], sem.at[s]).start(priority=priority)

    def wait_slot(s):
        pltpu.make_async_copy(tab_hbm.at[pl.ds(0, nrows)], tbuf.at[s], sem.at[s]).wait()

    @pl.when(i == 0)
    def _first_groups():
        def body(g, carry):
            for j in range(ISSUE_UNROLL):
                start_row(ids_ref, g * ISSUE_UNROLL + j, 0, j % 2)
                start_row(id1_ref, g * ISSUE_UNROLL + j, 1, j % 2)
            return carry
        lax.fori_loop(0, nrows // ISSUE_UNROLL, body, 0)

    wait_slot(slot)

    sub = lax.broadcasted_iota(I32, (SUBLANES, LANE), 0)
    first4, first2, first1 = sub < 4, sub % 4 < 2, sub % 2 < 1
    ngrp = PEER_PAIRS // SUBLANES
    early = 6
    late = SUBLANES - early
    for t in range(GATHER_TOKENS):
        x_lo = hf_ref[t, :SUBLANES]
        x_hi = hf_ref[t, SUBLANES:]
        base = t * PEER_PAIRS
        for j in range(ngrp):
            for q in range(early):
                start_row(idn_ref, base + j * early + q, nslot, q % 2)
            r = []
            for p in FOLD_ORDER:
                lo, hi = _unpack_pair(tbuf[slot, base + j * SUBLANES + p, 0])
                r.append(lo * x_lo + hi * x_hi)
            l1 = [_fold(r[0], r[1], 4, first4), _fold(r[2], r[3], 4, first4),
                  _fold(r[4], r[5], 4, first4), _fold(r[6], r[7], 4, first4)]
            l2 = [_fold(l1[0], l1[1], 2, first2), _fold(l1[2], l1[3], 2, first2)]
            y = _fold(l2[0], l2[1], 1, first1)
            rows = slice(j * SUBLANES, (j + 1) * SUBLANES)
            a = jnp.sum(y, axis=1, keepdims=True)
            w = gt_ref[0, rows, t:t + 1] * _gelu(a)
            wbuf[rows, :] = jnp.broadcast_to(w, (SUBLANES, LANE))
        nacc = 2
        acc_lo = [jnp.zeros((SUBLANES, LANE), F32) for _ in range(nacc)]
        acc_hi = [jnp.zeros((SUBLANES, LANE), F32) for _ in range(nacc)]
        late0 = ngrp * early
        for k in range(PEER_PAIRS):
            if k % SUBLANES == 0:
                for q in range(late0 + (k // SUBLANES) * late, late0 + (k // SUBLANES + 1) * late):
                    start_row(idn_ref, base + q, nslot, q % 2)
            lo, hi = _unpack_pair(tbuf[slot, base + k, 1])
            wk = wbuf[k:k + 1, :]
            acc_lo[k % nacc] = acc_lo[k % nacc] + lo * wk
            acc_hi[k % nacc] = acc_hi[k % nacc] + hi * wk
        o_ref[t, :SUBLANES] = xn_ref[t, :SUBLANES] + gate_ref[0, :SUBLANES] * (acc_lo[0] + acc_lo[1])
        o_ref[t, SUBLANES:] = xn_ref[t, SUBLANES:] + gate_ref[0, SUBLANES:] * (acc_hi[0] + acc_hi[1])

    @pl.when(i == ngroups - 1)
    def _drain():
        wait_slot(lax.rem(i + 1, GATHER_SLOTS))
        wait_slot(nslot)


def pack_expert_tables(u_tab, v_tab):
    def pack(tab):
        bits = lax.bitcast_convert_type(tab.astype(BF16), jnp.uint16).astype(jnp.uint32)
        half = tab.shape[1] // 2
        return jnp.bitwise_or(bits[:, :half], jnp.left_shift(bits[:, half:], 16))
    words = jnp.stack([pack(u_tab), pack(v_tab)], axis=1)
    return words.reshape(u_tab.shape[0], 2, SUBLANES, LANE)


def peer_gather(ids, hf, gates_t, xn, gate_ffn, table, mod_row_group):
    n, d = xn.shape
    gt = GATHER_TOKENS
    ngroups = n // gt
    assert ngroups >= GATHER_SLOTS - 1
    nrows = gt * PEER_PAIRS
    slab = (ROW_SUB, LANE)
    tok_spec = pl.BlockSpec((gt,) + slab, lambda i: (i, 0, 0))

    def ids_spec(ahead):
        return pl.BlockSpec((1, 1, nrows), lambda i: (jnp.minimum(i + ahead, ngroups - 1), 0, 0),
                            memory_space=pltpu.SMEM)

    out = pl.pallas_call(
        functools.partial(_peer_gather_kernel, ngroups=ngroups),
        grid=(ngroups,),
        in_specs=[ids_spec(0), ids_spec(1), ids_spec(GATHER_SLOTS - 1),
                  tok_spec,
                  pl.BlockSpec((1, PEER_PAIRS, gt), lambda i: (i, 0, 0)),
                  tok_spec,
                  pl.BlockSpec((1,) + slab, lambda i: (mod_row_group(i), 0, 0)),
                  pl.BlockSpec(memory_space=pl.ANY)],
        out_specs=tok_spec,
        out_shape=jax.ShapeDtypeStruct((n,) + slab, F32),
        scratch_shapes=[pltpu.VMEM((GATHER_SLOTS, nrows, 2, SUBLANES, LANE), jnp.uint32),
                        pltpu.VMEM((PEER_PAIRS, LANE), F32),
                        pltpu.SemaphoreType.DMA((GATHER_SLOTS,))],
        compiler_params=_params(("arbitrary",)),
    )(ids, ids, ids, hf.reshape((n,) + slab), gates_t, xn.reshape((n,) + slab),
      gate_ffn.reshape((-1,) + slab), table)
    return out.reshape(n, d)


def _final_norm_kernel(x_ref, g_ref, o_ref):
    x = x_ref[...]
    o_ref[0] = x * lax.rsqrt(jnp.mean(x * x, axis=-1, keepdims=True) + EPS) * g_ref[...]


def final_norm(x, g, batch, seq, ctx):
    n, d = x.shape
    tb = ctx + seq
    nrb = tb // ROW_TILE
    cb = ctx // ROW_TILE
    return pl.pallas_call(
        _final_norm_kernel,
        grid=(batch, seq // ROW_TILE),
        in_specs=[pl.BlockSpec((ROW_TILE, d), lambda b, i: (b * nrb + cb + i, 0)),
                  pl.BlockSpec((1, d), lambda b, i: (0, 0))],
        out_specs=pl.BlockSpec((1, ROW_TILE, d), lambda b, i: (b, i, 0)),
        out_shape=jax.ShapeDtypeStruct((batch, seq, d), F32),
        compiler_params=_params(("parallel", "parallel")),
    )(x, g)


def kernel(x, c, ctx, c_ctx, ada_w, ada_b, norm_mix, norm_ffn, w_in, b_gate, sgu_norm, sgu_w, sgu_b,
           qk_conv_w, mlstm_norm, pool_w, pool_scale, w_out, peer_wq, peer_keys, peer_u, peer_v,
           norm_final):
    batch, seq, d = x.shape
    ctx_len = ctx.shape[1]
    depth = ada_w.shape[0]
    tb = ctx_len + seq
    n = batch * tb
    assert d == D_MODEL and w_in.shape[2] == OFF_G + N_GATES
    assert ctx_len % ROW_TILE == 0 and seq % ROW_TILE == 0 and seq % GRID_W == 0
    assert n % GATHER_TOKENS == 0 and ROW_TILE % GATHER_TOKENS == 0
    blocks_per_batch = tb // ROW_TILE
    ctx_blocks = ctx_len // ROW_TILE
    nchunks = tb // CHUNK

    def mod_row(i):
        return jnp.where(i % blocks_per_batch < ctx_blocks, batch, i // blocks_per_batch)

    def mod_row_group(i):
        return mod_row(i // (ROW_TILE // GATHER_TOKENS))

    cond_rows = -(-(batch + 1) // 8) * 8
    cond = jnp.zeros((cond_rows, d), F32).at[:batch].set(c).at[batch].set(c_ctx)
    mods = ada_modulation_all(cond, ada_w, ada_b).reshape(depth, cond_rows, 6, d)

    xs = jnp.concatenate([ctx, x], axis=1).reshape(n, d)
    lane_pad = LANE - N_GATES
    for l in range(depth):
        mod = mods[l]
        w_main = w_in[l, :, :OFF_G].astype(BF16)
        w_gate = jnp.pad(w_in[l, :, OFF_G:], ((0, 0), (0, lane_pad))).astype(BF16)
        p, graw = in_proj(xs, mod, norm_mix[l][None], w_main, w_gate, mod_row)

        y_a = sgu_mixer(p, sgu_norm[l][None], sgu_w[l].astype(BF16), sgu_b[l].T)

        gp = gate_prep(graw[0], jnp.pad(b_gate[l], (0, lane_pad))[None])
        gp = gp[:, :N_GATES].reshape(n, 2, 2, MLSTM_HEADS)
        gcol = gp.transpose(3, 0, 1, 2).reshape(MLSTM_HEADS, n, 4)
        grow = gcol.reshape(MLSTM_HEADS, n // CHUNK, CHUNK, 4).transpose(0, 1, 3, 2)
        qk = qk_conv(p, qk_conv_w[l], blocks_per_batch, ctx_blocks)
        y_b = mlstm_mixer(p, qk, gcol, grow, mlstm_norm[l][None], batch, tb, ctx_len // CHUNK)

        y_c = pool_mixer(p, pool_w[l].astype(BF16), pool_scale[l].reshape(POOL_GROUPS, 1, POOL_GC),
                         batch, tb, ctx_len, seq)

        xn = out_proj(xs, y_a, y_b, y_c, mod, w_out[l].astype(BF16), mod_row)

        hf, e, gates = peer_route(xn, mod, norm_ffn[l][None], peer_wq[l].astype(BF16),
                                  peer_keys[l].astype(BF16), mod_row)
        ids = e.reshape(PEER_PAIRS, n).T.reshape(n // GATHER_TOKENS, 1, GATHER_TOKENS * PEER_PAIRS)
        gates_t = gates.reshape(PEER_PAIRS, n // GATHER_TOKENS, GATHER_TOKENS).transpose(1, 0, 2)
        xs = peer_gather(ids, hf, gates_t, xn, mod[:, 5], pack_expert_tables(peer_u[l], peer_v[l]),
                         mod_row_group)

    return final_norm(xs, norm_final[None], batch, seq, ctx_len)
```

```python
import functools

import jax
import jax.numpy as jnp
from jax import lax
from jax.experimental import pallas as pl
from jax.experimental.pallas import tpu as pltpu

F32 = jnp.float32
BF16 = jnp.bfloat16
I32 = jnp.int32

EPS = 1e-6
GRID_W = 64
GRID_W_LOG2 = 6
D_MODEL = 2048
D_SGU = D_MODEL // 4
SGU_HEADS = 4
SGU_HD = D_SGU // SGU_HEADS
CHUNK = 128
D_MLSTM = D_MODEL // 2
MLSTM_HEADS = 4
MLSTM_HD = D_MLSTM // MLSTM_HEADS
D_POOL = D_MODEL // 4
POOL_GROUPS = 4
POOL_GC = D_POOL // POOL_GROUPS
N_GATES = 2 * 2 * MLSTM_HEADS
OFF_U = 0
OFF_V = OFF_U + D_SGU
OFF_P = OFF_V + D_SGU
OFF_Q = OFF_P + D_POOL
OFF_O = OFF_Q + D_MLSTM
OFF_K = OFF_O + D_MLSTM
OFF_VM = OFF_K + D_MLSTM
OFF_G = OFF_VM + D_MLSTM
N_KEYS = 128
PEER_HEADS = 8
PEER_TOPK = 16
PEER_DK = 256
PEER_PAIRS = PEER_HEADS * PEER_TOPK

LANE = 128
ROW_TILE = 256
POOL_HALO_TILES = -(-(GRID_W << (POOL_GROUPS - 1)) // ROW_TILE)
GATHER_TOKENS = 8
GATHER_SLOTS = 3
VMEM_LIMIT = 56 * 1024 * 1024

CAND_PAIRS = [(a, b) for a in range(PEER_TOPK) for b in range(PEER_TOPK)
              if (a + 1) * (b + 1) <= PEER_TOPK]
CAND_ROWS = 56


def _params(sem, vmem=VMEM_LIMIT):
    return pltpu.CompilerParams(dimension_semantics=sem, vmem_limit_bytes=vmem)


def _norm_mod(x, g, shift, scale):
    ms = jnp.mean(x * x, axis=-1, keepdims=True)
    y = x * lax.rsqrt(ms + EPS)
    return (y * g) * (1.0 + scale) + shift


def _gelu(x):
    return jax.nn.gelu(x, approximate=True)


def _sigmoid(x):
    return 1.0 / (1.0 + jnp.exp(-x))


def _ada_kernel(cond_ref, w_ref, b_ref, o_ref):
    a = cond_ref[...]
    a = (a * _sigmoid(a)).astype(BF16)
    o_ref[0] = jnp.dot(a, w_ref[0].astype(BF16), preferred_element_type=F32) + b_ref[0]


def ada_modulation_all(cond, ada_w, ada_b):
    depth, d, d6 = ada_w.shape
    rows = cond.shape[0]
    tn = 1024
    return pl.pallas_call(
        _ada_kernel,
        grid=(depth, d6 // tn),
        in_specs=[pl.BlockSpec((rows, d), lambda l, j: (0, 0)),
                  pl.BlockSpec((1, d, tn), lambda l, j: (l, 0, j)),
                  pl.BlockSpec((1, 1, tn), lambda l, j: (l, 0, j))],
        out_specs=pl.BlockSpec((1, rows, tn), lambda l, j: (l, 0, j)),
        out_shape=jax.ShapeDtypeStruct((depth, rows, d6), F32),
        compiler_params=_params(("parallel", "parallel")),
    )(cond, ada_w, ada_b.reshape(depth, 1, d6))


def _in_proj_kernel(x_ref, mod_ref, g_ref, w_ref, wg_ref, p_ref, gate_ref, *, ncols):
    h = _norm_mod(x_ref[...], g_ref[...], mod_ref[0, 0:1, :], mod_ref[0, 1:2, :]).astype(BF16)
    for j in range(ncols // 256):
        sl = slice(j * 256, (j + 1) * 256)
        p_ref[:, sl] = jnp.dot(h, w_ref[:, sl], preferred_element_type=F32).astype(BF16)
    gate_ref[0] = jnp.dot(h, wg_ref[...], preferred_element_type=F32)


def in_proj(x, mod, norm_g, w_main, w_gate, mod_row):
    n, d = x.shape
    ncol_blocks = 2
    ncols = OFF_G // ncol_blocks
    return pl.pallas_call(
        functools.partial(_in_proj_kernel, ncols=ncols),
        grid=(ncol_blocks, n // ROW_TILE),
        in_specs=[pl.BlockSpec((ROW_TILE, d), lambda c, i: (i, 0)),
                  pl.BlockSpec((1, 6, d), lambda c, i: (mod_row(i), 0, 0)),
                  pl.BlockSpec((1, d), lambda c, i: (0, 0)),
                  pl.BlockSpec((d, ncols), lambda c, i: (0, c)),
                  pl.BlockSpec((d, LANE), lambda c, i: (0, 0))],
        out_specs=[pl.BlockSpec((ROW_TILE, ncols), lambda c, i: (i, c)),
                   pl.BlockSpec((1, ROW_TILE, LANE), lambda c, i: (c, i, 0))],
        out_shape=[jax.ShapeDtypeStruct((n, OFF_G), BF16),
                   jax.ShapeDtypeStruct((ncol_blocks, n, LANE), F32)],
        compiler_params=_params(("arbitrary", "arbitrary")),
    )(x, mod, norm_g, w_main, w_gate)


def _sgu_kernel(u_ref, v_ref, g_ref, ws_ref, bs_ref, y_ref):
    u = _gelu(u_ref[...].astype(F32))
    v = _gelu(v_ref[...].astype(F32))
    ms = jnp.mean(v * v, axis=-1, keepdims=True)
    v = (v * lax.rsqrt(ms + EPS) * g_ref[...]).astype(BF16)
    for c in range(ROW_TILE // CHUNK):
        rows = slice(c * CHUNK, (c + 1) * CHUNK)
        for h in range(SGU_HEADS):
            cols = slice(h * SGU_HD, (h + 1) * SGU_HD)
            mixed = jnp.dot(ws_ref[h], v[rows, cols], preferred_element_type=F32) + bs_ref[:, h:h + 1]
            y_ref[rows, cols] = (u[rows, cols] * mixed).astype(BF16)


def sgu_mixer(p, sgu_norm, sgu_w, sgu_bt):
    n = p.shape[0]
    return pl.pallas_call(
        _sgu_kernel,
        grid=(n // ROW_TILE,),
        in_specs=[pl.BlockSpec((ROW_TILE, D_SGU), lambda i: (i, OFF_U // D_SGU)),
                  pl.BlockSpec((ROW_TILE, D_SGU), lambda i: (i, OFF_V // D_SGU)),
                  pl.BlockSpec((1, D_SGU), lambda i: (0, 0)),
                  pl.BlockSpec((SGU_HEADS, CHUNK, CHUNK), lambda i: (0, 0, 0)),
                  pl.BlockSpec((CHUNK, SGU_HEADS), lambda i: (0, 0))],
        out_specs=pl.BlockSpec((ROW_TILE, D_SGU), lambda i: (i, 0)),
        out_shape=jax.ShapeDtypeStruct((n, D_SGU), BF16),
        compiler_params=_params(("parallel",)),
    )(p, p, sgu_norm, sgu_w, sgu_bt)


def _gate_prep_kernel(g_ref, bias_ref, o_ref):
    g = g_ref[...] + bias_ref[...]
    lf = jnp.minimum(g, 0.0) - jnp.log(1.0 + jnp.exp(-jnp.abs(g)))
    row = lax.broadcasted_iota(I32, (CHUNK, CHUNK), 0)
    col = lax.broadcasted_iota(I32, (CHUNK, CHUNK), 1)
    lower = (col <= row).astype(F32)
    upper = (col >= row).astype(F32)
    b_fwd = jnp.dot(lower, lf, preferred_element_type=F32, precision=lax.Precision.HIGHEST)
    b_bwd = jnp.dot(upper, lf, preferred_element_type=F32, precision=lax.Precision.HIGHEST)
    lane = lax.broadcasted_iota(I32, (1, LANE), 1)
    is_input_gate = (lane // MLSTM_HEADS) % 2 == 0
    is_fwd = lane < 2 * MLSTM_HEADS
    o_ref[...] = jnp.where(is_input_gate, g, jnp.where(is_fwd, b_fwd, b_bwd))


def gate_prep(graw, bias):
    n = graw.shape[0]
    return pl.pallas_call(
        _gate_prep_kernel,
        grid=(n // CHUNK,),
        in_specs=[pl.BlockSpec((CHUNK, LANE), lambda i: (i, 0)),
                  pl.BlockSpec((1, LANE), lambda i: (0, 0))],
        out_specs=pl.BlockSpec((CHUNK, LANE), lambda i: (i, 0)),
        out_shape=jax.ShapeDtypeStruct((n, LANE), F32),
        compiler_params=_params(("parallel",)),
    )(graw, bias)


HALO = 16


def _qk_conv_kernel(xm_ref, xp_ref, xn_ref, w_ref, o_ref, *, blocks_per_batch, ctx_blocks):
    i = pl.program_id(0)
    j = pl.program_id(1)
    x = xm_ref[...].astype(F32)
    tm = x.shape[0]
    prev_row = xp_ref[...].astype(F32)[HALO - 1:HALO, :]
    next_row = xn_ref[...].astype(F32)[0:1, :]
    ib = i % blocks_per_batch
    at_start = jnp.logical_or(ib == 0, ib == ctx_blocks)
    at_end = jnp.logical_or(ib == ctx_blocks - 1, ib == blocks_per_batch - 1)
    prev_row = jnp.where(at_start, 0.0, prev_row)
    next_row = jnp.where(at_end, 0.0, next_row)
    rows = lax.broadcasted_iota(I32, (tm, 1), 0)
    x_prev = jnp.where(rows == 0, prev_row, pltpu.roll(x, 1, 0))
    x_next = jnp.where(rows == tm - 1, next_row, pltpu.roll(x, tm - 1, 0))
    y = w_ref[0:1, :] * x_prev + w_ref[1:2, :] * x + w_ref[2:3, :] * x_next
    y = y * _sigmoid(y)
    y = y * jnp.where(j >= 2, MLSTM_HD ** -0.5, 1.0)
    o_ref[...] = y.astype(BF16)


def qk_conv(p, conv_w, blocks_per_batch, ctx_blocks):
    n = p.shape[0]
    tc = 512
    nhalo = n // HALO
    per = ROW_TILE // HALO

    def col(j):
        return jnp.where(j < 2, OFF_Q // tc + j, OFF_K // tc + j - 2)

    return pl.pallas_call(
        functools.partial(_qk_conv_kernel, blocks_per_batch=blocks_per_batch, ctx_blocks=ctx_blocks),
        grid=(n // ROW_TILE, 2 * D_MLSTM // tc),
        in_specs=[pl.BlockSpec((ROW_TILE, tc), lambda i, j: (i, col(j))),
                  pl.BlockSpec((HALO, tc), lambda i, j: (jnp.maximum(i * per - 1, 0), col(j))),
                  pl.BlockSpec((HALO, tc), lambda i, j: (jnp.minimum((i + 1) * per, nhalo - 1), col(j))),
                  pl.BlockSpec((3, tc), lambda i, j: (0, j))],
        out_specs=pl.BlockSpec((ROW_TILE, tc), lambda i, j: (i, j)),
        out_shape=jax.ShapeDtypeStruct((n, 2 * D_MLSTM), BF16),
        compiler_params=_params(("parallel", "parallel")),
    )(p, p, p, conv_w)


def _mlstm_kernel(q_ref, k_ref, v_ref, o_ref, gc_ref, gr_ref, nrm_ref, y_ref,
                  hf_ref, hb_ref, c_ref, n_ref, m_ref, *, nchunks, nctx):
    c_ref[...] = jnp.zeros_like(c_ref)
    n_ref[...] = jnp.zeros_like(n_ref)
    m_ref[...] = jnp.zeros_like(m_ref)
    row = lax.broadcasted_iota(I32, (CHUNK, CHUNK), 0)
    col = lax.broadcasted_iota(I32, (CHUNK, CHUNK), 1)
    seen = (row >= col, row <= col)
    h_refs = (hf_ref, hb_ref)

    def step(d, c):
        r0 = pl.multiple_of(c * CHUNK, CHUNK)
        rows = pl.ds(r0, CHUNK)
        q = q_ref[rows, :]
        k = k_ref[rows, :]
        v = v_ref[rows, :]
        gcol = gc_ref[0, rows, :]
        grow = gr_ref[0, c]
        ig_c, b_c = gcol[:, 2 * d:2 * d + 1], gcol[:, 2 * d + 1:2 * d + 2]
        ig_r, b_r = grow[2 * d:2 * d + 1, :], grow[2 * d + 1:2 * d + 2, :]
        b_last = b_c[CHUNK - 1:CHUNK, :] if d == 0 else b_c[0:1, :]
        m = m_ref[d, 0:1, 0:1]
        g_c = b_last - b_c + ig_c
        m_new = jnp.maximum(b_last + m, jnp.max(g_c, axis=0, keepdims=True))
        wk_c = jnp.exp(g_c - m_new)
        decay = jnp.exp(b_last + m - m_new)

        log_w = jnp.where(seen[d], b_c - b_r + ig_r, -jnp.inf)
        inter = b_c + m
        m_t = jnp.maximum(inter, jnp.max(log_w, axis=1, keepdims=True))
        w_prev = jnp.exp(inter - m_t)
        s = lax.dot_general(q, k, (((1,), (1,)), ((), ())), preferred_element_type=F32)
        s = s * jnp.exp(log_w - m_t)
        ct = c_ref[d]
        nvec = n_ref[d]
        num = (jnp.dot(s.astype(BF16), v, preferred_element_type=F32)
               + w_prev * jnp.dot(q, ct.astype(BF16), preferred_element_type=F32))
        qf = q.astype(F32)
        kf = k.astype(F32)
        den = jnp.sum(s, axis=1, keepdims=True) + w_prev * jnp.sum(qf * nvec, axis=1, keepdims=True)
        h_refs[d][rows, :] = num / jnp.maximum(jnp.abs(den), jnp.exp(-m_t))

        wv = (v.astype(F32) * wk_c).astype(BF16)
        c_ref[d] = decay * ct + lax.dot_general(k, wv, (((0,), (0,)), ((), ())),
                                                preferred_element_type=F32)
        n_ref[d] = decay * nvec + jnp.sum(kf * wk_c, axis=0, keepdims=True)
        m_ref[d] = jnp.broadcast_to(m_new, (1, LANE))

    def body(i, carry):
        step(0, i)
        step(1, jnp.where(i < nctx, nctx - 1 - i, nchunks - 1 - (i - nctx)))
        return carry

    lax.fori_loop(0, nchunks, body, 0)

    def finish(c, carry):
        rows = pl.ds(pl.multiple_of(c * CHUNK, CHUNK), CHUNK)
        h = hf_ref[rows, :] + hb_ref[rows, :]
        hn = h * lax.rsqrt(jnp.mean(h * h, axis=-1, keepdims=True) + EPS) * nrm_ref[...]
        y_ref[rows, :] = (_sigmoid(o_ref[rows, :].astype(F32)) * hn).astype(BF16)
        return carry

    lax.fori_loop(0, nchunks, finish, 0)


def mlstm_mixer(p, qk, gcol, grow, mlstm_norm, batch, tb, nctx):
    n = p.shape[0]
    nchunks = tb // CHUNK
    hd = MLSTM_HD
    return pl.pallas_call(
        functools.partial(_mlstm_kernel, nchunks=nchunks, nctx=nctx),
        grid=(batch, MLSTM_HEADS),
        in_specs=[pl.BlockSpec((tb, hd), lambda b, h: (b, h)),
                  pl.BlockSpec((tb, hd), lambda b, h: (b, MLSTM_HEADS + h)),
                  pl.BlockSpec((tb, hd), lambda b, h: (b, OFF_VM // hd + h)),
                  pl.BlockSpec((tb, hd), lambda b, h: (b, OFF_O // hd + h)),
                  pl.BlockSpec((1, tb, 4), lambda b, h: (h, b, 0)),
                  pl.BlockSpec((1, nchunks, 4, CHUNK), lambda b, h: (h, b, 0, 0)),
                  pl.BlockSpec((1, hd), lambda b, h: (0, h))],
        out_specs=pl.BlockSpec((tb, hd), lambda b, h: (b, h)),
        out_shape=jax.ShapeDtypeStruct((n, D_MLSTM), BF16),
        scratch_shapes=[pltpu.VMEM((tb, hd), F32), pltpu.VMEM((tb, hd), F32),
                        pltpu.VMEM((2, hd, hd), F32), pltpu.VMEM((2, 1, hd), F32),
                        pltpu.VMEM((2, 1, LANE), F32)],
        compiler_params=_params(("parallel", "parallel")),
    )(qk, qk, p, p, gcol, grow, mlstm_norm)


def _pool_kernel(xfull_ref, xrow_ref, w_ref, sc_ref, y_ref, m_ref, inv_ref, *, tb, ctx, seq):
    g = pl.program_id(0)
    i = pl.program_id(1)
    b = pl.program_id(2)
    tm = xrow_ref.shape[0]
    half = jnp.left_shift(1, g)
    win = 2 * half

    def grid_coords(pos):
        is_ctx = pos < ctx
        lat = pos - ctx
        r = jnp.where(is_ctx, 0, jnp.right_shift(lat, GRID_W_LOG2))
        c = jnp.where(is_ctx, pos, jnp.bitwise_and(lat, GRID_W - 1))
        return is_ctx, r, c

    band = m_ref.shape[1]
    first_tile = jnp.clip(i - POOL_HALO_TILES, 0, (tb - band) // tm)
    s0 = pl.multiple_of(first_tile * tm, tm)

    @pl.when(b == 0)
    def _build_window_matrix():
        t = i * tm + lax.broadcasted_iota(I32, (tm, 1), 0)
        s = s0 + lax.broadcasted_iota(I32, (1, band), 1)
        ctx_t, r_t, c_t = grid_coords(t)
        ctx_s, r_s, c_s = grid_coords(s)
        dr = r_s - r_t + half
        dc = c_s - c_t + half
        inside = (ctx_t == ctx_s) & (dr >= 0) & (dr < win) & (dc >= 0) & (dc < win)
        m_ref[...] = jnp.where(inside, 1.0, 0.0).astype(BF16)
        nrows = jnp.where(ctx_t, 1, seq // GRID_W)
        ncols = jnp.where(ctx_t, ctx, GRID_W)
        cnt_r = jnp.minimum(r_t - half + win, nrows) - jnp.maximum(r_t - half, 0)
        cnt_c = jnp.minimum(c_t - half + win, ncols) - jnp.maximum(c_t - half, 0)
        inv_ref[...] = 1.0 / (cnt_r * cnt_c).astype(F32)

    tot = jnp.dot(m_ref[...], xfull_ref[pl.ds(s0, band), :], preferred_element_type=F32)
    d = (tot * inv_ref[...] - xrow_ref[...].astype(F32)).astype(BF16)
    y = jnp.dot(d, w_ref[0], preferred_element_type=F32) * sc_ref[0]
    y_ref[...] = y.astype(BF16)


def pool_mixer(p, pool_w, pool_scale, batch, tb, ctx, seq):
    n = p.shape[0]
    nrb = tb // ROW_TILE
    c0 = OFF_P // POOL_GC
    band = min(2 * POOL_HALO_TILES + 1, nrb) * ROW_TILE
    return pl.pallas_call(
        functools.partial(_pool_kernel, tb=tb, ctx=ctx, seq=seq),
        grid=(POOL_GROUPS, nrb, batch),
        in_specs=[pl.BlockSpec((tb, POOL_GC), lambda g, i, b: (b, c0 + g)),
                  pl.BlockSpec((ROW_TILE, POOL_GC), lambda g, i, b: (b * nrb + i, c0 + g)),
                  pl.BlockSpec((1, POOL_GC, POOL_GC), lambda g, i, b: (g, 0, 0)),
                  pl.BlockSpec((1, 1, POOL_GC), lambda g, i, b: (g, 0, 0))],
        out_specs=pl.BlockSpec((ROW_TILE, POOL_GC), lambda g, i, b: (b * nrb + i, g)),
        out_shape=jax.ShapeDtypeStruct((n, D_POOL), BF16),
        scratch_shapes=[pltpu.VMEM((ROW_TILE, band), BF16), pltpu.VMEM((ROW_TILE, 1), F32)],
        compiler_params=_params(("arbitrary", "arbitrary", "arbitrary")),
    )(p, p, pool_w, pool_scale)


def _out_proj_kernel(x_ref, ya_ref, yb_ref, yc_ref, mod_ref, w_ref, o_ref):
    ya, yb, yc = ya_ref[...], yb_ref[...], yc_ref[...]
    k1 = D_SGU
    k2 = D_SGU + D_MLSTM
    for j in range(D_MODEL // 256):
        sl = slice(j * 256, (j + 1) * 256)
        acc = jnp.dot(ya, w_ref[0:k1, sl], preferred_element_type=F32)
        acc += jnp.dot(yb, w_ref[k1:k2, sl], preferred_element_type=F32)
        acc += jnp.dot(yc, w_ref[k2:D_MODEL, sl], preferred_element_type=F32)
        o_ref[:, sl] = x_ref[:, sl] + mod_ref[0, 2:3, sl] * acc


def out_proj(x, ya, yb, yc, mod, w_out, mod_row):
    n, d = x.shape
    return pl.pallas_call(
        _out_proj_kernel,
        grid=(n // ROW_TILE,),
        in_specs=[pl.BlockSpec((ROW_TILE, d), lambda i: (i, 0)),
                  pl.BlockSpec((ROW_TILE, D_SGU), lambda i: (i, 0)),
                  pl.BlockSpec((ROW_TILE, D_MLSTM), lambda i: (i, 0)),
                  pl.BlockSpec((ROW_TILE, D_POOL), lambda i: (i, 0)),
                  pl.BlockSpec((1, 6, d), lambda i: (mod_row(i), 0, 0)),
                  pl.BlockSpec((d, d), lambda i: (0, 0))],
        out_specs=pl.BlockSpec((ROW_TILE, d), lambda i: (i, 0)),
        out_shape=jax.ShapeDtypeStruct((n, d), F32),
        compiler_params=_params(("parallel",)),
    )(x, ya, yb, yc, mod, w_out)


def _topk_rows(vals, k):
    nrows = vals.shape[0]
    iota = lax.broadcasted_iota(I32, vals.shape, 0)
    out_v, out_i = [], []
    for _ in range(k):
        mx = jnp.max(vals, axis=0, keepdims=True)
        am = jnp.min(jnp.where(vals == mx, iota, nrows), axis=0, keepdims=True)
        out_v.append(mx)
        out_i.append(am)
        vals = jnp.where(iota == am, -jnp.inf, vals)
    return out_v, out_i


def _peer_route_kernel(x_ref, mod_ref, g_ref, wq_ref, keys_ref, hf_ref, e_ref, gate_ref,
                       hb_ref, cand_ref, cid_ref):
    h = pl.program_id(1)

    @pl.when(h == 0)
    def _norm():
        hf = _norm_mod(x_ref[...], g_ref[...], mod_ref[0, 3:4, :], mod_ref[0, 4:5, :])
        hf_ref[...] = hf
        hb_ref[...] = hf.astype(BF16)

    q = jnp.dot(hb_ref[...], wq_ref[...], preferred_element_type=F32).astype(BF16)
    half = PEER_DK // 2
    nt = (((1,), (1,)), ((), ()))
    scores1 = lax.dot_general(keys_ref[0], q[:, :half], nt, preferred_element_type=F32)
    scores2 = lax.dot_general(keys_ref[1], q[:, half:], nt, preferred_element_type=F32)
    npad = CAND_ROWS - len(CAND_PAIRS)
    for c in range(q.shape[0] // LANE):
        lanes = slice(c * LANE, (c + 1) * LANE)
        s1, i1 = _topk_rows(scores1[:, lanes], PEER_TOPK)
        s2, i2 = _topk_rows(scores2[:, lanes], PEER_TOPK)
        for r, (a, b) in enumerate(CAND_PAIRS):
            cand_ref[r:r + 1, lanes] = s1[a] + s2[b]
            cid_ref[r:r + 1, lanes] = i1[a] * N_KEYS + i2[b]
        cand_ref[len(CAND_PAIRS):, lanes] = jnp.full((npad, LANE), -jnp.inf, F32)
        cid_ref[len(CAND_PAIRS):, lanes] = jnp.zeros((npad, LANE), I32)
        sc, slot = _topk_rows(cand_ref[:, lanes], PEER_TOPK)
        cid = cid_ref[:, lanes]
        riota = lax.broadcasted_iota(I32, cid.shape, 0)
        ex = [jnp.exp(v - sc[0]) for v in sc]
        tot = ex[0]
        for v in ex[1:]:
            tot = tot + v
        for j in range(PEER_TOPK):
            e_ref[0, j:j + 1, lanes] = jnp.sum(jnp.where(riota == slot[j], cid, 0), axis=0, keepdims=True)
            gate_ref[0, j:j + 1, lanes] = ex[j] / tot


def peer_route(x, mod, norm_g, wq, keys, mod_row):
    n, d = x.shape
    tt = ROW_TILE
    return pl.pallas_call(
        _peer_route_kernel,
        grid=(n // tt, PEER_HEADS),
        in_specs=[pl.BlockSpec((tt, d), lambda i, h: (i, 0)),
                  pl.BlockSpec((1, 6, d), lambda i, h: (mod_row(i), 0, 0)),
                  pl.BlockSpec((1, d), lambda i, h: (0, 0)),
                  pl.BlockSpec((d, PEER_DK), lambda i, h: (0, h)),
                  pl.BlockSpec((2, N_KEYS, PEER_DK // 2), lambda i, h: (0, 0, 0))],
        out_specs=[pl.BlockSpec((tt, d), lambda i, h: (i, 0)),
                   pl.BlockSpec((1, PEER_TOPK, tt), lambda i, h: (h, 0, i)),
                   pl.BlockSpec((1, PEER_TOPK, tt), lambda i, h: (h, 0, i))],
        out_shape=[jax.ShapeDtypeStruct((n, d), F32),
                   jax.ShapeDtypeStruct((PEER_HEADS, PEER_TOPK, n), I32),
                   jax.ShapeDtypeStruct((PEER_HEADS, PEER_TOPK, n), F32)],
        scratch_shapes=[pltpu.VMEM((tt, d), BF16), pltpu.VMEM((CAND_ROWS, tt), F32),
                        pltpu.VMEM((CAND_ROWS, tt), I32)],
        compiler_params=_params(("parallel", "arbitrary")),
    )(x, mod, norm_g, wq, keys)


ISSUE_UNROLL = 8
SUBLANES = 8
FOLD_ORDER = (0, 4, 2, 6, 1, 5, 3, 7)


def _fold(a, b, shift, first):
    return (jnp.where(first, a, pltpu.roll(b, shift, 0))
            + jnp.where(first, pltpu.roll(a, SUBLANES - shift, 0), b))


def _transpose8(vs, sub):
    vs = list(vs)
    for d in (4, 2, 1):
        keep = jnp.bitwise_and(sub, d) == 0
        nxt = list(vs)
        for i in range(SUBLANES):
            if i & d == 0:
                a, b = vs[i], vs[i + d]
                nxt[i] = jnp.where(keep, a, pltpu.roll(b, d, 0))
                nxt[i + d] = jnp.where(keep, pltpu.roll(a, SUBLANES - d, 0), b)
        vs = nxt
    return vs


def _unpack_pair(words):
    lo = pltpu.bitcast(jnp.left_shift(words, 16), F32)
    hi = pltpu.bitcast(jnp.bitwise_and(words, jnp.uint32(0xFFFF0000)), F32)
    return lo, hi


def _peer_gather_kernel(ids_ref, id1_ref, idn_ref, hf_ref, gt_ref, xn_ref, mod_ref, tab_hbm, o_ref,
                        tbuf, wbuf, sem, *, ngroups):
    i = pl.program_id(0)
    slot = lax.rem(i, GATHER_SLOTS)
    nslot = lax.rem(i + GATHER_SLOTS - 1, GATHER_SLOTS)
    nrows = GATHER_TOKENS * PEER_PAIRS

    def start_row(ids, r, s, priority):
        pltpu.make_async_copy(tab_hbm.at[ids[0, 0, r]], tbuf.at[s, r], sem.at[s]).start(priority=priority)

    def wait_slot(s):
        pltpu.make_async_copy(tab_hbm.at[pl.ds(0, nrows)], tbuf.at[s], sem.at[s]).wait()

    @pl.when(i == 0)
    def _first_groups():
        def body(g, carry):
            for j in range(ISSUE_UNROLL):
                start_row(ids_ref, g * ISSUE_UNROLL + j, 0, j % 2)
                start_row(id1_ref, g * ISSUE_UNROLL + j, 1, j % 2)
            return carry
        lax.fori_loop(0, nrows // ISSUE_UNROLL, body, 0)

    wait_slot(slot)

    sub = lax.broadcasted_iota(I32, (SUBLANES, LANE), 0)
    first4, first2, first1 = sub < 4, sub % 4 < 2, sub % 2 < 1
    ngrp = PEER_PAIRS // SUBLANES
    early = 6
    late = SUBLANES - early
    def chunks(ref, first):
        return [ref[:, (first + c) * LANE:(first + c + 1) * LANE] for c in range(SUBLANES)]
    xs_lo = _transpose8(chunks(hf_ref, 0), sub)
    xs_hi = _transpose8(chunks(hf_ref, SUBLANES), sub)
    out_lo, out_hi = [], []
    for t in range(GATHER_TOKENS):
        x_lo = xs_lo[t]
        x_hi = xs_hi[t]
        base = t * PEER_PAIRS
        for j in range(ngrp):
            for q in range(early):
                start_row(idn_ref, base + j * early + q, nslot, q % 2)
            r = []
            for p in FOLD_ORDER:
                lo, hi = _unpack_pair(tbuf[slot, base + j * SUBLANES + p, 0])
                r.append(lo * x_lo + hi * x_hi)
            l1 = [_fold(r[0], r[1], 4, first4), _fold(r[2], r[3], 4, first4),
                  _fold(r[4], r[5], 4, first4), _fold(r[6], r[7], 4, first4)]
            l2 = [_fold(l1[0], l1[1], 2, first2), _fold(l1[2], l1[3], 2, first2)]
            y = _fold(l2[0], l2[1], 1, first1)
            rows = slice(j * SUBLANES, (j + 1) * SUBLANES)
            a = jnp.sum(y, axis=1, keepdims=True)
            w = gt_ref[0, rows, t:t + 1] * _gelu(a)
            wbuf[rows, :] = jnp.broadcast_to(w, (SUBLANES, LANE))
        nacc = 2
        acc_lo = [jnp.zeros((SUBLANES, LANE), F32) for _ in range(nacc)]
        acc_hi = [jnp.zeros((SUBLANES, LANE), F32) for _ in range(nacc)]
        late0 = ngrp * early
        for k in range(PEER_PAIRS):
            if k % SUBLANES == 0:
                for q in range(late0 + (k // SUBLANES) * late, late0 + (k // SUBLANES + 1) * late):
                    start_row(idn_ref, base + q, nslot, q % 2)
            lo, hi = _unpack_pair(tbuf[slot, base + k, 1])
            wk = wbuf[k:k + 1, :]
            acc_lo[k % nacc] = acc_lo[k % nacc] + lo * wk
            acc_hi[k % nacc] = acc_hi[k % nacc] + hi * wk
        out_lo.append(acc_lo[0] + acc_lo[1])
        out_hi.append(acc_hi[0] + acc_hi[1])

    for first, outs in ((0, out_lo), (SUBLANES, out_hi)):
        for c, rows_c in enumerate(_transpose8(outs, sub)):
            cols = slice((first + c) * LANE, (first + c + 1) * LANE)
            o_ref[:, cols] = xn_ref[:, cols] + mod_ref[0, 5:6, cols] * rows_c

    @pl.when(i == ngroups - 1)
    def _drain():
        wait_slot(lax.rem(i + 1, GATHER_SLOTS))
        wait_slot(nslot)


PACK_ROWS = 128


def _pack_kernel(u_ref, v_ref, o_ref):
    sub = lax.broadcasted_iota(I32, (SUBLANES, LANE), 0)
    half = D_MODEL // 2
    for g in range(PACK_ROWS // SUBLANES):
        rows = slice(g * SUBLANES, (g + 1) * SUBLANES)
        for which, ref in enumerate((u_ref, v_ref)):
            bits = pltpu.bitcast(ref[0, rows, :], jnp.uint32)
            rnd = bits + jnp.uint32(0x7FFF) + jnp.bitwise_and(jnp.right_shift(bits, 16), jnp.uint32(1))
            words = jnp.bitwise_or(jnp.right_shift(rnd[:, :half], 16),
                                   jnp.bitwise_and(rnd[:, half:], jnp.uint32(0xFFFF0000)))
            tiles = _transpose8([words[:, c * LANE:(c + 1) * LANE] for c in range(SUBLANES)], sub)
            for t in range(SUBLANES):
                o_ref[0, g * SUBLANES + t, which] = tiles[t]


def pack_expert_tables(u_tabs, v_tabs):
    depth, nexp, d = u_tabs.shape
    in_spec = pl.BlockSpec((1, PACK_ROWS, d), lambda l, i: (l, i, 0))
    return pl.pallas_call(
        _pack_kernel,
        grid=(depth, nexp // PACK_ROWS),
        in_specs=[in_spec, in_spec],
        out_specs=pl.BlockSpec((1, PACK_ROWS, 2, SUBLANES, LANE), lambda l, i: (l, i, 0, 0, 0)),
        out_shape=jax.ShapeDtypeStruct((depth, nexp, 2, SUBLANES, LANE), jnp.uint32),
        compiler_params=_params(("parallel", "parallel")),
    )(u_tabs, v_tabs)


def peer_gather(ids, hf, gates_t, xn, mod, table, mod_row_group):
    n, d = xn.shape
    gt = GATHER_TOKENS
    ngroups = n // gt
    assert ngroups >= GATHER_SLOTS - 1 and gt == SUBLANES
    nrows = gt * PEER_PAIRS
    tok_spec = pl.BlockSpec((gt, d), lambda i: (i, 0))

    def ids_spec(ahead):
        return pl.BlockSpec((1, 1, nrows), lambda i: (jnp.minimum(i + ahead, ngroups - 1), 0, 0),
                            memory_space=pltpu.SMEM)

    return pl.pallas_call(
        functools.partial(_peer_gather_kernel, ngroups=ngroups),
        grid=(ngroups,),
        in_specs=[ids_spec(0), ids_spec(1), ids_spec(GATHER_SLOTS - 1),
                  tok_spec,
                  pl.BlockSpec((1, PEER_PAIRS, gt), lambda i: (i, 0, 0)),
                  tok_spec,
                  pl.BlockSpec((1, 6, d), lambda i: (mod_row_group(i), 0, 0)),
                  pl.BlockSpec(memory_space=pl.ANY)],
        out_specs=tok_spec,
        out_shape=jax.ShapeDtypeStruct((n, d), F32),
        scratch_shapes=[pltpu.VMEM((GATHER_SLOTS, nrows, 2, SUBLANES, LANE), jnp.uint32),
                        pltpu.VMEM((PEER_PAIRS, LANE), F32),
                        pltpu.SemaphoreType.DMA((GATHER_SLOTS,))],
        compiler_params=_params(("arbitrary",)),
    )(ids, ids, ids, hf, gates_t, xn, mod, table)


def _final_norm_kernel(x_ref, g_ref, o_ref):
    x = x_ref[...]
    o_ref[0] = x * lax.rsqrt(jnp.mean(x * x, axis=-1, keepdims=True) + EPS) * g_ref[...]


def final_norm(x, g, batch, seq, ctx):
    n, d = x.shape
    tb = ctx + seq
    nrb = tb // ROW_TILE
    cb = ctx // ROW_TILE
    return pl.pallas_call(
        _final_norm_kernel,
        grid=(batch, seq // ROW_TILE),
        in_specs=[pl.BlockSpec((ROW_TILE, d), lambda b, i: (b * nrb + cb + i, 0)),
                  pl.BlockSpec((1, d), lambda b, i: (0, 0))],
        out_specs=pl.BlockSpec((1, ROW_TILE, d), lambda b, i: (b, i, 0)),
        out_shape=jax.ShapeDtypeStruct((batch, seq, d), F32),
        compiler_params=_params(("parallel", "parallel")),
    )(x, g)


def kernel(x, c, ctx, c_ctx, ada_w, ada_b, norm_mix, norm_ffn, w_in, b_gate, sgu_norm, sgu_w, sgu_b,
           qk_conv_w, mlstm_norm, pool_w, pool_scale, w_out, peer_wq, peer_keys, peer_u, peer_v,
           norm_final):
    batch, seq, d = x.shape
    ctx_len = ctx.shape[1]
    depth = ada_w.shape[0]
    tb = ctx_len + seq
    n = batch * tb
    assert d == D_MODEL and w_in.shape[2] == OFF_G + N_GATES
    assert ctx_len % ROW_TILE == 0 and seq % ROW_TILE == 0 and seq % GRID_W == 0
    assert n % GATHER_TOKENS == 0 and ROW_TILE % GATHER_TOKENS == 0
    blocks_per_batch = tb // ROW_TILE
    ctx_blocks = ctx_len // ROW_TILE
    nchunks = tb // CHUNK

    def mod_row(i):
        return jnp.where(i % blocks_per_batch < ctx_blocks, batch, i // blocks_per_batch)

    def mod_row_group(i):
        return mod_row(i // (ROW_TILE // GATHER_TOKENS))

    cond_rows = -(-(batch + 1) // 8) * 8
    cond = jnp.zeros((cond_rows, d), F32).at[:batch].set(c).at[batch].set(c_ctx)
    mods = ada_modulation_all(cond, ada_w, ada_b).reshape(depth, cond_rows, 6, d)

    tables = pack_expert_tables(peer_u, peer_v)
    xs = jnp.concatenate([ctx, x], axis=1).reshape(n, d)
    lane_pad = LANE - N_GATES
    for l in range(depth):
        mod = mods[l]
        w_main = w_in[l, :, :OFF_G].astype(BF16)
        w_gate = jnp.pad(w_in[l, :, OFF_G:], ((0, 0), (0, lane_pad))).astype(BF16)
        p, graw = in_proj(xs, mod, norm_mix[l][None], w_main, w_gate, mod_row)

        y_a = sgu_mixer(p, sgu_norm[l][None], sgu_w[l].astype(BF16), sgu_b[l].T)

        gp = gate_prep(graw[0], jnp.pad(b_gate[l], (0, lane_pad))[None])
        gp = gp[:, :N_GATES].reshape(n, 2, 2, MLSTM_HEADS)
        gcol = gp.transpose(3, 0, 1, 2).reshape(MLSTM_HEADS, n, 4)
        grow = gcol.reshape(MLSTM_HEADS, n // CHUNK, CHUNK, 4).transpose(0, 1, 3, 2)
        qk = qk_conv(p, qk_conv_w[l], blocks_per_batch, ctx_blocks)
        y_b = mlstm_mixer(p, qk, gcol, grow, mlstm_norm[l][None], batch, tb, ctx_len // CHUNK)

        y_c = pool_mixer(p, pool_w[l].astype(BF16), pool_scale[l].reshape(POOL_GROUPS, 1, POOL_GC),
                         batch, tb, ctx_len, seq)

        xn = out_proj(xs, y_a, y_b, y_c, mod, w_out[l].astype(BF16), mod_row)

        hf, e, gates = peer_route(xn, mod, norm_ffn[l][None], peer_wq[l].astype(BF16),
                                  peer_keys[l].astype(BF16), mod_row)
        ids = e.reshape(PEER_PAIRS, n).T.reshape(n // GATHER_TOKENS, 1, GATHER_TOKENS * PEER_PAIRS)
        gates_t = gates.reshape(PEER_PAIRS, n // GATHER_TOKENS, GATHER_TOKENS).transpose(1, 0, 2)
        xs = peer_gather(ids, hf, gates_t, xn, mod, tables[l], mod_row_group)

    return final_norm(xs, norm_final[None], batch, seq, ctx_len)
```

```python
import functools

import jax
import jax.numpy as jnp
from jax import lax
from jax.experimental import pallas as pl
from jax.experimental.pallas import tpu as pltpu

F32 = jnp.float32
BF16 = jnp.bfloat16
I32 = jnp.int32

EPS = 1e-6
GRID_W = 64
GRID_W_LOG2 = 6
D_MODEL = 2048
D_SGU = D_MODEL // 4
SGU_HEADS = 4
SGU_HD = D_SGU // SGU_HEADS
CHUNK = 128
D_MLSTM = D_MODEL // 2
MLSTM_HEADS = 4
MLSTM_HD = D_MLSTM // MLSTM_HEADS
D_POOL = D_MODEL // 4
POOL_GROUPS = 4
POOL_GC = D_POOL // POOL_GROUPS
N_GATES = 2 * 2 * MLSTM_HEADS
OFF_U = 0
OFF_V = OFF_U + D_SGU
OFF_P = OFF_V + D_SGU
OFF_Q = OFF_P + D_POOL
OFF_O = OFF_Q + D_MLSTM
OFF_K = OFF_O + D_MLSTM
OFF_VM = OFF_K + D_MLSTM
OFF_G = OFF_VM + D_MLSTM
N_KEYS = 128
PEER_HEADS = 8
PEER_TOPK = 16
PEER_DK = 256
PEER_PAIRS = PEER_HEADS * PEER_TOPK

LANE = 128
ROW_TILE = 256
POOL_HALO_TILES = -(-(GRID_W << (POOL_GROUPS - 1)) // ROW_TILE)
GATHER_TOKENS = 8
GATHER_SLOTS = 3
VMEM_LIMIT = 56 * 1024 * 1024

CAND_PAIRS = [(a, b) for a in range(PEER_TOPK) for b in range(PEER_TOPK)
              if (a + 1) * (b + 1) <= PEER_TOPK]
CAND_ROWS = 56


def _params(sem, vmem=VMEM_LIMIT):
    return pltpu.CompilerParams(dimension_semantics=sem, vmem_limit_bytes=vmem)


def _norm_mod(x, g, shift, scale):
    ms = jnp.mean(x * x, axis=-1, keepdims=True)
    y = x * lax.rsqrt(ms + EPS)
    return (y * g) * (1.0 + scale) + shift


def _gelu(x):
    return jax.nn.gelu(x, approximate=True)


def _sigmoid(x):
    return 1.0 / (1.0 + jnp.exp(-x))


def _ada_kernel(cond_ref, w_ref, b_ref, o_ref):
    a = cond_ref[...]
    a = (a * _sigmoid(a)).astype(BF16)
    o_ref[0] = jnp.dot(a, w_ref[0].astype(BF16), preferred_element_type=F32) + b_ref[0]


def ada_modulation_all(cond, ada_w, ada_b):
    depth, d, d6 = ada_w.shape
    rows = cond.shape[0]
    tn = 1024
    return pl.pallas_call(
        _ada_kernel,
        grid=(depth, d6 // tn),
        in_specs=[pl.BlockSpec((rows, d), lambda l, j: (0, 0)),
                  pl.BlockSpec((1, d, tn), lambda l, j: (l, 0, j)),
                  pl.BlockSpec((1, 1, tn), lambda l, j: (l, 0, j))],
        out_specs=pl.BlockSpec((1, rows, tn), lambda l, j: (l, 0, j)),
        out_shape=jax.ShapeDtypeStruct((depth, rows, d6), F32),
        compiler_params=_params(("parallel", "parallel")),
    )(cond, ada_w, ada_b.reshape(depth, 1, d6))


def _in_proj_kernel(x_ref, mod_ref, g_ref, w_ref, wg_ref, p_ref, gate_ref, *, ncols):
    h = _norm_mod(x_ref[...], g_ref[...], mod_ref[0, 0:1, :], mod_ref[0, 1:2, :]).astype(BF16)
    for j in range(ncols // 256):
        sl = slice(j * 256, (j + 1) * 256)
        p_ref[:, sl] = jnp.dot(h, w_ref[:, sl], preferred_element_type=F32).astype(BF16)
    gate_ref[0] = jnp.dot(h, wg_ref[...], preferred_element_type=F32)


def in_proj(x, mod, norm_g, w_main, w_gate, mod_row):
    n, d = x.shape
    ncol_blocks = 2
    ncols = OFF_G // ncol_blocks
    return pl.pallas_call(
        functools.partial(_in_proj_kernel, ncols=ncols),
        grid=(ncol_blocks, n // ROW_TILE),
        in_specs=[pl.BlockSpec((ROW_TILE, d), lambda c, i: (i, 0)),
                  pl.BlockSpec((1, 6, d), lambda c, i: (mod_row(i), 0, 0)),
                  pl.BlockSpec((1, d), lambda c, i: (0, 0)),
                  pl.BlockSpec((d, ncols), lambda c, i: (0, c)),
                  pl.BlockSpec((d, LANE), lambda c, i: (0, 0))],
        out_specs=[pl.BlockSpec((ROW_TILE, ncols), lambda c, i: (i, c)),
                   pl.BlockSpec((1, ROW_TILE, LANE), lambda c, i: (c, i, 0))],
        out_shape=[jax.ShapeDtypeStruct((n, OFF_G), BF16),
                   jax.ShapeDtypeStruct((ncol_blocks, n, LANE), F32)],
        compiler_params=_params(("arbitrary", "arbitrary")),
    )(x, mod, norm_g, w_main, w_gate)


def _sgu_kernel(u_ref, v_ref, g_ref, ws_ref, bs_ref, y_ref):
    u = _gelu(u_ref[...].astype(F32))
    v = _gelu(v_ref[...].astype(F32))
    ms = jnp.mean(v * v, axis=-1, keepdims=True)
    v = (v * lax.rsqrt(ms + EPS) * g_ref[...]).astype(BF16)
    for c in range(ROW_TILE // CHUNK):
        rows = slice(c * CHUNK, (c + 1) * CHUNK)
        for h in range(SGU_HEADS):
            cols = slice(h * SGU_HD, (h + 1) * SGU_HD)
            mixed = jnp.dot(ws_ref[h], v[rows, cols], preferred_element_type=F32) + bs_ref[:, h:h + 1]
            y_ref[rows, cols] = (u[rows, cols] * mixed).astype(BF16)


def sgu_mixer(p, sgu_norm, sgu_w, sgu_bt):
    n = p.shape[0]
    return pl.pallas_call(
        _sgu_kernel,
        grid=(n // ROW_TILE,),
        in_specs=[pl.BlockSpec((ROW_TILE, D_SGU), lambda i: (i, OFF_U // D_SGU)),
                  pl.BlockSpec((ROW_TILE, D_SGU), lambda i: (i, OFF_V // D_SGU)),
                  pl.BlockSpec((1, D_SGU), lambda i: (0, 0)),
                  pl.BlockSpec((SGU_HEADS, CHUNK, CHUNK), lambda i: (0, 0, 0)),
                  pl.BlockSpec((CHUNK, SGU_HEADS), lambda i: (0, 0))],
        out_specs=pl.BlockSpec((ROW_TILE, D_SGU), lambda i: (i, 0)),
        out_shape=jax.ShapeDtypeStruct((n, D_SGU), BF16),
        compiler_params=_params(("parallel",)),
    )(p, p, sgu_norm, sgu_w, sgu_bt)


def _gate_prep_kernel(g_ref, bias_ref, o_ref):
    g = g_ref[...] + bias_ref[...]
    lf = jnp.minimum(g, 0.0) - jnp.log(1.0 + jnp.exp(-jnp.abs(g)))
    row = lax.broadcasted_iota(I32, (CHUNK, CHUNK), 0)
    col = lax.broadcasted_iota(I32, (CHUNK, CHUNK), 1)
    lower = (col <= row).astype(F32)
    upper = (col >= row).astype(F32)
    b_fwd = jnp.dot(lower, lf, preferred_element_type=F32, precision=lax.Precision.HIGHEST)
    b_bwd = jnp.dot(upper, lf, preferred_element_type=F32, precision=lax.Precision.HIGHEST)
    lane = lax.broadcasted_iota(I32, (1, LANE), 1)
    is_input_gate = (lane // MLSTM_HEADS) % 2 == 0
    is_fwd = lane < 2 * MLSTM_HEADS
    o_ref[...] = jnp.where(is_input_gate, g, jnp.where(is_fwd, b_fwd, b_bwd))


def gate_prep(graw, bias):
    n = graw.shape[0]
    return pl.pallas_call(
        _gate_prep_kernel,
        grid=(n // CHUNK,),
        in_specs=[pl.BlockSpec((CHUNK, LANE), lambda i: (i, 0)),
                  pl.BlockSpec((1, LANE), lambda i: (0, 0))],
        out_specs=pl.BlockSpec((CHUNK, LANE), lambda i: (i, 0)),
        out_shape=jax.ShapeDtypeStruct((n, LANE), F32),
        compiler_params=_params(("parallel",)),
    )(graw, bias)


HALO = 16


def _qk_conv_kernel(xm_ref, xp_ref, xn_ref, w_ref, o_ref, *, blocks_per_batch, ctx_blocks):
    i = pl.program_id(0)
    j = pl.program_id(1)
    x = xm_ref[...].astype(F32)
    tm = x.shape[0]
    prev_row = xp_ref[...].astype(F32)[HALO - 1:HALO, :]
    next_row = xn_ref[...].astype(F32)[0:1, :]
    ib = i % blocks_per_batch
    at_start = jnp.logical_or(ib == 0, ib == ctx_blocks)
    at_end = jnp.logical_or(ib == ctx_blocks - 1, ib == blocks_per_batch - 1)
    prev_row = jnp.where(at_start, 0.0, prev_row)
    next_row = jnp.where(at_end, 0.0, next_row)
    rows = lax.broadcasted_iota(I32, (tm, 1), 0)
    x_prev = jnp.where(rows == 0, prev_row, pltpu.roll(x, 1, 0))
    x_next = jnp.where(rows == tm - 1, next_row, pltpu.roll(x, tm - 1, 0))
    y = w_ref[0:1, :] * x_prev + w_ref[1:2, :] * x + w_ref[2:3, :] * x_next
    y = y * _sigmoid(y)
    y = y * jnp.where(j >= 2, MLSTM_HD ** -0.5, 1.0)
    o_ref[...] = y.astype(BF16)


def qk_conv(p, conv_w, blocks_per_batch, ctx_blocks):
    n = p.shape[0]
    tc = 512
    nhalo = n // HALO
    per = ROW_TILE // HALO

    def col(j):
        return jnp.where(j < 2, OFF_Q // tc + j, OFF_K // tc + j - 2)

    return pl.pallas_call(
        functools.partial(_qk_conv_kernel, blocks_per_batch=blocks_per_batch, ctx_blocks=ctx_blocks),
        grid=(n // ROW_TILE, 2 * D_MLSTM // tc),
        in_specs=[pl.BlockSpec((ROW_TILE, tc), lambda i, j: (i, col(j))),
                  pl.BlockSpec((HALO, tc), lambda i, j: (jnp.maximum(i * per - 1, 0), col(j))),
                  pl.BlockSpec((HALO, tc), lambda i, j: (jnp.minimum((i + 1) * per, nhalo - 1), col(j))),
                  pl.BlockSpec((3, tc), lambda i, j: (0, j))],
        out_specs=pl.BlockSpec((ROW_TILE, tc), lambda i, j: (i, j)),
        out_shape=jax.ShapeDtypeStruct((n, 2 * D_MLSTM), BF16),
        compiler_params=_params(("parallel", "parallel")),
    )(p, p, p, conv_w)


def _mlstm_kernel(q_ref, k_ref, v_ref, o_ref, gc_ref, gr_ref, nrm_ref, y_ref,
                  hf_ref, hb_ref, c_ref, n_ref, m_ref, *, nchunks, nctx):
    c_ref[...] = jnp.zeros_like(c_ref)
    n_ref[...] = jnp.zeros_like(n_ref)
    m_ref[...] = jnp.zeros_like(m_ref)
    row = lax.broadcasted_iota(I32, (CHUNK, CHUNK), 0)
    col = lax.broadcasted_iota(I32, (CHUNK, CHUNK), 1)
    seen = (row >= col, row <= col)
    h_refs = (hf_ref, hb_ref)

    def step(d, c):
        r0 = pl.multiple_of(c * CHUNK, CHUNK)
        rows = pl.ds(r0, CHUNK)
        q = q_ref[rows, :]
        k = k_ref[rows, :]
        v = v_ref[rows, :]
        gcol = gc_ref[0, rows, :]
        grow = gr_ref[0, c]
        ig_c, b_c = gcol[:, 2 * d:2 * d + 1], gcol[:, 2 * d + 1:2 * d + 2]
        ig_r, b_r = grow[2 * d:2 * d + 1, :], grow[2 * d + 1:2 * d + 2, :]
        b_last = b_c[CHUNK - 1:CHUNK, :] if d == 0 else b_c[0:1, :]
        m = m_ref[d, 0:1, 0:1]
        g_c = b_last - b_c + ig_c
        m_new = jnp.maximum(b_last + m, jnp.max(g_c, axis=0, keepdims=True))
        wk_c = jnp.exp(g_c - m_new)
        decay = jnp.exp(b_last + m - m_new)

        log_w = jnp.where(seen[d], b_c - b_r + ig_r, -jnp.inf)
        inter = b_c + m
        m_t = jnp.maximum(inter, jnp.max(log_w, axis=1, keepdims=True))
        w_prev = jnp.exp(inter - m_t)
        s = lax.dot_general(q, k, (((1,), (1,)), ((), ())), preferred_element_type=F32)
        s = s * jnp.exp(log_w - m_t)
        ct = c_ref[d]
        nvec = n_ref[d]
        num = (jnp.dot(s.astype(BF16), v, preferred_element_type=F32)
               + w_prev * jnp.dot(q, ct.astype(BF16), preferred_element_type=F32))
        qf = q.astype(F32)
        kf = k.astype(F32)
        den = jnp.sum(s, axis=1, keepdims=True) + w_prev * jnp.sum(qf * nvec, axis=1, keepdims=True)
        h_refs[d][rows, :] = num / jnp.maximum(jnp.abs(den), jnp.exp(-m_t))

        wv = (v.astype(F32) * wk_c).astype(BF16)
        c_ref[d] = decay * ct + lax.dot_general(k, wv, (((0,), (0,)), ((), ())),
                                                preferred_element_type=F32)
        n_ref[d] = decay * nvec + jnp.sum(kf * wk_c, axis=0, keepdims=True)
        m_ref[d] = jnp.broadcast_to(m_new, (1, LANE))

    def body(i, carry):
        step(0, i)
        step(1, jnp.where(i < nctx, nctx - 1 - i, nchunks - 1 - (i - nctx)))
        return carry

    lax.fori_loop(0, nchunks, body, 0)

    def finish(c, carry):
        rows = pl.ds(pl.multiple_of(c * CHUNK, CHUNK), CHUNK)
        h = hf_ref[rows, :] + hb_ref[rows, :]
        hn = h * lax.rsqrt(jnp.mean(h * h, axis=-1, keepdims=True) + EPS) * nrm_ref[...]
        y_ref[rows, :] = (_sigmoid(o_ref[rows, :].astype(F32)) * hn).astype(BF16)
        return carry

    lax.fori_loop(0, nchunks, finish, 0)


def mlstm_mixer(p, qk, gcol, grow, mlstm_norm, batch, tb, nctx):
    n = p.shape[0]
    nchunks = tb // CHUNK
    hd = MLSTM_HD
    return pl.pallas_call(
        functools.partial(_mlstm_kernel, nchunks=nchunks, nctx=nctx),
        grid=(batch, MLSTM_HEADS),
        in_specs=[pl.BlockSpec((tb, hd), lambda b, h: (b, h)),
                  pl.BlockSpec((tb, hd), lambda b, h: (b, MLSTM_HEADS + h)),
                  pl.BlockSpec((tb, hd), lambda b, h: (b, OFF_VM // hd + h)),
                  pl.BlockSpec((tb, hd), lambda b, h: (b, OFF_O // hd + h)),
                  pl.BlockSpec((1, tb, 4), lambda b, h: (h, b, 0)),
                  pl.BlockSpec((1, nchunks, 4, CHUNK), lambda b, h: (h, b, 0, 0)),
                  pl.BlockSpec((1, hd), lambda b, h: (0, h))],
        out_specs=pl.BlockSpec((tb, hd), lambda b, h: (b, h)),
        out_shape=jax.ShapeDtypeStruct((n, D_MLSTM), BF16),
        scratch_shapes=[pltpu.VMEM((tb, hd), F32), pltpu.VMEM((tb, hd), F32),
                        pltpu.VMEM((2, hd, hd), F32), pltpu.VMEM((2, 1, hd), F32),
                        pltpu.VMEM((2, 1, LANE), F32)],
        compiler_params=_params(("parallel", "parallel")),
    )(qk, qk, p, p, gcol, grow, mlstm_norm)


def _pool_kernel(xfull_ref, xrow_ref, w_ref, sc_ref, y_ref, m_ref, inv_ref, *, tb, ctx, seq):
    g = pl.program_id(0)
    i = pl.program_id(1)
    b = pl.program_id(2)
    tm = xrow_ref.shape[0]
    half = jnp.left_shift(1, g)
    win = 2 * half

    def grid_coords(pos):
        is_ctx = pos < ctx
        lat = pos - ctx
        r = jnp.where(is_ctx, 0, jnp.right_shift(lat, GRID_W_LOG2))
        c = jnp.where(is_ctx, pos, jnp.bitwise_and(lat, GRID_W - 1))
        return is_ctx, r, c

    band = m_ref.shape[1]
    first_tile = jnp.clip(i - POOL_HALO_TILES, 0, (tb - band) // tm)
    s0 = pl.multiple_of(first_tile * tm, tm)

    @pl.when(b == 0)
    def _build_window_matrix():
        t = i * tm + lax.broadcasted_iota(I32, (tm, 1), 0)
        s = s0 + lax.broadcasted_iota(I32, (1, band), 1)
        ctx_t, r_t, c_t = grid_coords(t)
        ctx_s, r_s, c_s = grid_coords(s)
        dr = r_s - r_t + half
        dc = c_s - c_t + half
        inside = (ctx_t == ctx_s) & (dr >= 0) & (dr < win) & (dc >= 0) & (dc < win)
        m_ref[...] = jnp.where(inside, 1.0, 0.0).astype(BF16)
        nrows = jnp.where(ctx_t, 1, seq // GRID_W)
        ncols = jnp.where(ctx_t, ctx, GRID_W)
        cnt_r = jnp.minimum(r_t - half + win, nrows) - jnp.maximum(r_t - half, 0)
        cnt_c = jnp.minimum(c_t - half + win, ncols) - jnp.maximum(c_t - half, 0)
        inv_ref[...] = 1.0 / (cnt_r * cnt_c).astype(F32)

    tot = jnp.dot(m_ref[...], xfull_ref[pl.ds(s0, band), :], preferred_element_type=F32)
    d = (tot * inv_ref[...] - xrow_ref[...].astype(F32)).astype(BF16)
    y = jnp.dot(d, w_ref[0], preferred_element_type=F32) * sc_ref[0]
    y_ref[...] = y.astype(BF16)


def pool_mixer(p, pool_w, pool_scale, batch, tb, ctx, seq):
    n = p.shape[0]
    nrb = tb // ROW_TILE
    c0 = OFF_P // POOL_GC
    band = min(2 * POOL_HALO_TILES + 1, nrb) * ROW_TILE
    return pl.pallas_call(
        functools.partial(_pool_kernel, tb=tb, ctx=ctx, seq=seq),
        grid=(POOL_GROUPS, nrb, batch),
        in_specs=[pl.BlockSpec((tb, POOL_GC), lambda g, i, b: (b, c0 + g)),
                  pl.BlockSpec((ROW_TILE, POOL_GC), lambda g, i, b: (b * nrb + i, c0 + g)),
                  pl.BlockSpec((1, POOL_GC, POOL_GC), lambda g, i, b: (g, 0, 0)),
                  pl.BlockSpec((1, 1, POOL_GC), lambda g, i, b: (g, 0, 0))],
        out_specs=pl.BlockSpec((ROW_TILE, POOL_GC), lambda g, i, b: (b * nrb + i, g)),
        out_shape=jax.ShapeDtypeStruct((n, D_POOL), BF16),
        scratch_shapes=[pltpu.VMEM((ROW_TILE, band), BF16), pltpu.VMEM((ROW_TILE, 1), F32)],
        compiler_params=_params(("arbitrary", "arbitrary", "arbitrary")),
    )(p, p, pool_w, pool_scale)


def _out_proj_kernel(x_ref, ya_ref, yb_ref, yc_ref, mod_ref, w_ref, o_ref):
    ya, yb, yc = ya_ref[...], yb_ref[...], yc_ref[...]
    k1 = D_SGU
    k2 = D_SGU + D_MLSTM
    for j in range(D_MODEL // 256):
        sl = slice(j * 256, (j + 1) * 256)
        acc = jnp.dot(ya, w_ref[0:k1, sl], preferred_element_type=F32)
        acc += jnp.dot(yb, w_ref[k1:k2, sl], preferred_element_type=F32)
        acc += jnp.dot(yc, w_ref[k2:D_MODEL, sl], preferred_element_type=F32)
        o_ref[:, sl] = x_ref[:, sl] + mod_ref[0, 2:3, sl] * acc


def out_proj(x, ya, yb, yc, mod, w_out, mod_row):
    n, d = x.shape
    return pl.pallas_call(
        _out_proj_kernel,
        grid=(n // ROW_TILE,),
        in_specs=[pl.BlockSpec((ROW_TILE, d), lambda i: (i, 0)),
                  pl.BlockSpec((ROW_TILE, D_SGU), lambda i: (i, 0)),
                  pl.BlockSpec((ROW_TILE, D_MLSTM), lambda i: (i, 0)),
                  pl.BlockSpec((ROW_TILE, D_POOL), lambda i: (i, 0)),
                  pl.BlockSpec((1, 6, d), lambda i: (mod_row(i), 0, 0)),
                  pl.BlockSpec((d, d), lambda i: (0, 0))],
        out_specs=pl.BlockSpec((ROW_TILE, d), lambda i: (i, 0)),
        out_shape=jax.ShapeDtypeStruct((n, d), F32),
        compiler_params=_params(("parallel",)),
    )(x, ya, yb, yc, mod, w_out)


def _topk_rows(vals, k):
    nrows = vals.shape[0]
    iota = lax.broadcasted_iota(I32, vals.shape, 0)
    out_v, out_i = [], []
    for _ in range(k):
        mx = jnp.max(vals, axis=0, keepdims=True)
        am = jnp.min(jnp.where(vals == mx, iota, nrows), axis=0, keepdims=True)
        out_v.append(mx)
        out_i.append(am)
        vals = jnp.where(iota == am, -jnp.inf, vals)
    return out_v, out_i


def _peer_route_kernel(x_ref, mod_ref, g_ref, wq_ref, keys_ref, hf_ref, e_ref, gate_ref,
                       hb_ref, cand_ref, cid_ref):
    h = pl.program_id(1)

    @pl.when(h == 0)
    def _norm():
        hf = _norm_mod(x_ref[...], g_ref[...], mod_ref[0, 3:4, :], mod_ref[0, 4:5, :])
        hf_ref[...] = hf
        hb_ref[...] = hf.astype(BF16)

    q = jnp.dot(hb_ref[...], wq_ref[...], preferred_element_type=F32).astype(BF16)
    half = PEER_DK // 2
    nt = (((1,), (1,)), ((), ()))
    scores1 = lax.dot_general(keys_ref[0], q[:, :half], nt, preferred_element_type=F32)
    scores2 = lax.dot_general(keys_ref[1], q[:, half:], nt, preferred_element_type=F32)
    npad = CAND_ROWS - len(CAND_PAIRS)
    for c in range(q.shape[0] // LANE):
        lanes = slice(c * LANE, (c + 1) * LANE)
        s1, i1 = _topk_rows(scores1[:, lanes], PEER_TOPK)
        s2, i2 = _topk_rows(scores2[:, lanes], PEER_TOPK)
        for r, (a, b) in enumerate(CAND_PAIRS):
            cand_ref[r:r + 1, lanes] = s1[a] + s2[b]
            cid_ref[r:r + 1, lanes] = i1[a] * N_KEYS + i2[b]
        cand_ref[len(CAND_PAIRS):, lanes] = jnp.full((npad, LANE), -jnp.inf, F32)
        cid_ref[len(CAND_PAIRS):, lanes] = jnp.zeros((npad, LANE), I32)
        sc, slot = _topk_rows(cand_ref[:, lanes], PEER_TOPK)
        cid = cid_ref[:, lanes]
        riota = lax.broadcasted_iota(I32, cid.shape, 0)
        ex = [jnp.exp(v - sc[0]) for v in sc]
        tot = ex[0]
        for v in ex[1:]:
            tot = tot + v
        for j in range(PEER_TOPK):
            e_ref[0, j:j + 1, lanes] = jnp.sum(jnp.where(riota == slot[j], cid, 0), axis=0, keepdims=True)
            gate_ref[0, j:j + 1, lanes] = ex[j] / tot


def peer_route(x, mod, norm_g, wq, keys, mod_row, nblocks, blk):
    n, d = x.shape
    tt = ROW_TILE
    return pl.pallas_call(
        _peer_route_kernel,
        grid=(nblocks, PEER_HEADS),
        in_specs=[pl.BlockSpec((tt, d), lambda i, h: (blk(i), 0)),
                  pl.BlockSpec((1, 6, d), lambda i, h: (mod_row(blk(i)), 0, 0)),
                  pl.BlockSpec((1, d), lambda i, h: (0, 0)),
                  pl.BlockSpec((d, PEER_DK), lambda i, h: (0, h)),
                  pl.BlockSpec((2, N_KEYS, PEER_DK // 2), lambda i, h: (0, 0, 0))],
        out_specs=[pl.BlockSpec((tt, d), lambda i, h: (blk(i), 0)),
                   pl.BlockSpec((1, PEER_TOPK, tt), lambda i, h: (h, 0, blk(i))),
                   pl.BlockSpec((1, PEER_TOPK, tt), lambda i, h: (h, 0, blk(i)))],
        out_shape=[jax.ShapeDtypeStruct((n, d), F32),
                   jax.ShapeDtypeStruct((PEER_HEADS, PEER_TOPK, n), I32),
                   jax.ShapeDtypeStruct((PEER_HEADS, PEER_TOPK, n), F32)],
        scratch_shapes=[pltpu.VMEM((tt, d), BF16), pltpu.VMEM((CAND_ROWS, tt), F32),
                        pltpu.VMEM((CAND_ROWS, tt), I32)],
        compiler_params=_params(("parallel", "arbitrary")),
    )(x, mod, norm_g, wq, keys)


ISSUE_UNROLL = 8
SUBLANES = 8
FOLD_ORDER = (0, 4, 2, 6, 1, 5, 3, 7)


def _fold(a, b, shift, first):
    return (jnp.where(first, a, pltpu.roll(b, shift, 0))
            + jnp.where(first, pltpu.roll(a, SUBLANES - shift, 0), b))


def _transpose8(vs, sub):
    vs = list(vs)
    for d in (4, 2, 1):
        keep = jnp.bitwise_and(sub, d) == 0
        nxt = list(vs)
        for i in range(SUBLANES):
            if i & d == 0:
                a, b = vs[i], vs[i + d]
                nxt[i] = jnp.where(keep, a, pltpu.roll(b, d, 0))
                nxt[i + d] = jnp.where(keep, pltpu.roll(a, SUBLANES - d, 0), b)
        vs = nxt
    return vs


def _unpack_pair(words):
    lo = pltpu.bitcast(jnp.left_shift(words, 16), F32)
    hi = pltpu.bitcast(jnp.bitwise_and(words, jnp.uint32(0xFFFF0000)), F32)
    return lo, hi


def _peer_gather_kernel(ids_ref, id1_ref, idn_ref, hf_ref, gt_ref, xn_ref, mod_ref, tab_hbm, o_ref,
                        tbuf, wbuf, sem, *, ngroups, layer):
    i = pl.program_id(0)
    slot = lax.rem(i, GATHER_SLOTS)
    nslot = lax.rem(i + GATHER_SLOTS - 1, GATHER_SLOTS)
    nrows = GATHER_TOKENS * PEER_PAIRS

    def start_row(ids, r, s, priority):
        pltpu.make_async_copy(tab_hbm.at[layer, ids[0, 0, r]], tbuf.at[s, r], sem.at[s]).start(priority=priority)

    def wait_slot(s):
        pltpu.make_async_copy(tab_hbm.at[layer, pl.ds(0, nrows)], tbuf.at[s], sem.at[s]).wait()

    @pl.when(i == 0)
    def _first_groups():
        def body(g, carry):
            for j in range(ISSUE_UNROLL):
                start_row(ids_ref, g * ISSUE_UNROLL + j, 0, j % 2)
                start_row(id1_ref, g * ISSUE_UNROLL + j, 1, j % 2)
            return carry
        lax.fori_loop(0, nrows // ISSUE_UNROLL, body, 0)

    wait_slot(slot)

    sub = lax.broadcasted_iota(I32, (SUBLANES, LANE), 0)
    first4, first2, first1 = sub < 4, sub % 4 < 2, sub % 2 < 1
    ngrp = PEER_PAIRS // SUBLANES
    early = 6
    late = SUBLANES - early
    def chunks(ref, first):
        return [ref[:, (first + c) * LANE:(first + c + 1) * LANE] for c in range(SUBLANES)]
    xs_lo = _transpose8(chunks(hf_ref, 0), sub)
    xs_hi = _transpose8(chunks(hf_ref, SUBLANES), sub)
    out_lo, out_hi = [], []
    for t in range(GATHER_TOKENS):
        x_lo = xs_lo[t]
        x_hi = xs_hi[t]
        base = t * PEER_PAIRS
        for j in range(ngrp):
            for q in range(early):
                start_row(idn_ref, base + j * early + q, nslot, q % 2)
            r = []
            for p in FOLD_ORDER:
                lo, hi = _unpack_pair(tbuf[slot, base + j * SUBLANES + p, 0])
                r.append(lo * x_lo + hi * x_hi)
            l1 = [_fold(r[0], r[1], 4, first4), _fold(r[2], r[3], 4, first4),
                  _fold(r[4], r[5], 4, first4), _fold(r[6], r[7], 4, first4)]
            l2 = [_fold(l1[0], l1[1], 2, first2), _fold(l1[2], l1[3], 2, first2)]
            y = _fold(l2[0], l2[1], 1, first1)
            rows = slice(j * SUBLANES, (j + 1) * SUBLANES)
            a = jnp.sum(y, axis=1, keepdims=True)
            w = gt_ref[0, rows, t:t + 1] * _gelu(a)
            wbuf[rows, :] = jnp.broadcast_to(w, (SUBLANES, LANE))
        nacc = 2
        acc_lo = [jnp.zeros((SUBLANES, LANE), F32) for _ in range(nacc)]
        acc_hi = [jnp.zeros((SUBLANES, LANE), F32) for _ in range(nacc)]
        late0 = ngrp * early
        for k in range(PEER_PAIRS):
            if k % SUBLANES == 0:
                for q in range(late0 + (k // SUBLANES) * late, late0 + (k // SUBLANES + 1) * late):
                    start_row(idn_ref, base + q, nslot, q % 2)
            lo, hi = _unpack_pair(tbuf[slot, base + k, 1])
            wk = wbuf[k:k + 1, :]
            acc_lo[k % nacc] = acc_lo[k % nacc] + lo * wk
            acc_hi[k % nacc] = acc_hi[k % nacc] + hi * wk
        out_lo.append(acc_lo[0] + acc_lo[1])
        out_hi.append(acc_hi[0] + acc_hi[1])

    for first, outs in ((0, out_lo), (SUBLANES, out_hi)):
        for c, rows_c in enumerate(_transpose8(outs, sub)):
            cols = slice((first + c) * LANE, (first + c + 1) * LANE)
            o_ref[:, cols] = xn_ref[:, cols] + mod_ref[0, 5:6, cols] * rows_c

    @pl.when(i == ngroups - 1)
    def _drain():
        wait_slot(lax.rem(i + 1, GATHER_SLOTS))
        wait_slot(nslot)


PACK_ROWS = 256


def _pack_kernel(u_ref, v_ref, o_ref):
    sub = lax.broadcasted_iota(I32, (SUBLANES, LANE), 0)
    half = D_MODEL // 2
    for g in range(PACK_ROWS // SUBLANES):
        rows = slice(g * SUBLANES, (g + 1) * SUBLANES)
        for which, ref in enumerate((u_ref, v_ref)):
            bits = pltpu.bitcast(ref[0, rows, :], jnp.uint32)
            rnd = bits + jnp.uint32(0x7FFF) + jnp.bitwise_and(jnp.right_shift(bits, 16), jnp.uint32(1))
            words = jnp.bitwise_or(jnp.right_shift(rnd[:, :half], 16),
                                   jnp.bitwise_and(rnd[:, half:], jnp.uint32(0xFFFF0000)))
            tiles = _transpose8([words[:, c * LANE:(c + 1) * LANE] for c in range(SUBLANES)], sub)
            for t in range(SUBLANES):
                o_ref[0, g * SUBLANES + t, which] = tiles[t]


def pack_expert_tables(u_tabs, v_tabs):
    depth, nexp, d = u_tabs.shape
    in_spec = pl.BlockSpec((1, PACK_ROWS, d), lambda l, i: (l, i, 0))
    return pl.pallas_call(
        _pack_kernel,
        grid=(depth, nexp // PACK_ROWS),
        in_specs=[in_spec, in_spec],
        out_specs=pl.BlockSpec((1, PACK_ROWS, 2, SUBLANES, LANE), lambda l, i: (l, i, 0, 0, 0)),
        out_shape=jax.ShapeDtypeStruct((depth, nexp, 2, SUBLANES, LANE), jnp.uint32),
        compiler_params=_params(("parallel", "parallel")),
    )(u_tabs, v_tabs)


def peer_gather(ids, hf, gates_t, xn, mod, tables, layer, mod_row_group, ngroups, grp):
    n, d = xn.shape
    gt = GATHER_TOKENS
    assert ngroups >= GATHER_SLOTS - 1 and gt == SUBLANES
    nrows = gt * PEER_PAIRS
    tok_spec = pl.BlockSpec((gt, d), lambda i: (grp(i), 0))

    def ids_spec(ahead):
        return pl.BlockSpec((1, 1, nrows), lambda i: (grp(jnp.minimum(i + ahead, ngroups - 1)), 0, 0),
                            memory_space=pltpu.SMEM)

    return pl.pallas_call(
        functools.partial(_peer_gather_kernel, ngroups=ngroups, layer=layer),
        grid=(ngroups,),
        in_specs=[ids_spec(0), ids_spec(1), ids_spec(GATHER_SLOTS - 1),
                  tok_spec,
                  pl.BlockSpec((1, PEER_PAIRS, gt), lambda i: (grp(i), 0, 0)),
                  tok_spec,
                  pl.BlockSpec((1, 6, d), lambda i: (mod_row_group(grp(i)), 0, 0)),
                  pl.BlockSpec(memory_space=pl.ANY)],
        out_specs=tok_spec,
        out_shape=jax.ShapeDtypeStruct((n, d), F32),
        scratch_shapes=[pltpu.VMEM((GATHER_SLOTS, nrows, 2, SUBLANES, LANE), jnp.uint32),
                        pltpu.VMEM((PEER_PAIRS, LANE), F32),
                        pltpu.SemaphoreType.DMA((GATHER_SLOTS,))],
        compiler_params=_params(("arbitrary",)),
    )(ids, ids, ids, hf, gates_t, xn, mod, tables)


def _final_norm_kernel(x_ref, g_ref, o_ref):
    x = x_ref[...]
    o_ref[0] = x * lax.rsqrt(jnp.mean(x * x, axis=-1, keepdims=True) + EPS) * g_ref[...]


def final_norm(x, g, batch, seq, ctx):
    n, d = x.shape
    tb = ctx + seq
    nrb = tb // ROW_TILE
    cb = ctx // ROW_TILE
    return pl.pallas_call(
        _final_norm_kernel,
        grid=(batch, seq // ROW_TILE),
        in_specs=[pl.BlockSpec((ROW_TILE, d), lambda b, i: (b * nrb + cb + i, 0)),
                  pl.BlockSpec((1, d), lambda b, i: (0, 0))],
        out_specs=pl.BlockSpec((1, ROW_TILE, d), lambda b, i: (b, i, 0)),
        out_shape=jax.ShapeDtypeStruct((batch, seq, d), F32),
        compiler_params=_params(("parallel", "parallel")),
    )(x, g)


def kernel(x, c, ctx, c_ctx, ada_w, ada_b, norm_mix, norm_ffn, w_in, b_gate, sgu_norm, sgu_w, sgu_b,
           qk_conv_w, mlstm_norm, pool_w, pool_scale, w_out, peer_wq, peer_keys, peer_u, peer_v,
           norm_final):
    batch, seq, d = x.shape
    ctx_len = ctx.shape[1]
    depth = ada_w.shape[0]
    tb = ctx_len + seq
    n = batch * tb
    assert d == D_MODEL and w_in.shape[2] == OFF_G + N_GATES
    assert ctx_len % ROW_TILE == 0 and seq % ROW_TILE == 0 and seq % GRID_W == 0
    assert n % GATHER_TOKENS == 0 and ROW_TILE % GATHER_TOKENS == 0
    blocks_per_batch = tb // ROW_TILE
    ctx_blocks = ctx_len // ROW_TILE
    nchunks = tb // CHUNK

    def mod_row(i):
        return jnp.where(i % blocks_per_batch < ctx_blocks, batch, i // blocks_per_batch)

    groups_per_block = ROW_TILE // GATHER_TOKENS
    lat_blocks = seq // ROW_TILE

    def mod_row_group(i):
        return mod_row(i // groups_per_block)

    def lat_blk(j):
        return (j // lat_blocks) * blocks_per_batch + ctx_blocks + j % lat_blocks

    cond_rows = -(-(batch + 1) // 8) * 8
    cond = jnp.zeros((cond_rows, d), F32).at[:batch].set(c).at[batch].set(c_ctx)
    mods = ada_modulation_all(cond, ada_w, ada_b).reshape(depth, cond_rows, 6, d)

    tables = pack_expert_tables(peer_u, peer_v)
    xs = jnp.concatenate([ctx, x], axis=1).reshape(n, d)
    lane_pad = LANE - N_GATES
    for l in range(depth):
        mod = mods[l]
        w_main = w_in[l, :, :OFF_G].astype(BF16)
        w_gate = jnp.pad(w_in[l, :, OFF_G:], ((0, 0), (0, lane_pad))).astype(BF16)
        p, graw = in_proj(xs, mod, norm_mix[l][None], w_main, w_gate, mod_row)

        y_a = sgu_mixer(p, sgu_norm[l][None], sgu_w[l].astype(BF16), sgu_b[l].T)

        gp = gate_prep(graw[0], jnp.pad(b_gate[l], (0, lane_pad))[None])
        gp = gp[:, :N_GATES].reshape(n, 2, 2, MLSTM_HEADS)
        gcol = gp.transpose(3, 0, 1, 2).reshape(MLSTM_HEADS, n, 4)
        grow = gcol.reshape(MLSTM_HEADS, n // CHUNK, CHUNK, 4).transpose(0, 1, 3, 2)
        qk = qk_conv(p, qk_conv_w[l], blocks_per_batch, ctx_blocks)
        y_b = mlstm_mixer(p, qk, gcol, grow, mlstm_norm[l][None], batch, tb, ctx_len // CHUNK)

        y_c = pool_mixer(p, pool_w[l].astype(BF16), pool_scale[l].reshape(POOL_GROUPS, 1, POOL_GC),
                         batch, tb, ctx_len, seq)

        xn = out_proj(xs, y_a, y_b, y_c, mod, w_out[l].astype(BF16), mod_row)

        if l == depth - 1:
            nblocks, blk = batch * lat_blocks, lat_blk
        else:
            nblocks, blk = n // ROW_TILE, lambda j: j
        hf, e, gates = peer_route(xn, mod, norm_ffn[l][None], peer_wq[l].astype(BF16),
                                  peer_keys[l].astype(BF16), mod_row, nblocks, blk)
        ids = e.reshape(PEER_PAIRS, n).T.reshape(n // GATHER_TOKENS, 1, GATHER_TOKENS * PEER_PAIRS)
        gates_t = gates.reshape(PEER_PAIRS, n // GATHER_TOKENS, GATHER_TOKENS).transpose(1, 0, 2)
        xs = peer_gather(ids, hf, gates_t, xn, mod, tables, l, mod_row_group,
                         nblocks * groups_per_block,
                         lambda g, blk=blk: blk(g // groups_per_block) * groups_per_block + g % groups_per_block)

    return final_norm(xs, norm_final[None], batch, seq, ctx_len)
```

```python
import functools

import jax
import jax.numpy as jnp
from jax import lax
from jax.experimental import pallas as pl
from jax.experimental.pallas import tpu as pltpu

F32 = jnp.float32
BF16 = jnp.bfloat16
I32 = jnp.int32

EPS = 1e-6
GRID_W = 64
GRID_W_LOG2 = 6
D_MODEL = 2048
D_SGU = D_MODEL // 4
SGU_HEADS = 4
SGU_HD = D_SGU // SGU_HEADS
CHUNK = 128
D_MLSTM = D_MODEL // 2
MLSTM_HEADS = 4
MLSTM_HD = D_MLSTM // MLSTM_HEADS
D_POOL = D_MODEL // 4
POOL_GROUPS = 4
POOL_GC = D_POOL // POOL_GROUPS
N_GATES = 2 * 2 * MLSTM_HEADS
OFF_U = 0
OFF_V = OFF_U + D_SGU
OFF_P = OFF_V + D_SGU
OFF_Q = OFF_P + D_POOL
OFF_O = OFF_Q + D_MLSTM
OFF_K = OFF_O + D_MLSTM
OFF_VM = OFF_K + D_MLSTM
OFF_G = OFF_VM + D_MLSTM
N_KEYS = 128
PEER_HEADS = 8
PEER_TOPK = 16
PEER_DK = 256
PEER_PAIRS = PEER_HEADS * PEER_TOPK

LANE = 128
ROW_TILE = 256
POOL_HALO_TILES = -(-(GRID_W << (POOL_GROUPS - 1)) // ROW_TILE)
GATHER_TOKENS = 8
GATHER_SLOTS = 3
VMEM_LIMIT = 56 * 1024 * 1024

CAND_PAIRS = [(a, b) for a in range(PEER_TOPK) for b in range(PEER_TOPK)
              if (a + 1) * (b + 1) <= PEER_TOPK]
CAND_ROWS = 56


def _params(sem, vmem=VMEM_LIMIT):
    return pltpu.CompilerParams(dimension_semantics=sem, vmem_limit_bytes=vmem)


def _norm_mod(x, g, shift, scale):
    ms = jnp.mean(x * x, axis=-1, keepdims=True)
    y = x * lax.rsqrt(ms + EPS)
    return (y * g) * (1.0 + scale) + shift


def _gelu(x):
    return jax.nn.gelu(x, approximate=True)


def _sigmoid(x):
    return 1.0 / (1.0 + jnp.exp(-x))


def _ada_kernel(cond_ref, w_ref, b_ref, o_ref):
    a = cond_ref[...]
    a = (a * _sigmoid(a)).astype(BF16)
    o_ref[0] = jnp.dot(a, w_ref[0].astype(BF16), preferred_element_type=F32) + b_ref[0]


def ada_modulation_all(cond, ada_w, ada_b):
    depth, d, d6 = ada_w.shape
    rows = cond.shape[0]
    tn = 1024
    return pl.pallas_call(
        _ada_kernel,
        grid=(depth, d6 // tn),
        in_specs=[pl.BlockSpec((rows, d), lambda l, j: (0, 0)),
                  pl.BlockSpec((1, d, tn), lambda l, j: (l, 0, j)),
                  pl.BlockSpec((1, 1, tn), lambda l, j: (l, 0, j))],
        out_specs=pl.BlockSpec((1, rows, tn), lambda l, j: (l, 0, j)),
        out_shape=jax.ShapeDtypeStruct((depth, rows, d6), F32),
        compiler_params=_params(("parallel", "parallel")),
    )(cond, ada_w, ada_b.reshape(depth, 1, d6))


def _in_proj_kernel(x_ref, mod_ref, g_ref, w_ref, wg_ref, p_ref, gate_ref, *, ncols):
    h = _norm_mod(x_ref[...], g_ref[...], mod_ref[0, 0:1, :], mod_ref[0, 1:2, :]).astype(BF16)
    for j in range(ncols // 256):
        sl = slice(j * 256, (j + 1) * 256)
        p_ref[:, sl] = jnp.dot(h, w_ref[:, sl], preferred_element_type=F32).astype(BF16)
    gate_ref[0] = jnp.dot(h, wg_ref[...], preferred_element_type=F32)


def in_proj(x, mod, norm_g, w_main, w_gate, mod_row):
    n, d = x.shape
    ncol_blocks = 2
    ncols = OFF_G // ncol_blocks
    return pl.pallas_call(
        functools.partial(_in_proj_kernel, ncols=ncols),
        grid=(ncol_blocks, n // ROW_TILE),
        in_specs=[pl.BlockSpec((ROW_TILE, d), lambda c, i: (i, 0)),
                  pl.BlockSpec((1, 6, d), lambda c, i: (mod_row(i), 0, 0)),
                  pl.BlockSpec((1, d), lambda c, i: (0, 0)),
                  pl.BlockSpec((d, ncols), lambda c, i: (0, c)),
                  pl.BlockSpec((d, LANE), lambda c, i: (0, 0))],
        out_specs=[pl.BlockSpec((ROW_TILE, ncols), lambda c, i: (i, c)),
                   pl.BlockSpec((1, ROW_TILE, LANE), lambda c, i: (c, i, 0))],
        out_shape=[jax.ShapeDtypeStruct((n, OFF_G), BF16),
                   jax.ShapeDtypeStruct((ncol_blocks, n, LANE), F32)],
        compiler_params=_params(("arbitrary", "arbitrary")),
    )(x, mod, norm_g, w_main, w_gate)


def _sgu_kernel(u_ref, v_ref, g_ref, ws_ref, bs_ref, y_ref):
    u = _gelu(u_ref[...].astype(F32))
    v = _gelu(v_ref[...].astype(F32))
    ms = jnp.mean(v * v, axis=-1, keepdims=True)
    v = (v * lax.rsqrt(ms + EPS) * g_ref[...]).astype(BF16)
    for c in range(ROW_TILE // CHUNK):
        rows = slice(c * CHUNK, (c + 1) * CHUNK)
        for h in range(SGU_HEADS):
            cols = slice(h * SGU_HD, (h + 1) * SGU_HD)
            mixed = jnp.dot(ws_ref[h], v[rows, cols], preferred_element_type=F32) + bs_ref[:, h:h + 1]
            y_ref[rows, cols] = (u[rows, cols] * mixed).astype(BF16)


def sgu_mixer(p, sgu_norm, sgu_w, sgu_bt):
    n = p.shape[0]
    return pl.pallas_call(
        _sgu_kernel,
        grid=(n // ROW_TILE,),
        in_specs=[pl.BlockSpec((ROW_TILE, D_SGU), lambda i: (i, OFF_U // D_SGU)),
                  pl.BlockSpec((ROW_TILE, D_SGU), lambda i: (i, OFF_V // D_SGU)),
                  pl.BlockSpec((1, D_SGU), lambda i: (0, 0)),
                  pl.BlockSpec((SGU_HEADS, CHUNK, CHUNK), lambda i: (0, 0, 0)),
                  pl.BlockSpec((CHUNK, SGU_HEADS), lambda i: (0, 0))],
        out_specs=pl.BlockSpec((ROW_TILE, D_SGU), lambda i: (i, 0)),
        out_shape=jax.ShapeDtypeStruct((n, D_SGU), BF16),
        compiler_params=_params(("parallel",)),
    )(p, p, sgu_norm, sgu_w, sgu_bt)


def _gate_prep_kernel(g_ref, bias_ref, o_ref):
    g = g_ref[...] + bias_ref[...]
    lf = jnp.minimum(g, 0.0) - jnp.log(1.0 + jnp.exp(-jnp.abs(g)))
    row = lax.broadcasted_iota(I32, (CHUNK, CHUNK), 0)
    col = lax.broadcasted_iota(I32, (CHUNK, CHUNK), 1)
    lower = (col <= row).astype(F32)
    upper = (col >= row).astype(F32)
    b_fwd = jnp.dot(lower, lf, preferred_element_type=F32, precision=lax.Precision.HIGHEST)
    b_bwd = jnp.dot(upper, lf, preferred_element_type=F32, precision=lax.Precision.HIGHEST)
    lane = lax.broadcasted_iota(I32, (1, LANE), 1)
    is_input_gate = (lane // MLSTM_HEADS) % 2 == 0
    is_fwd = lane < 2 * MLSTM_HEADS
    o_ref[...] = jnp.where(is_input_gate, g, jnp.where(is_fwd, b_fwd, b_bwd))


def gate_prep(graw, bias):
    n = graw.shape[0]
    return pl.pallas_call(
        _gate_prep_kernel,
        grid=(n // CHUNK,),
        in_specs=[pl.BlockSpec((CHUNK, LANE), lambda i: (i, 0)),
                  pl.BlockSpec((1, LANE), lambda i: (0, 0))],
        out_specs=pl.BlockSpec((CHUNK, LANE), lambda i: (i, 0)),
        out_shape=jax.ShapeDtypeStruct((n, LANE), F32),
        compiler_params=_params(("parallel",)),
    )(graw, bias)


HALO = 16


def _qk_conv_kernel(xm_ref, xp_ref, xn_ref, w_ref, o_ref, *, blocks_per_batch, ctx_blocks):
    i = pl.program_id(0)
    j = pl.program_id(1)
    x = xm_ref[...].astype(F32)
    tm = x.shape[0]
    prev_row = xp_ref[...].astype(F32)[HALO - 1:HALO, :]
    next_row = xn_ref[...].astype(F32)[0:1, :]
    ib = i % blocks_per_batch
    at_start = jnp.logical_or(ib == 0, ib == ctx_blocks)
    at_end = jnp.logical_or(ib == ctx_blocks - 1, ib == blocks_per_batch - 1)
    prev_row = jnp.where(at_start, 0.0, prev_row)
    next_row = jnp.where(at_end, 0.0, next_row)
    rows = lax.broadcasted_iota(I32, (tm, 1), 0)
    x_prev = jnp.where(rows == 0, prev_row, pltpu.roll(x, 1, 0))
    x_next = jnp.where(rows == tm - 1, next_row, pltpu.roll(x, tm - 1, 0))
    y = w_ref[0:1, :] * x_prev + w_ref[1:2, :] * x + w_ref[2:3, :] * x_next
    y = y * _sigmoid(y)
    y = y * jnp.where(j >= 2, MLSTM_HD ** -0.5, 1.0)
    o_ref[...] = y.astype(BF16)


def qk_conv(p, conv_w, blocks_per_batch, ctx_blocks):
    n = p.shape[0]
    tc = 512
    nhalo = n // HALO
    per = ROW_TILE // HALO

    def col(j):
        return jnp.where(j < 2, OFF_Q // tc + j, OFF_K // tc + j - 2)

    return pl.pallas_call(
        functools.partial(_qk_conv_kernel, blocks_per_batch=blocks_per_batch, ctx_blocks=ctx_blocks),
        grid=(n // ROW_TILE, 2 * D_MLSTM // tc),
        in_specs=[pl.BlockSpec((ROW_TILE, tc), lambda i, j: (i, col(j))),
                  pl.BlockSpec((HALO, tc), lambda i, j: (jnp.maximum(i * per - 1, 0), col(j))),
                  pl.BlockSpec((HALO, tc), lambda i, j: (jnp.minimum((i + 1) * per, nhalo - 1), col(j))),
                  pl.BlockSpec((3, tc), lambda i, j: (0, j))],
        out_specs=pl.BlockSpec((ROW_TILE, tc), lambda i, j: (i, j)),
        out_shape=jax.ShapeDtypeStruct((n, 2 * D_MLSTM), BF16),
        compiler_params=_params(("parallel", "parallel")),
    )(p, p, p, conv_w)


def _mlstm_kernel(q_ref, k_ref, v_ref, o_ref, gc_ref, gr_ref, nrm_ref, y_ref,
                  hf_ref, hb_ref, c_ref, n_ref, m_ref, *, nchunks, nctx):
    c_ref[...] = jnp.zeros_like(c_ref)
    n_ref[...] = jnp.zeros_like(n_ref)
    m_ref[...] = jnp.zeros_like(m_ref)
    row = lax.broadcasted_iota(I32, (CHUNK, CHUNK), 0)
    col = lax.broadcasted_iota(I32, (CHUNK, CHUNK), 1)
    seen = (row >= col, row <= col)
    h_refs = (hf_ref, hb_ref)

    def step(d, c):
        r0 = pl.multiple_of(c * CHUNK, CHUNK)
        rows = pl.ds(r0, CHUNK)
        q = q_ref[rows, :]
        k = k_ref[rows, :]
        v = v_ref[rows, :]
        gcol = gc_ref[0, rows, :]
        grow = gr_ref[0, c]
        ig_c, b_c = gcol[:, 2 * d:2 * d + 1], gcol[:, 2 * d + 1:2 * d + 2]
        ig_r, b_r = grow[2 * d:2 * d + 1, :], grow[2 * d + 1:2 * d + 2, :]
        b_last = b_c[CHUNK - 1:CHUNK, :] if d == 0 else b_c[0:1, :]
        m = m_ref[d, 0:1, 0:1]
        g_c = b_last - b_c + ig_c
        m_new = jnp.maximum(b_last + m, jnp.max(g_c, axis=0, keepdims=True))
        wk_c = jnp.exp(g_c - m_new)
        decay = jnp.exp(b_last + m - m_new)

        log_w = jnp.where(seen[d], b_c - b_r + ig_r, -jnp.inf)
        inter = b_c + m
        m_t = jnp.maximum(inter, jnp.max(log_w, axis=1, keepdims=True))
        w_prev = jnp.exp(inter - m_t)
        s = lax.dot_general(q, k, (((1,), (1,)), ((), ())), preferred_element_type=F32)
        s = s * jnp.exp(log_w - m_t)
        ct = c_ref[d]
        nvec = n_ref[d]
        num = (jnp.dot(s.astype(BF16), v, preferred_element_type=F32)
               + w_prev * jnp.dot(q, ct.astype(BF16), preferred_element_type=F32))
        qf = q.astype(F32)
        kf = k.astype(F32)
        den = jnp.sum(s, axis=1, keepdims=True) + w_prev * jnp.sum(qf * nvec, axis=1, keepdims=True)
        h_refs[d][rows, :] = num / jnp.maximum(jnp.abs(den), jnp.exp(-m_t))

        wv = (v.astype(F32) * wk_c).astype(BF16)
        c_ref[d] = decay * ct + lax.dot_general(k, wv, (((0,), (0,)), ((), ())),
                                                preferred_element_type=F32)
        n_ref[d] = decay * nvec + jnp.sum(kf * wk_c, axis=0, keepdims=True)
        m_ref[d] = jnp.broadcast_to(m_new, (1, LANE))

    def body(i, carry):
        step(0, i)
        step(1, jnp.where(i < nctx, nctx - 1 - i, nchunks - 1 - (i - nctx)))
        return carry

    lax.fori_loop(0, nchunks, body, 0)

    def finish(c, carry):
        rows = pl.ds(pl.multiple_of(c * CHUNK, CHUNK), CHUNK)
        h = hf_ref[rows, :] + hb_ref[rows, :]
        hn = h * lax.rsqrt(jnp.mean(h * h, axis=-1, keepdims=True) + EPS) * nrm_ref[...]
        y_ref[rows, :] = (_sigmoid(o_ref[rows, :].astype(F32)) * hn).astype(BF16)
        return carry

    lax.fori_loop(0, nchunks, finish, 0)


def mlstm_mixer(p, qk, gcol, grow, mlstm_norm, batch, tb, nctx):
    n = p.shape[0]
    nchunks = tb // CHUNK
    hd = MLSTM_HD
    return pl.pallas_call(
        functools.partial(_mlstm_kernel, nchunks=nchunks, nctx=nctx),
        grid=(batch, MLSTM_HEADS),
        in_specs=[pl.BlockSpec((tb, hd), lambda b, h: (b, h)),
                  pl.BlockSpec((tb, hd), lambda b, h: (b, MLSTM_HEADS + h)),
                  pl.BlockSpec((tb, hd), lambda b, h: (b, OFF_VM // hd + h)),
                  pl.BlockSpec((tb, hd), lambda b, h: (b, OFF_O // hd + h)),
                  pl.BlockSpec((1, tb, 4), lambda b, h: (h, b, 0)),
                  pl.BlockSpec((1, nchunks, 4, CHUNK), lambda b, h: (h, b, 0, 0)),
                  pl.BlockSpec((1, hd), lambda b, h: (0, h))],
        out_specs=pl.BlockSpec((tb, hd), lambda b, h: (b, h)),
        out_shape=jax.ShapeDtypeStruct((n, D_MLSTM), BF16),
        scratch_shapes=[pltpu.VMEM((tb, hd), F32), pltpu.VMEM((tb, hd), F32),
                        pltpu.VMEM((2, hd, hd), F32), pltpu.VMEM((2, 1, hd), F32),
                        pltpu.VMEM((2, 1, LANE), F32)],
        compiler_params=_params(("parallel", "parallel")),
    )(qk, qk, p, p, gcol, grow, mlstm_norm)


def _pool_kernel(xfull_ref, xrow_ref, w_ref, sc_ref, y_ref, m_ref, inv_ref, *, tb, ctx, seq):
    g = pl.program_id(0)
    i = pl.program_id(1)
    b = pl.program_id(2)
    tm = xrow_ref.shape[0]
    half = jnp.left_shift(1, g)
    win = 2 * half

    def grid_coords(pos):
        is_ctx = pos < ctx
        lat = pos - ctx
        r = jnp.where(is_ctx, 0, jnp.right_shift(lat, GRID_W_LOG2))
        c = jnp.where(is_ctx, pos, jnp.bitwise_and(lat, GRID_W - 1))
        return is_ctx, r, c

    band = m_ref.shape[1]
    first_tile = jnp.clip(i - POOL_HALO_TILES, 0, (tb - band) // tm)
    s0 = pl.multiple_of(first_tile * tm, tm)

    @pl.when(b == 0)
    def _build_window_matrix():
        t = i * tm + lax.broadcasted_iota(I32, (tm, 1), 0)
        s = s0 + lax.broadcasted_iota(I32, (1, band), 1)
        ctx_t, r_t, c_t = grid_coords(t)
        ctx_s, r_s, c_s = grid_coords(s)
        dr = r_s - r_t + half
        dc = c_s - c_t + half
        inside = (ctx_t == ctx_s) & (dr >= 0) & (dr < win) & (dc >= 0) & (dc < win)
        m_ref[...] = jnp.where(inside, 1.0, 0.0).astype(BF16)
        nrows = jnp.where(ctx_t, 1, seq // GRID_W)
        ncols = jnp.where(ctx_t, ctx, GRID_W)
        cnt_r = jnp.minimum(r_t - half + win, nrows) - jnp.maximum(r_t - half, 0)
        cnt_c = jnp.minimum(c_t - half + win, ncols) - jnp.maximum(c_t - half, 0)
        inv_ref[...] = 1.0 / (cnt_r * cnt_c).astype(F32)

    tot = jnp.dot(m_ref[...], xfull_ref[pl.ds(s0, band), :], preferred_element_type=F32)
    d = (tot * inv_ref[...] - xrow_ref[...].astype(F32)).astype(BF16)
    y = jnp.dot(d, w_ref[0], preferred_element_type=F32) * sc_ref[0]
    y_ref[...] = y.astype(BF16)


def pool_mixer(p, pool_w, pool_scale, batch, tb, ctx, seq):
    n = p.shape[0]
    nrb = tb // ROW_TILE
    c0 = OFF_P // POOL_GC
    band = min(2 * POOL_HALO_TILES + 1, nrb) * ROW_TILE
    return pl.pallas_call(
        functools.partial(_pool_kernel, tb=tb, ctx=ctx, seq=seq),
        grid=(POOL_GROUPS, nrb, batch),
        in_specs=[pl.BlockSpec((tb, POOL_GC), lambda g, i, b: (b, c0 + g)),
                  pl.BlockSpec((ROW_TILE, POOL_GC), lambda g, i, b: (b * nrb + i, c0 + g)),
                  pl.BlockSpec((1, POOL_GC, POOL_GC), lambda g, i, b: (g, 0, 0)),
                  pl.BlockSpec((1, 1, POOL_GC), lambda g, i, b: (g, 0, 0))],
        out_specs=pl.BlockSpec((ROW_TILE, POOL_GC), lambda g, i, b: (b * nrb + i, g)),
        out_shape=jax.ShapeDtypeStruct((n, D_POOL), BF16),
        scratch_shapes=[pltpu.VMEM((ROW_TILE, band), BF16), pltpu.VMEM((ROW_TILE, 1), F32)],
        compiler_params=_params(("arbitrary", "arbitrary", "arbitrary")),
    )(p, p, pool_w, pool_scale)


def _out_proj_kernel(x_ref, ya_ref, yb_ref, yc_ref, mod_ref, w_ref, o_ref):
    ya, yb, yc = ya_ref[...], yb_ref[...], yc_ref[...]
    k1 = D_SGU
    k2 = D_SGU + D_MLSTM
    for j in range(D_MODEL // 256):
        sl = slice(j * 256, (j + 1) * 256)
        acc = jnp.dot(ya, w_ref[0:k1, sl], preferred_element_type=F32)
        acc += jnp.dot(yb, w_ref[k1:k2, sl], preferred_element_type=F32)
        acc += jnp.dot(yc, w_ref[k2:D_MODEL, sl], preferred_element_type=F32)
        o_ref[:, sl] = x_ref[:, sl] + mod_ref[0, 2:3, sl] * acc


def out_proj(x, ya, yb, yc, mod, w_out, mod_row):
    n, d = x.shape
    return pl.pallas_call(
        _out_proj_kernel,
        grid=(n // ROW_TILE,),
        in_specs=[pl.BlockSpec((ROW_TILE, d), lambda i: (i, 0)),
                  pl.BlockSpec((ROW_TILE, D_SGU), lambda i: (i, 0)),
                  pl.BlockSpec((ROW_TILE, D_MLSTM), lambda i: (i, 0)),
                  pl.BlockSpec((ROW_TILE, D_POOL), lambda i: (i, 0)),
                  pl.BlockSpec((1, 6, d), lambda i: (mod_row(i), 0, 0)),
                  pl.BlockSpec((d, d), lambda i: (0, 0))],
        out_specs=pl.BlockSpec((ROW_TILE, d), lambda i: (i, 0)),
        out_shape=jax.ShapeDtypeStruct((n, d), F32),
        compiler_params=_params(("parallel",)),
    )(x, ya, yb, yc, mod, w_out)


def _topk_rows(vals, k):
    nrows = vals.shape[0]
    iota = lax.broadcasted_iota(I32, vals.shape, 0)
    out_v, out_i = [], []
    for _ in range(k):
        mx = jnp.max(vals, axis=0, keepdims=True)
        am = jnp.min(jnp.where(vals == mx, iota, nrows), axis=0, keepdims=True)
        out_v.append(mx)
        out_i.append(am)
        vals = jnp.where(iota == am, -jnp.inf, vals)
    return out_v, out_i


def _peer_route_kernel(x_ref, mod_ref, g_ref, wq_ref, keys_ref, hf_ref, e_ref, gate_ref,
                       hb_ref, cand_ref, cid_ref):
    h = pl.program_id(1)

    @pl.when(h == 0)
    def _norm():
        hf = _norm_mod(x_ref[...], g_ref[...], mod_ref[0, 3:4, :], mod_ref[0, 4:5, :])
        hf_ref[...] = hf
        hb_ref[...] = hf.astype(BF16)

    q = jnp.dot(hb_ref[...], wq_ref[...], preferred_element_type=F32).astype(BF16)
    half = PEER_DK // 2
    nt = (((1,), (1,)), ((), ()))
    scores1 = lax.dot_general(keys_ref[0], q[:, :half], nt, preferred_element_type=F32)
    scores2 = lax.dot_general(keys_ref[1], q[:, half:], nt, preferred_element_type=F32)
    npad = CAND_ROWS - len(CAND_PAIRS)
    for c in range(q.shape[0] // LANE):
        lanes = slice(c * LANE, (c + 1) * LANE)
        s1, i1 = _topk_rows(scores1[:, lanes], PEER_TOPK)
        s2, i2 = _topk_rows(scores2[:, lanes], PEER_TOPK)
        for r, (a, b) in enumerate(CAND_PAIRS):
            cand_ref[r:r + 1, lanes] = s1[a] + s2[b]
            cid_ref[r:r + 1, lanes] = i1[a] * N_KEYS + i2[b]
        cand_ref[len(CAND_PAIRS):, lanes] = jnp.full((npad, LANE), -jnp.inf, F32)
        cid_ref[len(CAND_PAIRS):, lanes] = jnp.zeros((npad, LANE), I32)
        sc, slot = _topk_rows(cand_ref[:, lanes], PEER_TOPK)
        cid = cid_ref[:, lanes]
        riota = lax.broadcasted_iota(I32, cid.shape, 0)
        ex = [jnp.exp(v - sc[0]) for v in sc]
        tot = ex[0]
        for v in ex[1:]:
            tot = tot + v
        for j in range(PEER_TOPK):
            e_ref[0, j:j + 1, lanes] = jnp.sum(jnp.where(riota == slot[j], cid, 0), axis=0, keepdims=True)
            gate_ref[0, j:j + 1, lanes] = ex[j] / tot


def peer_route(x, mod, norm_g, wq, keys, mod_row, nblocks, blk):
    n, d = x.shape
    tt = ROW_TILE
    return pl.pallas_call(
        _peer_route_kernel,
        grid=(nblocks, PEER_HEADS),
        in_specs=[pl.BlockSpec((tt, d), lambda i, h: (blk(i), 0)),
                  pl.BlockSpec((1, 6, d), lambda i, h: (mod_row(blk(i)), 0, 0)),
                  pl.BlockSpec((1, d), lambda i, h: (0, 0)),
                  pl.BlockSpec((d, PEER_DK), lambda i, h: (0, h)),
                  pl.BlockSpec((2, N_KEYS, PEER_DK // 2), lambda i, h: (0, 0, 0))],
        out_specs=[pl.BlockSpec((tt, d), lambda i, h: (blk(i), 0)),
                   pl.BlockSpec((1, PEER_TOPK, tt), lambda i, h: (h, 0, blk(i))),
                   pl.BlockSpec((1, PEER_TOPK, tt), lambda i, h: (h, 0, blk(i)))],
        out_shape=[jax.ShapeDtypeStruct((n, d), F32),
                   jax.ShapeDtypeStruct((PEER_HEADS, PEER_TOPK, n), I32),
                   jax.ShapeDtypeStruct((PEER_HEADS, PEER_TOPK, n), F32)],
        scratch_shapes=[pltpu.VMEM((tt, d), BF16), pltpu.VMEM((CAND_ROWS, tt), F32),
                        pltpu.VMEM((CAND_ROWS, tt), I32)],
        compiler_params=_params(("parallel", "arbitrary")),
    )(x, mod, norm_g, wq, keys)


ISSUE_UNROLL = 8
SUBLANES = 8
FOLD_ORDER = (0, 4, 2, 6, 1, 5, 3, 7)


def _fold(a, b, shift, first):
    return (jnp.where(first, a, pltpu.roll(b, shift, 0))
            + jnp.where(first, pltpu.roll(a, SUBLANES - shift, 0), b))


def _transpose8(vs, sub):
    vs = list(vs)
    for d in (4, 2, 1):
        keep = jnp.bitwise_and(sub, d) == 0
        nxt = list(vs)
        for i in range(SUBLANES):
            if i & d == 0:
                a, b = vs[i], vs[i + d]
                nxt[i] = jnp.where(keep, a, pltpu.roll(b, d, 0))
                nxt[i + d] = jnp.where(keep, pltpu.roll(a, SUBLANES - d, 0), b)
        vs = nxt
    return vs


def _unpack_pair(words):
    lo = pltpu.bitcast(jnp.left_shift(words, 16), F32)
    hi = pltpu.bitcast(jnp.bitwise_and(words, jnp.uint32(0xFFFF0000)), F32)
    return lo, hi


def _peer_gather_kernel(ids_ref, id1_ref, idn_ref, hf_ref, gt_ref, xn_ref, mod_ref, tab_hbm, o_ref,
                        tbuf, wbuf, sem, *, ngroups):
    i = pl.program_id(0)
    slot = lax.rem(i, GATHER_SLOTS)
    nslot = lax.rem(i + GATHER_SLOTS - 1, GATHER_SLOTS)
    nrows = GATHER_TOKENS * PEER_PAIRS

    def start_row(ids, r, s, priority):
        pltpu.make_async_copy(tab_hbm.at[ids[0, 0, r]], tbuf.at[s, r], sem.at[s]).start(priority=priority)

    def wait_slot(s):
        pltpu.make_async_copy(tab_hbm.at[pl.ds(0, nrows)], tbuf.at[s], sem.at[s]).wait()

    @pl.when(i == 0)
    def _first_groups():
        def body(g, carry):
            for j in range(ISSUE_UNROLL):
                start_row(ids_ref, g * ISSUE_UNROLL + j, 0, j % 2)
                start_row(id1_ref, g * ISSUE_UNROLL + j, 1, j % 2)
            return carry
        lax.fori_loop(0, nrows // ISSUE_UNROLL, body, 0)

    wait_slot(slot)

    sub = lax.broadcasted_iota(I32, (SUBLANES, LANE), 0)
    first4, first2, first1 = sub < 4, sub % 4 < 2, sub % 2 < 1
    ngrp = PEER_PAIRS // SUBLANES
    early = 6
    late = SUBLANES - early
    def chunks(ref, first):
        return [ref[:, (first + c) * LANE:(first + c + 1) * LANE] for c in range(SUBLANES)]
    xs_lo = _transpose8(chunks(hf_ref, 0), sub)
    xs_hi = _transpose8(chunks(hf_ref, SUBLANES), sub)
    out_lo, out_hi = [], []
    for t in range(GATHER_TOKENS):
        x_lo = xs_lo[t]
        x_hi = xs_hi[t]
        base = t * PEER_PAIRS
        for j in range(ngrp):
            for q in range(early):
                start_row(idn_ref, base + j * early + q, nslot, q % 2)
            r = []
            for p in FOLD_ORDER:
                lo, hi = _unpack_pair(tbuf[slot, base + j * SUBLANES + p, 0])
                r.append(lo * x_lo + hi * x_hi)
            l1 = [_fold(r[0], r[1], 4, first4), _fold(r[2], r[3], 4, first4),
                  _fold(r[4], r[5], 4, first4), _fold(r[6], r[7], 4, first4)]
            l2 = [_fold(l1[0], l1[1], 2, first2), _fold(l1[2], l1[3], 2, first2)]
            y = _fold(l2[0], l2[1], 1, first1)
            rows = slice(j * SUBLANES, (j + 1) * SUBLANES)
            a = jnp.sum(y, axis=1, keepdims=True)
            w = gt_ref[0, rows, t:t + 1] * _gelu(a)
            wbuf[rows, :] = jnp.broadcast_to(w, (SUBLANES, LANE))
        nacc = 2
        acc_lo = [jnp.zeros((SUBLANES, LANE), F32) for _ in range(nacc)]
        acc_hi = [jnp.zeros((SUBLANES, LANE), F32) for _ in range(nacc)]
        late0 = ngrp * early
        for k in range(PEER_PAIRS):
            if k % SUBLANES == 0:
                for q in range(late0 + (k // SUBLANES) * late, late0 + (k // SUBLANES + 1) * late):
                    start_row(idn_ref, base + q, nslot, q % 2)
            lo, hi = _unpack_pair(tbuf[slot, base + k, 1])
            wk = wbuf[k:k + 1, :]
            acc_lo[k % nacc] = acc_lo[k % nacc] + lo * wk
            acc_hi[k % nacc] = acc_hi[k % nacc] + hi * wk
        out_lo.append(acc_lo[0] + acc_lo[1])
        out_hi.append(acc_hi[0] + acc_hi[1])

    for first, outs in ((0, out_lo), (SUBLANES, out_hi)):
        for c, rows_c in enumerate(_transpose8(outs, sub)):
            cols = slice((first + c) * LANE, (first + c + 1) * LANE)
            o_ref[:, cols] = xn_ref[:, cols] + mod_ref[0, 5:6, cols] * rows_c

    @pl.when(i == ngroups - 1)
    def _drain():
        wait_slot(lax.rem(i + 1, GATHER_SLOTS))
        wait_slot(nslot)


PACK_ROWS = 256


def _pack_kernel(u_ref, v_ref, o_ref):
    sub = lax.broadcasted_iota(I32, (SUBLANES, LANE), 0)
    half = D_MODEL // 2
    for g in range(PACK_ROWS // SUBLANES):
        rows = slice(g * SUBLANES, (g + 1) * SUBLANES)
        for which, ref in enumerate((u_ref, v_ref)):
            bits = pltpu.bitcast(ref[0, rows, :], jnp.uint32)
            rnd = bits + jnp.uint32(0x7FFF) + jnp.bitwise_and(jnp.right_shift(bits, 16), jnp.uint32(1))
            words = jnp.bitwise_or(jnp.right_shift(rnd[:, :half], 16),
                                   jnp.bitwise_and(rnd[:, half:], jnp.uint32(0xFFFF0000)))
            tiles = _transpose8([words[:, c * LANE:(c + 1) * LANE] for c in range(SUBLANES)], sub)
            for t in range(SUBLANES):
                o_ref[g * SUBLANES + t, which] = tiles[t]


def pack_expert_table(u_tabs, v_tabs, layer):
    _, nexp, d = u_tabs.shape
    in_spec = pl.BlockSpec((1, PACK_ROWS, d), lambda i: (layer, i, 0))
    return pl.pallas_call(
        _pack_kernel,
        grid=(nexp // PACK_ROWS,),
        in_specs=[in_spec, in_spec],
        out_specs=pl.BlockSpec((PACK_ROWS, 2, SUBLANES, LANE), lambda i: (i, 0, 0, 0)),
        out_shape=jax.ShapeDtypeStruct((nexp, 2, SUBLANES, LANE), jnp.uint32),
        compiler_params=_params(("parallel",)),
    )(u_tabs, v_tabs)


def peer_gather(ids, hf, gates_t, xn, mod, table, mod_row_group, ngroups, grp):
    n, d = xn.shape
    gt = GATHER_TOKENS
    assert ngroups >= GATHER_SLOTS - 1 and gt == SUBLANES
    nrows = gt * PEER_PAIRS
    tok_spec = pl.BlockSpec((gt, d), lambda i: (grp(i), 0))

    def ids_spec(ahead):
        return pl.BlockSpec((1, 1, nrows), lambda i: (grp(jnp.minimum(i + ahead, ngroups - 1)), 0, 0),
                            memory_space=pltpu.SMEM)

    return pl.pallas_call(
        functools.partial(_peer_gather_kernel, ngroups=ngroups),
        grid=(ngroups,),
        in_specs=[ids_spec(0), ids_spec(1), ids_spec(GATHER_SLOTS - 1),
                  tok_spec,
                  pl.BlockSpec((1, PEER_PAIRS, gt), lambda i: (grp(i), 0, 0)),
                  tok_spec,
                  pl.BlockSpec((1, 6, d), lambda i: (mod_row_group(grp(i)), 0, 0)),
                  pl.BlockSpec(memory_space=pl.ANY)],
        out_specs=tok_spec,
        out_shape=jax.ShapeDtypeStruct((n, d), F32),
        scratch_shapes=[pltpu.VMEM((GATHER_SLOTS, nrows, 2, SUBLANES, LANE), jnp.uint32),
                        pltpu.VMEM((PEER_PAIRS, LANE), F32),
                        pltpu.SemaphoreType.DMA((GATHER_SLOTS,))],
        compiler_params=_params(("arbitrary",)),
    )(ids, ids, ids, hf, gates_t, xn, mod, table)


def _final_norm_kernel(x_ref, g_ref, o_ref):
    x = x_ref[...]
    o_ref[0] = x * lax.rsqrt(jnp.mean(x * x, axis=-1, keepdims=True) + EPS) * g_ref[...]


def final_norm(x, g, batch, seq, ctx):
    n, d = x.shape
    tb = ctx + seq
    nrb = tb // ROW_TILE
    cb = ctx // ROW_TILE
    return pl.pallas_call(
        _final_norm_kernel,
        grid=(batch, seq // ROW_TILE),
        in_specs=[pl.BlockSpec((ROW_TILE, d), lambda b, i: (b * nrb + cb + i, 0)),
                  pl.BlockSpec((1, d), lambda b, i: (0, 0))],
        out_specs=pl.BlockSpec((1, ROW_TILE, d), lambda b, i: (b, i, 0)),
        out_shape=jax.ShapeDtypeStruct((batch, seq, d), F32),
        compiler_params=_params(("parallel", "parallel")),
    )(x, g)


def kernel(x, c, ctx, c_ctx, ada_w, ada_b, norm_mix, norm_ffn, w_in, b_gate, sgu_norm, sgu_w, sgu_b,
           qk_conv_w, mlstm_norm, pool_w, pool_scale, w_out, peer_wq, peer_keys, peer_u, peer_v,
           norm_final):
    batch, seq, d = x.shape
    ctx_len = ctx.shape[1]
    depth = ada_w.shape[0]
    tb = ctx_len + seq
    n = batch * tb
    assert d == D_MODEL and w_in.shape[2] == OFF_G + N_GATES
    assert ctx_len % ROW_TILE == 0 and seq % ROW_TILE == 0 and seq % GRID_W == 0
    assert n % GATHER_TOKENS == 0 and ROW_TILE % GATHER_TOKENS == 0
    blocks_per_batch = tb // ROW_TILE
    ctx_blocks = ctx_len // ROW_TILE
    nchunks = tb // CHUNK

    def mod_row(i):
        return jnp.where(i % blocks_per_batch < ctx_blocks, batch, i // blocks_per_batch)

    groups_per_block = ROW_TILE // GATHER_TOKENS
    lat_blocks = seq // ROW_TILE

    def mod_row_group(i):
        return mod_row(i // groups_per_block)

    def lat_blk(j):
        return (j // lat_blocks) * blocks_per_batch + ctx_blocks + j % lat_blocks

    cond_rows = -(-(batch + 1) // 8) * 8
    cond = jnp.zeros((cond_rows, d), F32).at[:batch].set(c).at[batch].set(c_ctx)
    mods = ada_modulation_all(cond, ada_w, ada_b).reshape(depth, cond_rows, 6, d)

    xs = jnp.concatenate([ctx, x], axis=1).reshape(n, d)
    lane_pad = LANE - N_GATES
    for l in range(depth):
        mod = mods[l]
        w_main = w_in[l, :, :OFF_G].astype(BF16)
        w_gate = jnp.pad(w_in[l, :, OFF_G:], ((0, 0), (0, lane_pad))).astype(BF16)
        p, graw = in_proj(xs, mod, norm_mix[l][None], w_main, w_gate, mod_row)

        y_a = sgu_mixer(p, sgu_norm[l][None], sgu_w[l].astype(BF16), sgu_b[l].T)

        gp = gate_prep(graw[0], jnp.pad(b_gate[l], (0, lane_pad))[None])
        gp = gp[:, :N_GATES].reshape(n, 2, 2, MLSTM_HEADS)
        gcol = gp.transpose(3, 0, 1, 2).reshape(MLSTM_HEADS, n, 4)
        grow = gcol.reshape(MLSTM_HEADS, n // CHUNK, CHUNK, 4).transpose(0, 1, 3, 2)
        qk = qk_conv(p, qk_conv_w[l], blocks_per_batch, ctx_blocks)
        y_b = mlstm_mixer(p, qk, gcol, grow, mlstm_norm[l][None], batch, tb, ctx_len // CHUNK)

        y_c = pool_mixer(p, pool_w[l].astype(BF16), pool_scale[l].reshape(POOL_GROUPS, 1, POOL_GC),
                         batch, tb, ctx_len, seq)

        xn = out_proj(xs, y_a, y_b, y_c, mod, w_out[l].astype(BF16), mod_row)

        if l == depth - 1:
            nblocks, blk = batch * lat_blocks, lat_blk
        else:
            nblocks, blk = n // ROW_TILE, lambda j: j
        hf, e, gates = peer_route(xn, mod, norm_ffn[l][None], peer_wq[l].astype(BF16),
                                  peer_keys[l].astype(BF16), mod_row, nblocks, blk)
        ids = e.reshape(PEER_PAIRS, n).T.reshape(n // GATHER_TOKENS, 1, GATHER_TOKENS * PEER_PAIRS)
        gates_t = gates.reshape(PEER_PAIRS, n // GATHER_TOKENS, GATHER_TOKENS).transpose(1, 0, 2)
        xs = peer_gather(ids, hf, gates_t, xn, mod, pack_expert_table(peer_u, peer_v, l), mod_row_group,
                         nblocks * groups_per_block,
                         lambda g, blk=blk: blk(g // groups_per_block) * groups_per_block + g % groups_per_block)

    return final_norm(xs, norm_final[None], batch, seq, ctx_len)
```

```python
import functools

import jax
import jax.numpy as jnp
from jax import lax
from jax.experimental import pallas as pl
from jax.experimental.pallas import tpu as pltpu

F32 = jnp.float32
BF16 = jnp.bfloat16
I32 = jnp.int32

EPS = 1e-6
GRID_W = 64
GRID_W_LOG2 = 6
D_MODEL = 2048
D_SGU = D_MODEL // 4
SGU_HEADS = 4
SGU_HD = D_SGU // SGU_HEADS
CHUNK = 128
D_MLSTM = D_MODEL // 2
MLSTM_HEADS = 4
MLSTM_HD = D_MLSTM // MLSTM_HEADS
D_POOL = D_MODEL // 4
POOL_GROUPS = 4
POOL_GC = D_POOL // POOL_GROUPS
N_GATES = 2 * 2 * MLSTM_HEADS
OFF_U = 0
OFF_V = OFF_U + D_SGU
OFF_P = OFF_V + D_SGU
OFF_Q = OFF_P + D_POOL
OFF_O = OFF_Q + D_MLSTM
OFF_K = OFF_O + D_MLSTM
OFF_VM = OFF_K + D_MLSTM
OFF_G = OFF_VM + D_MLSTM
N_KEYS = 128
PEER_HEADS = 8
PEER_TOPK = 16
PEER_DK = 256
PEER_PAIRS = PEER_HEADS * PEER_TOPK

LANE = 128
ROW_TILE = 256
POOL_HALO_TILES = -(-(GRID_W << (POOL_GROUPS - 1)) // ROW_TILE)
GATHER_TOKENS = 8
GATHER_SLOTS = 3
VMEM_LIMIT = 56 * 1024 * 1024

CAND_PAIRS = [(a, b) for a in range(PEER_TOPK) for b in range(PEER_TOPK)
              if (a + 1) * (b + 1) <= PEER_TOPK]
CAND_ROWS = 56


def _params(sem, vmem=VMEM_LIMIT):
    return pltpu.CompilerParams(dimension_semantics=sem, vmem_limit_bytes=vmem)


def _norm_mod(x, g, shift, scale):
    ms = jnp.mean(x * x, axis=-1, keepdims=True)
    y = x * lax.rsqrt(ms + EPS)
    return (y * g) * (1.0 + scale) + shift


def _gelu(x):
    return jax.nn.gelu(x, approximate=True)


def _sigmoid(x):
    return 1.0 / (1.0 + jnp.exp(-x))


def _ada_kernel(cond_ref, w_ref, b_ref, o_ref):
    a = cond_ref[...]
    a = (a * _sigmoid(a)).astype(BF16)
    o_ref[0] = jnp.dot(a, w_ref[0].astype(BF16), preferred_element_type=F32) + b_ref[0]


def ada_modulation_all(cond, ada_w, ada_b):
    depth, d, d6 = ada_w.shape
    rows = cond.shape[0]
    tn = 1024
    return pl.pallas_call(
        _ada_kernel,
        grid=(depth, d6 // tn),
        in_specs=[pl.BlockSpec((rows, d), lambda l, j: (0, 0)),
                  pl.BlockSpec((1, d, tn), lambda l, j: (l, 0, j)),
                  pl.BlockSpec((1, 1, tn), lambda l, j: (l, 0, j))],
        out_specs=pl.BlockSpec((1, rows, tn), lambda l, j: (l, 0, j)),
        out_shape=jax.ShapeDtypeStruct((depth, rows, d6), F32),
        compiler_params=_params(("parallel", "parallel")),
    )(cond, ada_w, ada_b.reshape(depth, 1, d6))


def _in_proj_kernel(x_ref, mod_ref, g_ref, w_ref, wg_ref, p_ref, gate_ref, *, ncols):
    h = _norm_mod(x_ref[...], g_ref[...], mod_ref[0, 0:1, :], mod_ref[0, 1:2, :]).astype(BF16)
    for j in range(ncols // 256):
        sl = slice(j * 256, (j + 1) * 256)
        p_ref[:, sl] = jnp.dot(h, w_ref[:, sl], preferred_element_type=F32).astype(BF16)
    gate_ref[0] = jnp.dot(h, wg_ref[...], preferred_element_type=F32)


def in_proj(x, mod, norm_g, w_main, w_gate, mod_row):
    n, d = x.shape
    ncol_blocks = 2
    ncols = OFF_G // ncol_blocks
    return pl.pallas_call(
        functools.partial(_in_proj_kernel, ncols=ncols),
        grid=(ncol_blocks, n // ROW_TILE),
        in_specs=[pl.BlockSpec((ROW_TILE, d), lambda c, i: (i, 0)),
                  pl.BlockSpec((1, 6, d), lambda c, i: (mod_row(i), 0, 0)),
                  pl.BlockSpec((1, d), lambda c, i: (0, 0)),
                  pl.BlockSpec((d, ncols), lambda c, i: (0, c)),
                  pl.BlockSpec((d, LANE), lambda c, i: (0, 0))],
        out_specs=[pl.BlockSpec((ROW_TILE, ncols), lambda c, i: (i, c)),
                   pl.BlockSpec((1, ROW_TILE, LANE), lambda c, i: (c, i, 0))],
        out_shape=[jax.ShapeDtypeStruct((n, OFF_G), BF16),
                   jax.ShapeDtypeStruct((ncol_blocks, n, LANE), F32)],
        compiler_params=_params(("arbitrary", "arbitrary")),
    )(x, mod, norm_g, w_main, w_gate)


def _sgu_kernel(u_ref, v_ref, g_ref, ws_ref, bs_ref, y_ref):
    u = _gelu(u_ref[...].astype(F32))
    v = _gelu(v_ref[...].astype(F32))
    ms = jnp.mean(v * v, axis=-1, keepdims=True)
    v = (v * lax.rsqrt(ms + EPS) * g_ref[...]).astype(BF16)
    for c in range(ROW_TILE // CHUNK):
        rows = slice(c * CHUNK, (c + 1) * CHUNK)
        for h in range(SGU_HEADS):
            cols = slice(h * SGU_HD, (h + 1) * SGU_HD)
            mixed = jnp.dot(ws_ref[h], v[rows, cols], preferred_element_type=F32) + bs_ref[:, h:h + 1]
            y_ref[rows, cols] = (u[rows, cols] * mixed).astype(BF16)


def sgu_mixer(p, sgu_norm, sgu_w, sgu_bt):
    n = p.shape[0]
    return pl.pallas_call(
        _sgu_kernel,
        grid=(n // ROW_TILE,),
        in_specs=[pl.BlockSpec((ROW_TILE, D_SGU), lambda i: (i, OFF_U // D_SGU)),
                  pl.BlockSpec((ROW_TILE, D_SGU), lambda i: (i, OFF_V // D_SGU)),
                  pl.BlockSpec((1, D_SGU), lambda i: (0, 0)),
                  pl.BlockSpec((SGU_HEADS, CHUNK, CHUNK), lambda i: (0, 0, 0)),
                  pl.BlockSpec((CHUNK, SGU_HEADS), lambda i: (0, 0))],
        out_specs=pl.BlockSpec((ROW_TILE, D_SGU), lambda i: (i, 0)),
        out_shape=jax.ShapeDtypeStruct((n, D_SGU), BF16),
        compiler_params=_params(("parallel",)),
    )(p, p, sgu_norm, sgu_w, sgu_bt)


def _gate_prep_kernel(g_ref, bias_ref, o_ref):
    g = g_ref[...] + bias_ref[...]
    lf = jnp.minimum(g, 0.0) - jnp.log(1.0 + jnp.exp(-jnp.abs(g)))
    row = lax.broadcasted_iota(I32, (CHUNK, CHUNK), 0)
    col = lax.broadcasted_iota(I32, (CHUNK, CHUNK), 1)
    lower = (col <= row).astype(F32)
    upper = (col >= row).astype(F32)
    b_fwd = jnp.dot(lower, lf, preferred_element_type=F32, precision=lax.Precision.HIGHEST)
    b_bwd = jnp.dot(upper, lf, preferred_element_type=F32, precision=lax.Precision.HIGHEST)
    lane = lax.broadcasted_iota(I32, (1, LANE), 1)
    is_input_gate = (lane // MLSTM_HEADS) % 2 == 0
    is_fwd = lane < 2 * MLSTM_HEADS
    o_ref[...] = jnp.where(is_input_gate, g, jnp.where(is_fwd, b_fwd, b_bwd))


def gate_prep(graw, bias):
    n = graw.shape[0]
    return pl.pallas_call(
        _gate_prep_kernel,
        grid=(n // CHUNK,),
        in_specs=[pl.BlockSpec((CHUNK, LANE), lambda i: (i, 0)),
                  pl.BlockSpec((1, LANE), lambda i: (0, 0))],
        out_specs=pl.BlockSpec((CHUNK, LANE), lambda i: (i, 0)),
        out_shape=jax.ShapeDtypeStruct((n, LANE), F32),
        compiler_params=_params(("parallel",)),
    )(graw, bias)


HALO = 16


def _qk_conv_kernel(xm_ref, xp_ref, xn_ref, w_ref, o_ref, *, blocks_per_batch, ctx_blocks):
    i = pl.program_id(0)
    j = pl.program_id(1)
    x = xm_ref[...].astype(F32)
    tm = x.shape[0]
    prev_row = xp_ref[...].astype(F32)[HALO - 1:HALO, :]
    next_row = xn_ref[...].astype(F32)[0:1, :]
    ib = i % blocks_per_batch
    at_start = jnp.logical_or(ib == 0, ib == ctx_blocks)
    at_end = jnp.logical_or(ib == ctx_blocks - 1, ib == blocks_per_batch - 1)
    prev_row = jnp.where(at_start, 0.0, prev_row)
    next_row = jnp.where(at_end, 0.0, next_row)
    rows = lax.broadcasted_iota(I32, (tm, 1), 0)
    x_prev = jnp.where(rows == 0, prev_row, pltpu.roll(x, 1, 0))
    x_next = jnp.where(rows == tm - 1, next_row, pltpu.roll(x, tm - 1, 0))
    y = w_ref[0:1, :] * x_prev + w_ref[1:2, :] * x + w_ref[2:3, :] * x_next
    y = y * _sigmoid(y)
    y = y * jnp.where(j >= 2, MLSTM_HD ** -0.5, 1.0)
    o_ref[...] = y.astype(BF16)


def qk_conv(p, conv_w, blocks_per_batch, ctx_blocks):
    n = p.shape[0]
    tc = 512
    nhalo = n // HALO
    per = ROW_TILE // HALO

    def col(j):
        return jnp.where(j < 2, OFF_Q // tc + j, OFF_K // tc + j - 2)

    return pl.pallas_call(
        functools.partial(_qk_conv_kernel, blocks_per_batch=blocks_per_batch, ctx_blocks=ctx_blocks),
        grid=(n // ROW_TILE, 2 * D_MLSTM // tc),
        in_specs=[pl.BlockSpec((ROW_TILE, tc), lambda i, j: (i, col(j))),
                  pl.BlockSpec((HALO, tc), lambda i, j: (jnp.maximum(i * per - 1, 0), col(j))),
                  pl.BlockSpec((HALO, tc), lambda i, j: (jnp.minimum((i + 1) * per, nhalo - 1), col(j))),
                  pl.BlockSpec((3, tc), lambda i, j: (0, j))],
        out_specs=pl.BlockSpec((ROW_TILE, tc), lambda i, j: (i, j)),
        out_shape=jax.ShapeDtypeStruct((n, 2 * D_MLSTM), BF16),
        compiler_params=_params(("parallel", "parallel")),
    )(p, p, p, conv_w)


def _mlstm_kernel(q_ref, k_ref, v_ref, o_ref, gc_ref, gr_ref, nrm_ref, y_ref,
                  hf_ref, hb_ref, c_ref, n_ref, m_ref, *, nchunks, nctx):
    c_ref[...] = jnp.zeros_like(c_ref)
    n_ref[...] = jnp.zeros_like(n_ref)
    m_ref[...] = jnp.zeros_like(m_ref)
    row = lax.broadcasted_iota(I32, (CHUNK, CHUNK), 0)
    col = lax.broadcasted_iota(I32, (CHUNK, CHUNK), 1)
    seen = (row >= col, row <= col)
    h_refs = (hf_ref, hb_ref)

    def step(d, c):
        r0 = pl.multiple_of(c * CHUNK, CHUNK)
        rows = pl.ds(r0, CHUNK)
        q = q_ref[rows, :]
        k = k_ref[rows, :]
        v = v_ref[rows, :]
        gcol = gc_ref[0, rows, :]
        grow = gr_ref[0, c]
        ig_c, b_c = gcol[:, 2 * d:2 * d + 1], gcol[:, 2 * d + 1:2 * d + 2]
        ig_r, b_r = grow[2 * d:2 * d + 1, :], grow[2 * d + 1:2 * d + 2, :]
        b_last = b_c[CHUNK - 1:CHUNK, :] if d == 0 else b_c[0:1, :]
        m = m_ref[d, 0:1, 0:1]
        g_c = b_last - b_c + ig_c
        m_new = jnp.maximum(b_last + m, jnp.max(g_c, axis=0, keepdims=True))
        wk_c = jnp.exp(g_c - m_new)
        decay = jnp.exp(b_last + m - m_new)

        log_w = jnp.where(seen[d], b_c - b_r + ig_r, -jnp.inf)
        inter = b_c + m
        m_t = jnp.maximum(inter, jnp.max(log_w, axis=1, keepdims=True))
        w_prev = jnp.exp(inter - m_t)
        s = lax.dot_general(q, k, (((1,), (1,)), ((), ())), preferred_element_type=F32)
        s = s * jnp.exp(log_w - m_t)
        ct = c_ref[d]
        nvec = n_ref[d]
        num = (jnp.dot(s.astype(BF16), v, preferred_element_type=F32)
               + w_prev * jnp.dot(q, ct.astype(BF16), preferred_element_type=F32))
        qf = q.astype(F32)
        kf = k.astype(F32)
        den = jnp.sum(s, axis=1, keepdims=True) + w_prev * jnp.sum(qf * nvec, axis=1, keepdims=True)
        h_refs[d][rows, :] = num / jnp.maximum(jnp.abs(den), jnp.exp(-m_t))

        wv = (v.astype(F32) * wk_c).astype(BF16)
        c_ref[d] = decay * ct + lax.dot_general(k, wv, (((0,), (0,)), ((), ())),
                                                preferred_element_type=F32)
        n_ref[d] = decay * nvec + jnp.sum(kf * wk_c, axis=0, keepdims=True)
        m_ref[d] = jnp.broadcast_to(m_new, (1, LANE))

    def body(i, carry):
        step(0, i)
        step(1, jnp.where(i < nctx, nctx - 1 - i, nchunks - 1 - (i - nctx)))
        return carry

    lax.fori_loop(0, nchunks, body, 0)

    def finish(c, carry):
        rows = pl.ds(pl.multiple_of(c * CHUNK, CHUNK), CHUNK)
        h = hf_ref[rows, :] + hb_ref[rows, :]
        hn = h * lax.rsqrt(jnp.mean(h * h, axis=-1, keepdims=True) + EPS) * nrm_ref[...]
        y_ref[rows, :] = (_sigmoid(o_ref[rows, :].astype(F32)) * hn).astype(BF16)
        return carry

    lax.fori_loop(0, nchunks, finish, 0)


def mlstm_mixer(p, qk, gcol, grow, mlstm_norm, batch, tb, nctx):
    n = p.shape[0]
    nchunks = tb // CHUNK
    hd = MLSTM_HD
    return pl.pallas_call(
        functools.partial(_mlstm_kernel, nchunks=nchunks, nctx=nctx),
        grid=(batch, MLSTM_HEADS),
        in_specs=[pl.BlockSpec((tb, hd), lambda b, h: (b, h)),
                  pl.BlockSpec((tb, hd), lambda b, h: (b, MLSTM_HEADS + h)),
                  pl.BlockSpec((tb, hd), lambda b, h: (b, OFF_VM // hd + h)),
                  pl.BlockSpec((tb, hd), lambda b, h: (b, OFF_O // hd + h)),
                  pl.BlockSpec((1, tb, 4), lambda b, h: (h, b, 0)),
                  pl.BlockSpec((1, nchunks, 4, CHUNK), lambda b, h: (h, b, 0, 0)),
                  pl.BlockSpec((1, hd), lambda b, h: (0, h))],
        out_specs=pl.BlockSpec((tb, hd), lambda b, h: (b, h)),
        out_shape=jax.ShapeDtypeStruct((n, D_MLSTM), BF16),
        scratch_shapes=[pltpu.VMEM((tb, hd), F32), pltpu.VMEM((tb, hd), F32),
                        pltpu.VMEM((2, hd, hd), F32), pltpu.VMEM((2, 1, hd), F32),
                        pltpu.VMEM((2, 1, LANE), F32)],
        compiler_params=_params(("parallel", "parallel")),
    )(qk, qk, p, p, gcol, grow, mlstm_norm)


def _pool_kernel(xfull_ref, xrow_ref, w_ref, sc_ref, y_ref, m_ref, inv_ref, *, tb, ctx, seq):
    g = pl.program_id(0)
    i = pl.program_id(1)
    b = pl.program_id(2)
    tm = xrow_ref.shape[0]
    half = jnp.left_shift(1, g)
    win = 2 * half

    def grid_coords(pos):
        is_ctx = pos < ctx
        lat = pos - ctx
        r = jnp.where(is_ctx, 0, jnp.right_shift(lat, GRID_W_LOG2))
        c = jnp.where(is_ctx, pos, jnp.bitwise_and(lat, GRID_W - 1))
        return is_ctx, r, c

    band = m_ref.shape[1]
    first_tile = jnp.clip(i - POOL_HALO_TILES, 0, (tb - band) // tm)
    s0 = pl.multiple_of(first_tile * tm, tm)

    @pl.when(b == 0)
    def _build_window_matrix():
        t = i * tm + lax.broadcasted_iota(I32, (tm, 1), 0)
        s = s0 + lax.broadcasted_iota(I32, (1, band), 1)
        ctx_t, r_t, c_t = grid_coords(t)
        ctx_s, r_s, c_s = grid_coords(s)
        dr = r_s - r_t + half
        dc = c_s - c_t + half
        inside = (ctx_t == ctx_s) & (dr >= 0) & (dr < win) & (dc >= 0) & (dc < win)
        m_ref[...] = jnp.where(inside, 1.0, 0.0).astype(BF16)
        nrows = jnp.where(ctx_t, 1, seq // GRID_W)
        ncols = jnp.where(ctx_t, ctx, GRID_W)
        cnt_r = jnp.minimum(r_t - half + win, nrows) - jnp.maximum(r_t - half, 0)
        cnt_c = jnp.minimum(c_t - half + win, ncols) - jnp.maximum(c_t - half, 0)
        inv_ref[...] = 1.0 / (cnt_r * cnt_c).astype(F32)

    tot = jnp.dot(m_ref[...], xfull_ref[pl.ds(s0, band), :], preferred_element_type=F32)
    d = (tot * inv_ref[...] - xrow_ref[...].astype(F32)).astype(BF16)
    y = jnp.dot(d, w_ref[0], preferred_element_type=F32) * sc_ref[0]
    y_ref[...] = y.astype(BF16)


def pool_mixer(p, pool_w, pool_scale, batch, tb, ctx, seq):
    n = p.shape[0]
    nrb = tb // ROW_TILE
    c0 = OFF_P // POOL_GC
    band = min(2 * POOL_HALO_TILES + 1, nrb) * ROW_TILE
    return pl.pallas_call(
        functools.partial(_pool_kernel, tb=tb, ctx=ctx, seq=seq),
        grid=(POOL_GROUPS, nrb, batch),
        in_specs=[pl.BlockSpec((tb, POOL_GC), lambda g, i, b: (b, c0 + g)),
                  pl.BlockSpec((ROW_TILE, POOL_GC), lambda g, i, b: (b * nrb + i, c0 + g)),
                  pl.BlockSpec((1, POOL_GC, POOL_GC), lambda g, i, b: (g, 0, 0)),
                  pl.BlockSpec((1, 1, POOL_GC), lambda g, i, b: (g, 0, 0))],
        out_specs=pl.BlockSpec((ROW_TILE, POOL_GC), lambda g, i, b: (b * nrb + i, g)),
        out_shape=jax.ShapeDtypeStruct((n, D_POOL), BF16),
        scratch_shapes=[pltpu.VMEM((ROW_TILE, band), BF16), pltpu.VMEM((ROW_TILE, 1), F32)],
        compiler_params=_params(("arbitrary", "arbitrary", "arbitrary")),
    )(p, p, pool_w, pool_scale)


def _out_proj_kernel(x_ref, ya_ref, yb_ref, yc_ref, mod_ref, w_ref, o_ref):
    ya, yb, yc = ya_ref[...], yb_ref[...], yc_ref[...]
    k1 = D_SGU
    k2 = D_SGU + D_MLSTM
    for j in range(D_MODEL // 256):
        sl = slice(j * 256, (j + 1) * 256)
        acc = jnp.dot(ya, w_ref[0:k1, sl], preferred_element_type=F32)
        acc += jnp.dot(yb, w_ref[k1:k2, sl], preferred_element_type=F32)
        acc += jnp.dot(yc, w_ref[k2:D_MODEL, sl], preferred_element_type=F32)
        o_ref[:, sl] = x_ref[:, sl] + mod_ref[0, 2:3, sl] * acc


def out_proj(x, ya, yb, yc, mod, w_out, mod_row):
    n, d = x.shape
    return pl.pallas_call(
        _out_proj_kernel,
        grid=(n // ROW_TILE,),
        in_specs=[pl.BlockSpec((ROW_TILE, d), lambda i: (i, 0)),
                  pl.BlockSpec((ROW_TILE, D_SGU), lambda i: (i, 0)),
                  pl.BlockSpec((ROW_TILE, D_MLSTM), lambda i: (i, 0)),
                  pl.BlockSpec((ROW_TILE, D_POOL), lambda i: (i, 0)),
                  pl.BlockSpec((1, 6, d), lambda i: (mod_row(i), 0, 0)),
                  pl.BlockSpec((d, d), lambda i: (0, 0))],
        out_specs=pl.BlockSpec((ROW_TILE, d), lambda i: (i, 0)),
        out_shape=jax.ShapeDtypeStruct((n, d), F32),
        compiler_params=_params(("parallel",)),
    )(x, ya, yb, yc, mod, w_out)


def _topk_rows(vals, k):
    nrows = vals.shape[0]
    iota = lax.broadcasted_iota(I32, vals.shape, 0)
    out_v, out_i = [], []
    for _ in range(k):
        mx = jnp.max(vals, axis=0, keepdims=True)
        am = jnp.min(jnp.where(vals == mx, iota, nrows), axis=0, keepdims=True)
        out_v.append(mx)
        out_i.append(am)
        vals = jnp.where(iota == am, -jnp.inf, vals)
    return out_v, out_i


def _peer_route_kernel(x_ref, mod_ref, g_ref, wq_ref, keys_ref, hf_ref, e_ref, gate_ref,
                       hb_ref, cand_ref, cid_ref):
    h = pl.program_id(1)

    @pl.when(h == 0)
    def _norm():
        hf = _norm_mod(x_ref[...], g_ref[...], mod_ref[0, 3:4, :], mod_ref[0, 4:5, :])
        hf_ref[...] = hf
        hb_ref[...] = hf.astype(BF16)

    q = jnp.dot(hb_ref[...], wq_ref[...], preferred_element_type=F32).astype(BF16)
    half = PEER_DK // 2
    nt = (((1,), (1,)), ((), ()))
    scores1 = lax.dot_general(keys_ref[0], q[:, :half], nt, preferred_element_type=F32)
    scores2 = lax.dot_general(keys_ref[1], q[:, half:], nt, preferred_element_type=F32)
    npad = CAND_ROWS - len(CAND_PAIRS)
    for c in range(q.shape[0] // LANE):
        lanes = slice(c * LANE, (c + 1) * LANE)
        s1, i1 = _topk_rows(scores1[:, lanes], PEER_TOPK)
        s2, i2 = _topk_rows(scores2[:, lanes], PEER_TOPK)
        for r, (a, b) in enumerate(CAND_PAIRS):
            cand_ref[r:r + 1, lanes] = s1[a] + s2[b]
            cid_ref[r:r + 1, lanes] = i1[a] * N_KEYS + i2[b]
        cand_ref[len(CAND_PAIRS):, lanes] = jnp.full((npad, LANE), -jnp.inf, F32)
        cid_ref[len(CAND_PAIRS):, lanes] = jnp.zeros((npad, LANE), I32)
        sc, slot = _topk_rows(cand_ref[:, lanes], PEER_TOPK)
        cid = cid_ref[:, lanes]
        riota = lax.broadcasted_iota(I32, cid.shape, 0)
        ex = [jnp.exp(v - sc[0]) for v in sc]
        tot = ex[0]
        for v in ex[1:]:
            tot = tot + v
        for j in range(PEER_TOPK):
            e_ref[0, j:j + 1, lanes] = jnp.sum(jnp.where(riota == slot[j], cid, 0), axis=0, keepdims=True)
            gate_ref[0, j:j + 1, lanes] = ex[j] / tot


def peer_route(x, mod, norm_g, wq, keys, mod_row, nblocks, blk):
    n, d = x.shape
    tt = ROW_TILE
    return pl.pallas_call(
        _peer_route_kernel,
        grid=(nblocks, PEER_HEADS),
        in_specs=[pl.BlockSpec((tt, d), lambda i, h: (blk(i), 0)),
                  pl.BlockSpec((1, 6, d), lambda i, h: (mod_row(blk(i)), 0, 0)),
                  pl.BlockSpec((1, d), lambda i, h: (0, 0)),
                  pl.BlockSpec((d, PEER_DK), lambda i, h: (0, h)),
                  pl.BlockSpec((2, N_KEYS, PEER_DK // 2), lambda i, h: (0, 0, 0))],
        out_specs=[pl.BlockSpec((tt, d), lambda i, h: (blk(i), 0)),
                   pl.BlockSpec((1, PEER_TOPK, tt), lambda i, h: (h, 0, blk(i))),
                   pl.BlockSpec((1, PEER_TOPK, tt), lambda i, h: (h, 0, blk(i)))],
        out_shape=[jax.ShapeDtypeStruct((n, d), F32),
                   jax.ShapeDtypeStruct((PEER_HEADS, PEER_TOPK, n), I32),
                   jax.ShapeDtypeStruct((PEER_HEADS, PEER_TOPK, n), F32)],
        scratch_shapes=[pltpu.VMEM((tt, d), BF16), pltpu.VMEM((CAND_ROWS, tt), F32),
                        pltpu.VMEM((CAND_ROWS, tt), I32)],
        compiler_params=_params(("parallel", "arbitrary")),
    )(x, mod, norm_g, wq, keys)


ISSUE_UNROLL = 8
SUBLANES = 8
FOLD_ORDER = (0, 4, 2, 6, 1, 5, 3, 7)


def _fold(a, b, shift, first):
    return (jnp.where(first, a, pltpu.roll(b, shift, 0))
            + jnp.where(first, pltpu.roll(a, SUBLANES - shift, 0), b))


def _transpose8(vs, sub):
    vs = list(vs)
    for d in (4, 2, 1):
        keep = jnp.bitwise_and(sub, d) == 0
        nxt = list(vs)
        for i in range(SUBLANES):
            if i & d == 0:
                a, b = vs[i], vs[i + d]
                nxt[i] = jnp.where(keep, a, pltpu.roll(b, d, 0))
                nxt[i + d] = jnp.where(keep, pltpu.roll(a, SUBLANES - d, 0), b)
        vs = nxt
    return vs


def _unpack_pair(words):
    lo = pltpu.bitcast(jnp.left_shift(words, 16), F32)
    hi = pltpu.bitcast(jnp.bitwise_and(words, jnp.uint32(0xFFFF0000)), F32)
    return lo, hi


def _peer_gather_kernel(ids_ref, id1_ref, idn_ref, hf_ref, gt_ref, xn_ref, mod_ref, tab_hbm, o_ref,
                        tbuf, wbuf, sem, *, ngroups):
    i = pl.program_id(0)
    slot = lax.rem(i, GATHER_SLOTS)
    nslot = lax.rem(i + GATHER_SLOTS - 1, GATHER_SLOTS)
    nrows = GATHER_TOKENS * PEER_PAIRS

    def start_row(ids, r, s, priority):
        pltpu.make_async_copy(tab_hbm.at[ids[0, 0, r]], tbuf.at[s, r], sem.at[s]).start(priority=priority)

    def wait_slot(s):
        pltpu.make_async_copy(tab_hbm.at[pl.ds(0, nrows)], tbuf.at[s], sem.at[s]).wait()

    @pl.when(i == 0)
    def _first_groups():
        def body(g, carry):
            for j in range(ISSUE_UNROLL):
                start_row(ids_ref, g * ISSUE_UNROLL + j, 0, j % 2)
                start_row(id1_ref, g * ISSUE_UNROLL + j, 1, j % 2)
            return carry
        lax.fori_loop(0, nrows // ISSUE_UNROLL, body, 0)

    wait_slot(slot)

    sub = lax.broadcasted_iota(I32, (SUBLANES, LANE), 0)
    first4, first2, first1 = sub < 4, sub % 4 < 2, sub % 2 < 1
    ngrp = PEER_PAIRS // SUBLANES
    early = 6
    late = SUBLANES - early
    def chunks(ref, first):
        return [ref[:, (first + c) * LANE:(first + c + 1) * LANE] for c in range(SUBLANES)]
    xs_lo = _transpose8(chunks(hf_ref, 0), sub)
    xs_hi = _transpose8(chunks(hf_ref, SUBLANES), sub)
    out_lo, out_hi = [], []
    for t in range(GATHER_TOKENS):
        x_lo = xs_lo[t]
        x_hi = xs_hi[t]
        base = t * PEER_PAIRS
        for j in range(ngrp):
            for q in range(early):
                start_row(idn_ref, base + j * early + q, nslot, q % 2)
            r = []
            for p in FOLD_ORDER:
                lo, hi = _unpack_pair(tbuf[slot, base + j * SUBLANES + p, 0])
                r.append(lo * x_lo + hi * x_hi)
            l1 = [_fold(r[0], r[1], 4, first4), _fold(r[2], r[3], 4, first4),
                  _fold(r[4], r[5], 4, first4), _fold(r[6], r[7], 4, first4)]
            l2 = [_fold(l1[0], l1[1], 2, first2), _fold(l1[2], l1[3], 2, first2)]
            y = _fold(l2[0], l2[1], 1, first1)
            rows = slice(j * SUBLANES, (j + 1) * SUBLANES)
            a = jnp.sum(y, axis=1, keepdims=True)
            w = gt_ref[0, rows, t:t + 1] * _gelu(a)
            wbuf[rows, :] = jnp.broadcast_to(w, (SUBLANES, LANE))
        nacc = 2
        acc_lo = [jnp.zeros((SUBLANES, LANE), F32) for _ in range(nacc)]
        acc_hi = [jnp.zeros((SUBLANES, LANE), F32) for _ in range(nacc)]
        late0 = ngrp * early
        for k in range(PEER_PAIRS):
            if k % SUBLANES == 0:
                for q in range(late0 + (k // SUBLANES) * late, late0 + (k // SUBLANES + 1) * late):
                    start_row(idn_ref, base + q, nslot, q % 2)
            lo, hi = _unpack_pair(tbuf[slot, base + k, 1])
            wk = wbuf[k:k + 1, :]
            acc_lo[k % nacc] = acc_lo[k % nacc] + lo * wk
            acc_hi[k % nacc] = acc_hi[k % nacc] + hi * wk
        out_lo.append(acc_lo[0] + acc_lo[1])
        out_hi.append(acc_hi[0] + acc_hi[1])

    for first, outs in ((0, out_lo), (SUBLANES, out_hi)):
        for c, rows_c in enumerate(_transpose8(outs, sub)):
            cols = slice((first + c) * LANE, (first + c + 1) * LANE)
            o_ref[:, cols] = xn_ref[:, cols] + mod_ref[0, 5:6, cols] * rows_c

    @pl.when(i == ngroups - 1)
    def _drain():
        wait_slot(lax.rem(i + 1, GATHER_SLOTS))
        wait_slot(nslot)


PACK_ROWS = 256


def _pack_kernel(u_ref, v_ref, o_ref):
    sub = lax.broadcasted_iota(I32, (SUBLANES, LANE), 0)
    half = D_MODEL // 2
    for g in range(PACK_ROWS // SUBLANES):
        rows = slice(g * SUBLANES, (g + 1) * SUBLANES)
        for which, ref in enumerate((u_ref, v_ref)):
            bits = pltpu.bitcast(ref[0, rows, :], jnp.uint32)
            rnd = bits + jnp.uint32(0x7FFF) + jnp.bitwise_and(jnp.right_shift(bits, 16), jnp.uint32(1))
            words = jnp.bitwise_or(jnp.right_shift(rnd[:, :half], 16),
                                   jnp.bitwise_and(rnd[:, half:], jnp.uint32(0xFFFF0000)))
            tiles = _transpose8([words[:, c * LANE:(c + 1) * LANE] for c in range(SUBLANES)], sub)
            for t in range(SUBLANES):
                o_ref[g * SUBLANES + t, which] = tiles[t]


def pack_expert_table(u_tabs, v_tabs, layer):
    _, nexp, d = u_tabs.shape
    in_spec = pl.BlockSpec((1, PACK_ROWS, d), lambda i: (layer, i, 0))
    return pl.pallas_call(
        _pack_kernel,
        grid=(nexp // PACK_ROWS,),
        in_specs=[in_spec, in_spec],
        out_specs=pl.BlockSpec((PACK_ROWS, 2, SUBLANES, LANE), lambda i: (i, 0, 0, 0)),
        out_shape=jax.ShapeDtypeStruct((nexp, 2, SUBLANES, LANE), jnp.uint32),
        compiler_params=_params(("parallel",)),
    )(u_tabs, v_tabs)


def peer_gather(ids, hf, gates_t, xn, mod, table, mod_row_group, ngroups, grp):
    n, d = xn.shape
    gt = GATHER_TOKENS
    assert ngroups >= GATHER_SLOTS - 1 and gt == SUBLANES
    nrows = gt * PEER_PAIRS
    tok_spec = pl.BlockSpec((gt, d), lambda i: (grp(i), 0))

    def ids_spec(ahead):
        return pl.BlockSpec((1, 1, nrows), lambda i: (grp(jnp.minimum(i + ahead, ngroups - 1)), 0, 0),
                            memory_space=pltpu.SMEM)

    return pl.pallas_call(
        functools.partial(_peer_gather_kernel, ngroups=ngroups),
        grid=(ngroups,),
        in_specs=[ids_spec(0), ids_spec(1), ids_spec(GATHER_SLOTS - 1),
                  tok_spec,
                  pl.BlockSpec((1, PEER_PAIRS, gt), lambda i: (grp(i), 0, 0)),
                  tok_spec,
                  pl.BlockSpec((1, 6, d), lambda i: (mod_row_group(grp(i)), 0, 0)),
                  pl.BlockSpec(memory_space=pl.ANY)],
        out_specs=tok_spec,
        out_shape=jax.ShapeDtypeStruct((n, d), F32),
        scratch_shapes=[pltpu.VMEM((GATHER_SLOTS, nrows, 2, SUBLANES, LANE), jnp.uint32),
                        pltpu.VMEM((PEER_PAIRS, LANE), F32),
                        pltpu.SemaphoreType.DMA((GATHER_SLOTS,))],
        compiler_params=_params(("arbitrary",)),
    )(ids, ids, ids, hf, gates_t, xn, mod, table)


def _final_norm_kernel(x_ref, g_ref, o_ref):
    x = x_ref[...]
    o_ref[0] = x * lax.rsqrt(jnp.mean(x * x, axis=-1, keepdims=True) + EPS) * g_ref[...]


def final_norm(x, g, batch, seq, ctx):
    n, d = x.shape
    tb = ctx + seq
    nrb = tb // ROW_TILE
    cb = ctx // ROW_TILE
    return pl.pallas_call(
        _final_norm_kernel,
        grid=(batch, seq // ROW_TILE),
        in_specs=[pl.BlockSpec((ROW_TILE, d), lambda b, i: (b * nrb + cb + i, 0)),
                  pl.BlockSpec((1, d), lambda b, i: (0, 0))],
        out_specs=pl.BlockSpec((1, ROW_TILE, d), lambda b, i: (b, i, 0)),
        out_shape=jax.ShapeDtypeStruct((batch, seq, d), F32),
        compiler_params=_params(("parallel", "parallel")),
    )(x, g)


def kernel(x, c, ctx, c_ctx, ada_w, ada_b, norm_mix, norm_ffn, w_in, b_gate, sgu_norm, sgu_w, sgu_b,
           qk_conv_w, mlstm_norm, pool_w, pool_scale, w_out, peer_wq, peer_keys, peer_u, peer_v,
           norm_final):
    batch, seq, d = x.shape
    ctx_len = ctx.shape[1]
    depth = ada_w.shape[0]
    tb = ctx_len + seq
    n = batch * tb
    assert d == D_MODEL and w_in.shape[2] == OFF_G + N_GATES
    assert ctx_len % ROW_TILE == 0 and seq % ROW_TILE == 0 and seq % GRID_W == 0
    assert n % GATHER_TOKENS == 0 and ROW_TILE % GATHER_TOKENS == 0
    blocks_per_batch = tb // ROW_TILE
    ctx_blocks = ctx_len // ROW_TILE
    nchunks = tb // CHUNK

    def mod_row(i):
        return jnp.where(i % blocks_per_batch < ctx_blocks, batch, i // blocks_per_batch)

    groups_per_block = ROW_TILE // GATHER_TOKENS
    lat_blocks = seq // ROW_TILE

    def mod_row_group(i):
        return mod_row(i // groups_per_block)

    def lat_blk(j):
        return (j // lat_blocks) * blocks_per_batch + ctx_blocks + j % lat_blocks

    cond_rows = -(-(batch + 1) // 8) * 8
    cond = jnp.zeros((cond_rows, d), F32).at[:batch].set(c).at[batch].set(c_ctx)
    mods = ada_modulation_all(cond, ada_w, ada_b).reshape(depth, cond_rows, 6, d)

    xs = jnp.concatenate([ctx, x], axis=1).reshape(n, d)
    lane_pad = LANE - N_GATES
    for l in range(depth):
        mod = mods[l]
        w_main = w_in[l, :, :OFF_G].astype(BF16)
        w_gate = jnp.pad(w_in[l, :, OFF_G:], ((0, 0), (0, lane_pad))).astype(BF16)
        p, graw = in_proj(xs, mod, norm_mix[l][None], w_main, w_gate, mod_row)

        y_a = sgu_mixer(p, sgu_norm[l][None], sgu_w[l].astype(BF16), sgu_b[l].T)

        gp = gate_prep(graw[0], jnp.pad(b_gate[l], (0, lane_pad))[None])
        gp = gp[:, :N_GATES].reshape(n, 2, 2, MLSTM_HEADS)
        gcol = gp.transpose(3, 0, 1, 2).reshape(MLSTM_HEADS, n, 4)
        grow = gcol.reshape(MLSTM_HEADS, n // CHUNK, CHUNK, 4).transpose(0, 1, 3, 2)
        qk = qk_conv(p, qk_conv_w[l], blocks_per_batch, ctx_blocks)
        y_b = mlstm_mixer(p, qk, gcol, grow, mlstm_norm[l][None], batch, tb, ctx_len // CHUNK)

        y_c = pool_mixer(p, pool_w[l].astype(BF16), pool_scale[l].reshape(POOL_GROUPS, 1, POOL_GC),
                         batch, tb, ctx_len, seq)

        xn = out_proj(xs, y_a, y_b, y_c, mod, w_out[l].astype(BF16), mod_row)

        if l == depth - 1:
            nblocks, blk = batch * lat_blocks, lat_blk

            def grp(g):
                return lat_blk(g // groups_per_block) * groups_per_block + g % groups_per_block
        else:
            nblocks, blk = n // ROW_TILE, lambda j: j

            def grp(g):
                return g
        hf, e, gates = peer_route(xn, mod, norm_ffn[l][None], peer_wq[l].astype(BF16),
                                  peer_keys[l].astype(BF16), mod_row, nblocks, blk)
        ids = e.reshape(PEER_PAIRS, n).T.reshape(n // GATHER_TOKENS, 1, GATHER_TOKENS * PEER_PAIRS)
        gates_t = gates.reshape(PEER_PAIRS, n // GATHER_TOKENS, GATHER_TOKENS).transpose(1, 0, 2)
        xs = peer_gather(ids, hf, gates_t, xn, mod, pack_expert_table(peer_u, peer_v, l), mod_row_group,
                         nblocks * groups_per_block, grp)

    return final_norm(xs, norm_final[None], batch, seq, ctx_len)
```

```python
import functools

import jax
import jax.numpy as jnp
from jax import lax
from jax.experimental import pallas as pl
from jax.experimental.pallas import tpu as pltpu

F32 = jnp.float32
BF16 = jnp.bfloat16
I32 = jnp.int32

EPS = 1e-6
GRID_W = 64
GRID_W_LOG2 = 6
D_MODEL = 2048
D_SGU = D_MODEL // 4
SGU_HEADS = 4
SGU_HD = D_SGU // SGU_HEADS
CHUNK = 128
D_MLSTM = D_MODEL // 2
MLSTM_HEADS = 4
MLSTM_HD = D_MLSTM // MLSTM_HEADS
D_POOL = D_MODEL // 4
POOL_GROUPS = 4
POOL_GC = D_POOL // POOL_GROUPS
N_GATES = 2 * 2 * MLSTM_HEADS
OFF_U = 0
OFF_V = OFF_U + D_SGU
OFF_P = OFF_V + D_SGU
OFF_Q = OFF_P + D_POOL
OFF_O = OFF_Q + D_MLSTM
OFF_K = OFF_O + D_MLSTM
OFF_VM = OFF_K + D_MLSTM
OFF_G = OFF_VM + D_MLSTM
N_KEYS = 128
PEER_HEADS = 8
PEER_TOPK = 16
PEER_DK = 256
PEER_PAIRS = PEER_HEADS * PEER_TOPK

LANE = 128
ROW_TILE = 256
POOL_HALO_TILES = -(-(GRID_W << (POOL_GROUPS - 1)) // ROW_TILE)
GATHER_TOKENS = 8
GATHER_SLOTS = 3
VMEM_LIMIT = 56 * 1024 * 1024

CAND_PAIRS = [(a, b) for a in range(PEER_TOPK) for b in range(PEER_TOPK)
              if (a + 1) * (b + 1) <= PEER_TOPK]
CAND_ROWS = 56


def _params(sem, vmem=VMEM_LIMIT):
    return pltpu.CompilerParams(dimension_semantics=sem, vmem_limit_bytes=vmem)


def _norm_mod(x, g, shift, scale):
    ms = jnp.mean(x * x, axis=-1, keepdims=True)
    y = x * lax.rsqrt(ms + EPS)
    return (y * g) * (1.0 + scale) + shift


def _gelu(x):
    return jax.nn.gelu(x, approximate=True)


def _sigmoid(x):
    return 1.0 / (1.0 + jnp.exp(-x))


def _ada_kernel(cond_ref, w_ref, b_ref, o_ref):
    a = cond_ref[...]
    a = (a * _sigmoid(a)).astype(BF16)
    o_ref[0] = jnp.dot(a, w_ref[0].astype(BF16), preferred_element_type=F32) + b_ref[0]


def ada_modulation_all(cond, ada_w, ada_b):
    depth, d, d6 = ada_w.shape
    rows = cond.shape[0]
    tn = 1024
    return pl.pallas_call(
        _ada_kernel,
        grid=(depth, d6 // tn),
        in_specs=[pl.BlockSpec((rows, d), lambda l, j: (0, 0)),
                  pl.BlockSpec((1, d, tn), lambda l, j: (l, 0, j)),
                  pl.BlockSpec((1, 1, tn), lambda l, j: (l, 0, j))],
        out_specs=pl.BlockSpec((1, rows, tn), lambda l, j: (l, 0, j)),
        out_shape=jax.ShapeDtypeStruct((depth, rows, d6), F32),
        compiler_params=_params(("parallel", "parallel")),
    )(cond, ada_w, ada_b.reshape(depth, 1, d6))


def _in_proj_kernel(x_ref, mod_ref, g_ref, w_ref, wg_ref, p_ref, gate_ref, *, ncols):
    h = _norm_mod(x_ref[...], g_ref[...], mod_ref[0, 0:1, :], mod_ref[0, 1:2, :]).astype(BF16)
    for j in range(ncols // 256):
        sl = slice(j * 256, (j + 1) * 256)
        p_ref[:, sl] = jnp.dot(h, w_ref[:, sl], preferred_element_type=F32).astype(BF16)
    gate_ref[0] = jnp.dot(h, wg_ref[...], preferred_element_type=F32)


def in_proj(x, mod, norm_g, w_main, w_gate, mod_row):
    n, d = x.shape
    ncol_blocks = 2
    ncols = OFF_G // ncol_blocks
    return pl.pallas_call(
        functools.partial(_in_proj_kernel, ncols=ncols),
        grid=(ncol_blocks, n // ROW_TILE),
        in_specs=[pl.BlockSpec((ROW_TILE, d), lambda c, i: (i, 0)),
                  pl.BlockSpec((1, 6, d), lambda c, i: (mod_row(i), 0, 0)),
                  pl.BlockSpec((1, d), lambda c, i: (0, 0)),
                  pl.BlockSpec((d, ncols), lambda c, i: (0, c)),
                  pl.BlockSpec((d, LANE), lambda c, i: (0, 0))],
        out_specs=[pl.BlockSpec((ROW_TILE, ncols), lambda c, i: (i, c)),
                   pl.BlockSpec((1, ROW_TILE, LANE), lambda c, i: (c, i, 0))],
        out_shape=[jax.ShapeDtypeStruct((n, OFF_G), BF16),
                   jax.ShapeDtypeStruct((ncol_blocks, n, LANE), F32)],
        compiler_params=_params(("arbitrary", "arbitrary")),
    )(x, mod, norm_g, w_main, w_gate)


def _sgu_kernel(u_ref, v_ref, g_ref, ws_ref, bs_ref, y_ref):
    u = _gelu(u_ref[...].astype(F32))
    v = _gelu(v_ref[...].astype(F32))
    ms = jnp.mean(v * v, axis=-1, keepdims=True)
    v = (v * lax.rsqrt(ms + EPS) * g_ref[...]).astype(BF16)
    for c in range(ROW_TILE // CHUNK):
        rows = slice(c * CHUNK, (c + 1) * CHUNK)
        for h in range(SGU_HEADS):
            cols = slice(h * SGU_HD, (h + 1) * SGU_HD)
            mixed = jnp.dot(ws_ref[h], v[rows, cols], preferred_element_type=F32) + bs_ref[:, h:h + 1]
            y_ref[rows, cols] = (u[rows, cols] * mixed).astype(BF16)


def sgu_mixer(p, sgu_norm, sgu_w, sgu_bt):
    n = p.shape[0]
    return pl.pallas_call(
        _sgu_kernel,
        grid=(n // ROW_TILE,),
        in_specs=[pl.BlockSpec((ROW_TILE, D_SGU), lambda i: (i, OFF_U // D_SGU)),
                  pl.BlockSpec((ROW_TILE, D_SGU), lambda i: (i, OFF_V // D_SGU)),
                  pl.BlockSpec((1, D_SGU), lambda i: (0, 0)),
                  pl.BlockSpec((SGU_HEADS, CHUNK, CHUNK), lambda i: (0, 0, 0)),
                  pl.BlockSpec((CHUNK, SGU_HEADS), lambda i: (0, 0))],
        out_specs=pl.BlockSpec((ROW_TILE, D_SGU), lambda i: (i, 0)),
        out_shape=jax.ShapeDtypeStruct((n, D_SGU), BF16),
        compiler_params=_params(("parallel",)),
    )(p, p, sgu_norm, sgu_w, sgu_bt)


def _gate_prep_kernel(g_ref, bias_ref, o_ref):
    g = g_ref[...] + bias_ref[...]
    lf = jnp.minimum(g, 0.0) - jnp.log(1.0 + jnp.exp(-jnp.abs(g)))
    row = lax.broadcasted_iota(I32, (CHUNK, CHUNK), 0)
    col = lax.broadcasted_iota(I32, (CHUNK, CHUNK), 1)
    lower = (col <= row).astype(F32)
    upper = (col >= row).astype(F32)
    b_fwd = jnp.dot(lower, lf, preferred_element_type=F32, precision=lax.Precision.HIGHEST)
    b_bwd = jnp.dot(upper, lf, preferred_element_type=F32, precision=lax.Precision.HIGHEST)
    lane = lax.broadcasted_iota(I32, (1, LANE), 1)
    is_input_gate = (lane // MLSTM_HEADS) % 2 == 0
    is_fwd = lane < 2 * MLSTM_HEADS
    o_ref[...] = jnp.where(is_input_gate, g, jnp.where(is_fwd, b_fwd, b_bwd))


def gate_prep(graw, bias):
    n = graw.shape[0]
    return pl.pallas_call(
        _gate_prep_kernel,
        grid=(n // CHUNK,),
        in_specs=[pl.BlockSpec((CHUNK, LANE), lambda i: (i, 0)),
                  pl.BlockSpec((1, LANE), lambda i: (0, 0))],
        out_specs=pl.BlockSpec((CHUNK, LANE), lambda i: (i, 0)),
        out_shape=jax.ShapeDtypeStruct((n, LANE), F32),
        compiler_params=_params(("parallel",)),
    )(graw, bias)


HALO = 16


def _qk_conv_kernel(xm_ref, xp_ref, xn_ref, w_ref, o_ref, *, blocks_per_batch, ctx_blocks):
    i = pl.program_id(0)
    j = pl.program_id(1)
    x = xm_ref[...].astype(F32)
    tm = x.shape[0]
    prev_row = xp_ref[...].astype(F32)[HALO - 1:HALO, :]
    next_row = xn_ref[...].astype(F32)[0:1, :]
    ib = i % blocks_per_batch
    at_start = jnp.logical_or(ib == 0, ib == ctx_blocks)
    at_end = jnp.logical_or(ib == ctx_blocks - 1, ib == blocks_per_batch - 1)
    prev_row = jnp.where(at_start, 0.0, prev_row)
    next_row = jnp.where(at_end, 0.0, next_row)
    rows = lax.broadcasted_iota(I32, (tm, 1), 0)
    x_prev = jnp.where(rows == 0, prev_row, pltpu.roll(x, 1, 0))
    x_next = jnp.where(rows == tm - 1, next_row, pltpu.roll(x, tm - 1, 0))
    y = w_ref[0:1, :] * x_prev + w_ref[1:2, :] * x + w_ref[2:3, :] * x_next
    y = y * _sigmoid(y)
    y = y * jnp.where(j >= 2, MLSTM_HD ** -0.5, 1.0)
    o_ref[...] = y.astype(BF16)


def qk_conv(p, conv_w, blocks_per_batch, ctx_blocks):
    n = p.shape[0]
    tc = 512
    nhalo = n // HALO
    per = ROW_TILE // HALO

    def col(j):
        return jnp.where(j < 2, OFF_Q // tc + j, OFF_K // tc + j - 2)

    return pl.pallas_call(
        functools.partial(_qk_conv_kernel, blocks_per_batch=blocks_per_batch, ctx_blocks=ctx_blocks),
        grid=(n // ROW_TILE, 2 * D_MLSTM // tc),
        in_specs=[pl.BlockSpec((ROW_TILE, tc), lambda i, j: (i, col(j))),
                  pl.BlockSpec((HALO, tc), lambda i, j: (jnp.maximum(i * per - 1, 0), col(j))),
                  pl.BlockSpec((HALO, tc), lambda i, j: (jnp.minimum((i + 1) * per, nhalo - 1), col(j))),
                  pl.BlockSpec((3, tc), lambda i, j: (0, j))],
        out_specs=pl.BlockSpec((ROW_TILE, tc), lambda i, j: (i, j)),
        out_shape=jax.ShapeDtypeStruct((n, 2 * D_MLSTM), BF16),
        compiler_params=_params(("parallel", "parallel")),
    )(p, p, p, conv_w)


def _mlstm_kernel(q_ref, k_ref, v_ref, o_ref, gc_ref, gr_ref, nrm_ref, y_ref,
                  hf_ref, hb_ref, c_ref, n_ref, m_ref, *, nchunks, nctx):
    c_ref[...] = jnp.zeros_like(c_ref)
    n_ref[...] = jnp.zeros_like(n_ref)
    m_ref[...] = jnp.zeros_like(m_ref)
    row = lax.broadcasted_iota(I32, (CHUNK, CHUNK), 0)
    col = lax.broadcasted_iota(I32, (CHUNK, CHUNK), 1)
    seen = (row >= col, row <= col)
    h_refs = (hf_ref, hb_ref)

    def step(d, c):
        r0 = pl.multiple_of(c * CHUNK, CHUNK)
        rows = pl.ds(r0, CHUNK)
        q = q_ref[rows, :]
        k = k_ref[rows, :]
        v = v_ref[rows, :]
        gcol = gc_ref[0, rows, :]
        grow = gr_ref[0, c]
        ig_c, b_c = gcol[:, 2 * d:2 * d + 1], gcol[:, 2 * d + 1:2 * d + 2]
        ig_r, b_r = grow[2 * d:2 * d + 1, :], grow[2 * d + 1:2 * d + 2, :]
        b_last = b_c[CHUNK - 1:CHUNK, :] if d == 0 else b_c[0:1, :]
        m = m_ref[d, 0:1, 0:1]
        g_c = b_last - b_c + ig_c
        m_new = jnp.maximum(b_last + m, jnp.max(g_c, axis=0, keepdims=True))
        wk_c = jnp.exp(g_c - m_new)
        decay = jnp.exp(b_last + m - m_new)

        log_w = jnp.where(seen[d], b_c - b_r + ig_r, -jnp.inf)
        inter = b_c + m
        m_t = jnp.maximum(inter, jnp.max(log_w, axis=1, keepdims=True))
        w_prev = jnp.exp(inter - m_t)
        s = lax.dot_general(q, k, (((1,), (1,)), ((), ())), preferred_element_type=F32)
        s = s * jnp.exp(log_w - m_t)
        ct = c_ref[d]
        nvec = n_ref[d]
        num = (jnp.dot(s.astype(BF16), v, preferred_element_type=F32)
               + w_prev * jnp.dot(q, ct.astype(BF16), preferred_element_type=F32))
        qf = q.astype(F32)
        kf = k.astype(F32)
        den = jnp.sum(s, axis=1, keepdims=True) + w_prev * jnp.sum(qf * nvec, axis=1, keepdims=True)
        h_refs[d][rows, :] = num / jnp.maximum(jnp.abs(den), jnp.exp(-m_t))

        wv = (v.astype(F32) * wk_c).astype(BF16)
        c_ref[d] = decay * ct + lax.dot_general(k, wv, (((0,), (0,)), ((), ())),
                                                preferred_element_type=F32)
        n_ref[d] = decay * nvec + jnp.sum(kf * wk_c, axis=0, keepdims=True)
        m_ref[d] = jnp.broadcast_to(m_new, (1, LANE))

    def body(i, carry):
        step(0, i)
        step(1, jnp.where(i < nctx, nctx - 1 - i, nchunks - 1 - (i - nctx)))
        return carry

    lax.fori_loop(0, nchunks, body, 0)

    def finish(c, carry):
        rows = pl.ds(pl.multiple_of(c * CHUNK, CHUNK), CHUNK)
        h = hf_ref[rows, :] + hb_ref[rows, :]
        hn = h * lax.rsqrt(jnp.mean(h * h, axis=-1, keepdims=True) + EPS) * nrm_ref[...]
        y_ref[rows, :] = (_sigmoid(o_ref[rows, :].astype(F32)) * hn).astype(BF16)
        return carry

    lax.fori_loop(0, nchunks, finish, 0)


def mlstm_mixer(p, qk, gcol, grow, mlstm_norm, batch, tb, nctx):
    n = p.shape[0]
    nchunks = tb // CHUNK
    hd = MLSTM_HD
    return pl.pallas_call(
        functools.partial(_mlstm_kernel, nchunks=nchunks, nctx=nctx),
        grid=(batch, MLSTM_HEADS),
        in_specs=[pl.BlockSpec((tb, hd), lambda b, h: (b, h)),
                  pl.BlockSpec((tb, hd), lambda b, h: (b, MLSTM_HEADS + h)),
                  pl.BlockSpec((tb, hd), lambda b, h: (b, OFF_VM // hd + h)),
                  pl.BlockSpec((tb, hd), lambda b, h: (b, OFF_O // hd + h)),
                  pl.BlockSpec((1, tb, 4), lambda b, h: (h, b, 0)),
                  pl.BlockSpec((1, nchunks, 4, CHUNK), lambda b, h: (h, b, 0, 0)),
                  pl.BlockSpec((1, hd), lambda b, h: (0, h))],
        out_specs=pl.BlockSpec((tb, hd), lambda b, h: (b, h)),
        out_shape=jax.ShapeDtypeStruct((n, D_MLSTM), BF16),
        scratch_shapes=[pltpu.VMEM((tb, hd), F32), pltpu.VMEM((tb, hd), F32),
                        pltpu.VMEM((2, hd, hd), F32), pltpu.VMEM((2, 1, hd), F32),
                        pltpu.VMEM((2, 1, LANE), F32)],
        compiler_params=_params(("parallel", "parallel")),
    )(qk, qk, p, p, gcol, grow, mlstm_norm)


def _pool_kernel(xfull_ref, xrow_ref, w_ref, sc_ref, y_ref, m_ref, inv_ref, *, tb, ctx, seq):
    g = pl.program_id(0)
    i = pl.program_id(1)
    b = pl.program_id(2)
    tm = xrow_ref.shape[0]
    half = jnp.left_shift(1, g)
    win = 2 * half

    def grid_coords(pos):
        is_ctx = pos < ctx
        lat = pos - ctx
        r = jnp.where(is_ctx, 0, jnp.right_shift(lat, GRID_W_LOG2))
        c = jnp.where(is_ctx, pos, jnp.bitwise_and(lat, GRID_W - 1))
        return is_ctx, r, c

    band = m_ref.shape[1]
    first_tile = jnp.clip(i - POOL_HALO_TILES, 0, (tb - band) // tm)
    s0 = pl.multiple_of(first_tile * tm, tm)

    @pl.when(b == 0)
    def _build_window_matrix():
        t = i * tm + lax.broadcasted_iota(I32, (tm, 1), 0)
        s = s0 + lax.broadcasted_iota(I32, (1, band), 1)
        ctx_t, r_t, c_t = grid_coords(t)
        ctx_s, r_s, c_s = grid_coords(s)
        dr = r_s - r_t + half
        dc = c_s - c_t + half
        inside = (ctx_t == ctx_s) & (dr >= 0) & (dr < win) & (dc >= 0) & (dc < win)
        m_ref[...] = jnp.where(inside, 1.0, 0.0).astype(BF16)
        nrows = jnp.where(ctx_t, 1, seq // GRID_W)
        ncols = jnp.where(ctx_t, ctx, GRID_W)
        cnt_r = jnp.minimum(r_t - half + win, nrows) - jnp.maximum(r_t - half, 0)
        cnt_c = jnp.minimum(c_t - half + win, ncols) - jnp.maximum(c_t - half, 0)
        inv_ref[...] = 1.0 / (cnt_r * cnt_c).astype(F32)

    tot = jnp.dot(m_ref[...], xfull_ref[pl.ds(s0, band), :], preferred_element_type=F32)
    d = (tot * inv_ref[...] - xrow_ref[...].astype(F32)).astype(BF16)
    y = jnp.dot(d, w_ref[0], preferred_element_type=F32) * sc_ref[0]
    y_ref[...] = y.astype(BF16)


def pool_mixer(p, pool_w, pool_scale, batch, tb, ctx, seq):
    n = p.shape[0]
    nrb = tb // ROW_TILE
    c0 = OFF_P // POOL_GC
    band = min(2 * POOL_HALO_TILES + 1, nrb) * ROW_TILE
    return pl.pallas_call(
        functools.partial(_pool_kernel, tb=tb, ctx=ctx, seq=seq),
        grid=(POOL_GROUPS, nrb, batch),
        in_specs=[pl.BlockSpec((tb, POOL_GC), lambda g, i, b: (b, c0 + g)),
                  pl.BlockSpec((ROW_TILE, POOL_GC), lambda g, i, b: (b * nrb + i, c0 + g)),
                  pl.BlockSpec((1, POOL_GC, POOL_GC), lambda g, i, b: (g, 0, 0)),
                  pl.BlockSpec((1, 1, POOL_GC), lambda g, i, b: (g, 0, 0))],
        out_specs=pl.BlockSpec((ROW_TILE, POOL_GC), lambda g, i, b: (b * nrb + i, g)),
        out_shape=jax.ShapeDtypeStruct((n, D_POOL), BF16),
        scratch_shapes=[pltpu.VMEM((ROW_TILE, band), BF16), pltpu.VMEM((ROW_TILE, 1), F32)],
        compiler_params=_params(("arbitrary", "arbitrary", "arbitrary")),
    )(p, p, pool_w, pool_scale)


def _out_proj_kernel(x_ref, ya_ref, yb_ref, yc_ref, mod_ref, w_ref, o_ref):
    ya, yb, yc = ya_ref[...], yb_ref[...], yc_ref[...]
    k1 = D_SGU
    k2 = D_SGU + D_MLSTM
    for j in range(D_MODEL // 256):
        sl = slice(j * 256, (j + 1) * 256)
        acc = jnp.dot(ya, w_ref[0:k1, sl], preferred_element_type=F32)
        acc += jnp.dot(yb, w_ref[k1:k2, sl], preferred_element_type=F32)
        acc += jnp.dot(yc, w_ref[k2:D_MODEL, sl], preferred_element_type=F32)
        o_ref[:, sl] = x_ref[:, sl] + mod_ref[0, 2:3, sl] * acc


def out_proj(x, ya, yb, yc, mod, w_out, mod_row):
    n, d = x.shape
    return pl.pallas_call(
        _out_proj_kernel,
        grid=(n // ROW_TILE,),
        in_specs=[pl.BlockSpec((ROW_TILE, d), lambda i: (i, 0)),
                  pl.BlockSpec((ROW_TILE, D_SGU), lambda i: (i, 0)),
                  pl.BlockSpec((ROW_TILE, D_MLSTM), lambda i: (i, 0)),
                  pl.BlockSpec((ROW_TILE, D_POOL), lambda i: (i, 0)),
                  pl.BlockSpec((1, 6, d), lambda i: (mod_row(i), 0, 0)),
                  pl.BlockSpec((d, d), lambda i: (0, 0))],
        out_specs=pl.BlockSpec((ROW_TILE, d), lambda i: (i, 0)),
        out_shape=jax.ShapeDtypeStruct((n, d), F32),
        compiler_params=_params(("parallel",)),
    )(x, ya, yb, yc, mod, w_out)


def _topk_rows(vals, k):
    nrows = vals.shape[0]
    iota = lax.broadcasted_iota(I32, vals.shape, 0)
    out_v, out_i = [], []
    for _ in range(k):
        mx = jnp.max(vals, axis=0, keepdims=True)
        am = jnp.min(jnp.where(vals == mx, iota, nrows), axis=0, keepdims=True)
        out_v.append(mx)
        out_i.append(am)
        vals = jnp.where(iota == am, -jnp.inf, vals)
    return out_v, out_i


def _peer_route_kernel(x_ref, mod_ref, g_ref, wq_ref, keys_ref, hf_ref, e_ref, gate_ref,
                       hb_ref, cand_ref, cid_ref):
    h = pl.program_id(1)

    @pl.when(h == 0)
    def _norm():
        hf = _norm_mod(x_ref[...], g_ref[...], mod_ref[0, 3:4, :], mod_ref[0, 4:5, :])
        hf_ref[...] = hf
        hb_ref[...] = hf.astype(BF16)

    q = jnp.dot(hb_ref[...], wq_ref[...], preferred_element_type=F32).astype(BF16)
    half = PEER_DK // 2
    nt = (((1,), (1,)), ((), ()))
    scores1 = lax.dot_general(keys_ref[0], q[:, :half], nt, preferred_element_type=F32)
    scores2 = lax.dot_general(keys_ref[1], q[:, half:], nt, preferred_element_type=F32)
    npad = CAND_ROWS - len(CAND_PAIRS)
    for c in range(q.shape[0] // LANE):
        lanes = slice(c * LANE, (c + 1) * LANE)
        s1, i1 = _topk_rows(scores1[:, lanes], PEER_TOPK)
        s2, i2 = _topk_rows(scores2[:, lanes], PEER_TOPK)
        for r, (a, b) in enumerate(CAND_PAIRS):
            cand_ref[r:r + 1, lanes] = s1[a] + s2[b]
            cid_ref[r:r + 1, lanes] = i1[a] * N_KEYS + i2[b]
        cand_ref[len(CAND_PAIRS):, lanes] = jnp.full((npad, LANE), -jnp.inf, F32)
        cid_ref[len(CAND_PAIRS):, lanes] = jnp.zeros((npad, LANE), I32)
        sc, slot = _topk_rows(cand_ref[:, lanes], PEER_TOPK)
        cid = cid_ref[:, lanes]
        riota = lax.broadcasted_iota(I32, cid.shape, 0)
        ex = [jnp.exp(v - sc[0]) for v in sc]
        tot = ex[0]
        for v in ex[1:]:
            tot = tot + v
        for j in range(PEER_TOPK):
            e_ref[0, j:j + 1, lanes] = jnp.sum(jnp.where(riota == slot[j], cid, 0), axis=0, keepdims=True)
            gate_ref[0, j:j + 1, lanes] = ex[j] / tot


def peer_route(x, mod, norm_g, wq, keys, mod_row, nblocks, blk):
    n, d = x.shape
    tt = ROW_TILE
    return pl.pallas_call(
        _peer_route_kernel,
        grid=(nblocks, PEER_HEADS),
        in_specs=[pl.BlockSpec((tt, d), lambda i, h: (blk(i), 0)),
                  pl.BlockSpec((1, 6, d), lambda i, h: (mod_row(blk(i)), 0, 0)),
                  pl.BlockSpec((1, d), lambda i, h: (0, 0)),
                  pl.BlockSpec((d, PEER_DK), lambda i, h: (0, h)),
                  pl.BlockSpec((2, N_KEYS, PEER_DK // 2), lambda i, h: (0, 0, 0))],
        out_specs=[pl.BlockSpec((tt, d), lambda i, h: (blk(i), 0)),
                   pl.BlockSpec((1, PEER_TOPK, tt), lambda i, h: (h, 0, blk(i))),
                   pl.BlockSpec((1, PEER_TOPK, tt), lambda i, h: (h, 0, blk(i)))],
        out_shape=[jax.ShapeDtypeStruct((n, d), F32),
                   jax.ShapeDtypeStruct((PEER_HEADS, PEER_TOPK, n), I32),
                   jax.ShapeDtypeStruct((PEER_HEADS, PEER_TOPK, n), F32)],
        scratch_shapes=[pltpu.VMEM((tt, d), BF16), pltpu.VMEM((CAND_ROWS, tt), F32),
                        pltpu.VMEM((CAND_ROWS, tt), I32)],
        compiler_params=_params(("parallel", "arbitrary")),
    )(x, mod, norm_g, wq, keys)


ISSUE_UNROLL = 8
SUBLANES = 8


def _transpose8(vs, sub):
    vs = list(vs)
    for d in (4, 2, 1):
        keep = jnp.bitwise_and(sub, d) == 0
        nxt = list(vs)
        for i in range(SUBLANES):
            if i & d == 0:
                a, b = vs[i], vs[i + d]
                nxt[i] = jnp.where(keep, a, pltpu.roll(b, d, 0))
                nxt[i + d] = jnp.where(keep, pltpu.roll(a, SUBLANES - d, 0), b)
        vs = nxt
    return vs


def _unpack_pair(words):
    lo = pltpu.bitcast(jnp.left_shift(words, 16), F32)
    hi = pltpu.bitcast(jnp.bitwise_and(words, jnp.uint32(0xFFFF0000)), F32)
    return lo, hi


def _peer_gather_kernel(ids_ref, id1_ref, idn_ref, hf_ref, gt_ref, xn_ref, mod_ref, tab_hbm, o_ref,
                        tbuf, wbuf, sem, *, ngroups):
    i = pl.program_id(0)
    slot = lax.rem(i, GATHER_SLOTS)
    nslot = lax.rem(i + GATHER_SLOTS - 1, GATHER_SLOTS)
    nrows = GATHER_TOKENS * PEER_PAIRS

    def start_row(ids, r, s, priority):
        pltpu.make_async_copy(tab_hbm.at[ids[0, 0, r]], tbuf.at[s, r], sem.at[s]).start(priority=priority)

    def wait_slot(s):
        pltpu.make_async_copy(tab_hbm.at[pl.ds(0, nrows)], tbuf.at[s], sem.at[s]).wait()

    @pl.when(i == 0)
    def _first_groups():
        def body(g, carry):
            for j in range(ISSUE_UNROLL):
                start_row(ids_ref, g * ISSUE_UNROLL + j, 0, j % 2)
                start_row(id1_ref, g * ISSUE_UNROLL + j, 1, j % 2)
            return carry
        lax.fori_loop(0, nrows // ISSUE_UNROLL, body, 0)

    wait_slot(slot)

    sub = lax.broadcasted_iota(I32, (SUBLANES, LANE), 0)
    sel_shape = (SUBLANES, SUBLANES * SUBLANES)
    sel = jnp.where(lax.broadcasted_iota(I32, sel_shape, 1) // SUBLANES
                    == lax.broadcasted_iota(I32, sel_shape, 0), 1.0, 0.0).astype(BF16)
    ngrp = PEER_PAIRS // SUBLANES
    early = 6
    late = SUBLANES - early
    def chunks(ref, first):
        return [ref[:, (first + c) * LANE:(first + c + 1) * LANE] for c in range(SUBLANES)]
    xs_lo = _transpose8(chunks(hf_ref, 0), sub)
    xs_hi = _transpose8(chunks(hf_ref, SUBLANES), sub)
    out_lo, out_hi = [], []
    for t in range(GATHER_TOKENS):
        x_lo = xs_lo[t]
        x_hi = xs_hi[t]
        base = t * PEER_PAIRS
        for j in range(ngrp):
            for q in range(early):
                start_row(idn_ref, base + j * early + q, nslot, q % 2)
            r = []
            for p in range(SUBLANES):
                lo, hi = _unpack_pair(tbuf[slot, base + j * SUBLANES + p, 0])
                r.append(lo * x_lo + hi * x_hi)
            y = jnp.dot(sel, jnp.concatenate(r, axis=0).astype(BF16), preferred_element_type=F32)
            rows = slice(j * SUBLANES, (j + 1) * SUBLANES)
            a = jnp.sum(y, axis=1, keepdims=True)
            w = gt_ref[0, rows, t:t + 1] * _gelu(a)
            wbuf[rows, :] = jnp.broadcast_to(w, (SUBLANES, LANE))
        nacc = 2
        acc_lo = [jnp.zeros((SUBLANES, LANE), F32) for _ in range(nacc)]
        acc_hi = [jnp.zeros((SUBLANES, LANE), F32) for _ in range(nacc)]
        late0 = ngrp * early
        for k in range(PEER_PAIRS):
            if k % SUBLANES == 0:
                for q in range(late0 + (k // SUBLANES) * late, late0 + (k // SUBLANES + 1) * late):
                    start_row(idn_ref, base + q, nslot, q % 2)
            lo, hi = _unpack_pair(tbuf[slot, base + k, 1])
            wk = wbuf[k:k + 1, :]
            acc_lo[k % nacc] = acc_lo[k % nacc] + lo * wk
            acc_hi[k % nacc] = acc_hi[k % nacc] + hi * wk
        out_lo.append(acc_lo[0] + acc_lo[1])
        out_hi.append(acc_hi[0] + acc_hi[1])

    for first, outs in ((0, out_lo), (SUBLANES, out_hi)):
        for c, rows_c in enumerate(_transpose8(outs, sub)):
            cols = slice((first + c) * LANE, (first + c + 1) * LANE)
            o_ref[:, cols] = xn_ref[:, cols] + mod_ref[0, 5:6, cols] * rows_c

    @pl.when(i == ngroups - 1)
    def _drain():
        wait_slot(lax.rem(i + 1, GATHER_SLOTS))
        wait_slot(nslot)


PACK_ROWS = 256


def _pack_kernel(u_ref, v_ref, o_ref):
    sub = lax.broadcasted_iota(I32, (SUBLANES, LANE), 0)
    half = D_MODEL // 2
    for g in range(PACK_ROWS // SUBLANES):
        rows = slice(g * SUBLANES, (g + 1) * SUBLANES)
        for which, ref in enumerate((u_ref, v_ref)):
            bits = pltpu.bitcast(ref[0, rows, :], jnp.uint32)
            rnd = bits + jnp.uint32(0x7FFF) + jnp.bitwise_and(jnp.right_shift(bits, 16), jnp.uint32(1))
            words = jnp.bitwise_or(jnp.right_shift(rnd[:, :half], 16),
                                   jnp.bitwise_and(rnd[:, half:], jnp.uint32(0xFFFF0000)))
            tiles = _transpose8([words[:, c * LANE:(c + 1) * LANE] for c in range(SUBLANES)], sub)
            for t in range(SUBLANES):
                o_ref[g * SUBLANES + t, which] = tiles[t]


def pack_expert_table(u_tabs, v_tabs, layer):
    _, nexp, d = u_tabs.shape
    in_spec = pl.BlockSpec((1, PACK_ROWS, d), lambda i: (layer, i, 0))
    return pl.pallas_call(
        _pack_kernel,
        grid=(nexp // PACK_ROWS,),
        in_specs=[in_spec, in_spec],
        out_specs=pl.BlockSpec((PACK_ROWS, 2, SUBLANES, LANE), lambda i: (i, 0, 0, 0)),
        out_shape=jax.ShapeDtypeStruct((nexp, 2, SUBLANES, LANE), jnp.uint32),
        compiler_params=_params(("parallel",)),
    )(u_tabs, v_tabs)


def peer_gather(ids, hf, gates_t, xn, mod, table, mod_row_group, ngroups, grp):
    n, d = xn.shape
    gt = GATHER_TOKENS
    assert ngroups >= GATHER_SLOTS - 1 and gt == SUBLANES
    nrows = gt * PEER_PAIRS
    tok_spec = pl.BlockSpec((gt, d), lambda i: (grp(i), 0))

    def ids_spec(ahead):
        return pl.BlockSpec((1, 1, nrows), lambda i: (grp(jnp.minimum(i + ahead, ngroups - 1)), 0, 0),
                            memory_space=pltpu.SMEM)

    return pl.pallas_call(
        functools.partial(_peer_gather_kernel, ngroups=ngroups),
        grid=(ngroups,),
        in_specs=[ids_spec(0), ids_spec(1), ids_spec(GATHER_SLOTS - 1),
                  tok_spec,
                  pl.BlockSpec((1, PEER_PAIRS, gt), lambda i: (grp(i), 0, 0)),
                  tok_spec,
                  pl.BlockSpec((1, 6, d), lambda i: (mod_row_group(grp(i)), 0, 0)),
                  pl.BlockSpec(memory_space=pl.ANY)],
        out_specs=tok_spec,
        out_shape=jax.ShapeDtypeStruct((n, d), F32),
        scratch_shapes=[pltpu.VMEM((GATHER_SLOTS, nrows, 2, SUBLANES, LANE), jnp.uint32),
                        pltpu.VMEM((PEER_PAIRS, LANE), F32),
                        pltpu.SemaphoreType.DMA((GATHER_SLOTS,))],
        compiler_params=_params(("arbitrary",)),
    )(ids, ids, ids, hf, gates_t, xn, mod, table)


def _final_norm_kernel(x_ref, g_ref, o_ref):
    x = x_ref[...]
    o_ref[0] = x * lax.rsqrt(jnp.mean(x * x, axis=-1, keepdims=True) + EPS) * g_ref[...]


def final_norm(x, g, batch, seq, ctx):
    n, d = x.shape
    tb = ctx + seq
    nrb = tb // ROW_TILE
    cb = ctx // ROW_TILE
    return pl.pallas_call(
        _final_norm_kernel,
        grid=(batch, seq // ROW_TILE),
        in_specs=[pl.BlockSpec((ROW_TILE, d), lambda b, i: (b * nrb + cb + i, 0)),
                  pl.BlockSpec((1, d), lambda b, i: (0, 0))],
        out_specs=pl.BlockSpec((1, ROW_TILE, d), lambda b, i: (b, i, 0)),
        out_shape=jax.ShapeDtypeStruct((batch, seq, d), F32),
        compiler_params=_params(("parallel", "parallel")),
    )(x, g)


def kernel(x, c, ctx, c_ctx, ada_w, ada_b, norm_mix, norm_ffn, w_in, b_gate, sgu_norm, sgu_w, sgu_b,
           qk_conv_w, mlstm_norm, pool_w, pool_scale, w_out, peer_wq, peer_keys, peer_u, peer_v,
           norm_final):
    batch, seq, d = x.shape
    ctx_len = ctx.shape[1]
    depth = ada_w.shape[0]
    tb = ctx_len + seq
    n = batch * tb
    assert d == D_MODEL and w_in.shape[2] == OFF_G + N_GATES
    assert ctx_len % ROW_TILE == 0 and seq % ROW_TILE == 0 and seq % GRID_W == 0
    assert n % GATHER_TOKENS == 0 and ROW_TILE % GATHER_TOKENS == 0
    blocks_per_batch = tb // ROW_TILE
    ctx_blocks = ctx_len // ROW_TILE
    nchunks = tb // CHUNK

    def mod_row(i):
        return jnp.where(i % blocks_per_batch < ctx_blocks, batch, i // blocks_per_batch)

    groups_per_block = ROW_TILE // GATHER_TOKENS
    lat_blocks = seq // ROW_TILE

    def mod_row_group(i):
        return mod_row(i // groups_per_block)

    def lat_blk(j):
        return (j // lat_blocks) * blocks_per_batch + ctx_blocks + j % lat_blocks

    cond_rows = -(-(batch + 1) // 8) * 8
    cond = jnp.zeros((cond_rows, d), F32).at[:batch].set(c).at[batch].set(c_ctx)
    mods = ada_modulation_all(cond, ada_w, ada_b).reshape(depth, cond_rows, 6, d)

    xs = jnp.concatenate([ctx, x], axis=1).reshape(n, d)
    lane_pad = LANE - N_GATES
    for l in range(depth):
        mod = mods[l]
        w_main = w_in[l, :, :OFF_G].astype(BF16)
        w_gate = jnp.pad(w_in[l, :, OFF_G:], ((0, 0), (0, lane_pad))).astype(BF16)
        p, graw = in_proj(xs, mod, norm_mix[l][None], w_main, w_gate, mod_row)

        y_a = sgu_mixer(p, sgu_norm[l][None], sgu_w[l].astype(BF16), sgu_b[l].T)

        gp = gate_prep(graw[0], jnp.pad(b_gate[l], (0, lane_pad))[None])
        gp = gp[:, :N_GATES].reshape(n, 2, 2, MLSTM_HEADS)
        gcol = gp.transpose(3, 0, 1, 2).reshape(MLSTM_HEADS, n, 4)
        grow = gcol.reshape(MLSTM_HEADS, n // CHUNK, CHUNK, 4).transpose(0, 1, 3, 2)
        qk = qk_conv(p, qk_conv_w[l], blocks_per_batch, ctx_blocks)
        y_b = mlstm_mixer(p, qk, gcol, grow, mlstm_norm[l][None], batch, tb, ctx_len // CHUNK)

        y_c = pool_mixer(p, pool_w[l].astype(BF16), pool_scale[l].reshape(POOL_GROUPS, 1, POOL_GC),
                         batch, tb, ctx_len, seq)

        xn = out_proj(xs, y_a, y_b, y_c, mod, w_out[l].astype(BF16), mod_row)

        if l == depth - 1:
            nblocks, blk = batch * lat_blocks, lat_blk

            def grp(g):
                return lat_blk(g // groups_per_block) * groups_per_block + g % groups_per_block
        else:
            nblocks, blk = n // ROW_TILE, lambda j: j

            def grp(g):
                return g
        hf, e, gates = peer_route(xn, mod, norm_ffn[l][None], peer_wq[l].astype(BF16),
                                  peer_keys[l].astype(BF16), mod_row, nblocks, blk)
        ids = e.reshape(PEER_PAIRS, n).T.reshape(n // GATHER_TOKENS, 1, GATHER_TOKENS * PEER_PAIRS)
        gates_t = gates.reshape(PEER_PAIRS, n // GATHER_TOKENS, GATHER_TOKENS).transpose(1, 0, 2)
        xs = peer_gather(ids, hf, gates_t, xn, mod, pack_expert_table(peer_u, peer_v, l), mod_row_group,
                         nblocks * groups_per_block, grp)

    return final_norm(xs, norm_final[None], batch, seq, ctx_len)
```

```python
import functools

import jax
import jax.numpy as jnp
from jax import lax
from jax.experimental import pallas as pl
from jax.experimental.pallas import tpu as pltpu

F32 = jnp.float32
BF16 = jnp.bfloat16
I32 = jnp.int32

EPS = 1e-6
GRID_W = 64
GRID_W_LOG2 = 6
D_MODEL = 2048
D_SGU = D_MODEL // 4
SGU_HEADS = 4
SGU_HD = D_SGU // SGU_HEADS
CHUNK = 128
D_MLSTM = D_MODEL // 2
MLSTM_HEADS = 4
MLSTM_HD = D_MLSTM // MLSTM_HEADS
D_POOL = D_MODEL // 4
POOL_GROUPS = 4
POOL_GC = D_POOL // POOL_GROUPS
N_GATES = 2 * 2 * MLSTM_HEADS
OFF_U = 0
OFF_V = OFF_U + D_SGU
OFF_P = OFF_V + D_SGU
OFF_Q = OFF_P + D_POOL
OFF_O = OFF_Q + D_MLSTM
OFF_K = OFF_O + D_MLSTM
OFF_VM = OFF_K + D_MLSTM
OFF_G = OFF_VM + D_MLSTM
N_KEYS = 128
PEER_HEADS = 8
PEER_TOPK = 16
PEER_DK = 256
PEER_PAIRS = PEER_HEADS * PEER_TOPK

LANE = 128
ROW_TILE = 256
POOL_HALO_TILES = -(-(GRID_W << (POOL_GROUPS - 1)) // ROW_TILE)
GATHER_TOKENS = 8
GATHER_SLOTS = 3
VMEM_LIMIT = 56 * 1024 * 1024

CAND_PAIRS = [(a, b) for a in range(PEER_TOPK) for b in range(PEER_TOPK)
              if (a + 1) * (b + 1) <= PEER_TOPK]
CAND_ROWS = 56


def _params(sem, vmem=VMEM_LIMIT):
    return pltpu.CompilerParams(dimension_semantics=sem, vmem_limit_bytes=vmem)


def _norm_mod(x, g, shift, scale):
    ms = jnp.mean(x * x, axis=-1, keepdims=True)
    y = x * lax.rsqrt(ms + EPS)
    return (y * g) * (1.0 + scale) + shift


def _gelu(x):
    return jax.nn.gelu(x, approximate=True)


def _sigmoid(x):
    return 1.0 / (1.0 + jnp.exp(-x))


def _ada_kernel(cond_ref, w_ref, b_ref, o_ref):
    a = cond_ref[...]
    a = (a * _sigmoid(a)).astype(BF16)
    o_ref[0] = jnp.dot(a, w_ref[0].astype(BF16), preferred_element_type=F32) + b_ref[0]


def ada_modulation_all(cond, ada_w, ada_b):
    depth, d, d6 = ada_w.shape
    rows = cond.shape[0]
    tn = 1024
    return pl.pallas_call(
        _ada_kernel,
        grid=(depth, d6 // tn),
        in_specs=[pl.BlockSpec((rows, d), lambda l, j: (0, 0)),
                  pl.BlockSpec((1, d, tn), lambda l, j: (l, 0, j)),
                  pl.BlockSpec((1, 1, tn), lambda l, j: (l, 0, j))],
        out_specs=pl.BlockSpec((1, rows, tn), lambda l, j: (l, 0, j)),
        out_shape=jax.ShapeDtypeStruct((depth, rows, d6), F32),
        compiler_params=_params(("parallel", "parallel")),
    )(cond, ada_w, ada_b.reshape(depth, 1, d6))


def _in_proj_kernel(x_ref, mod_ref, g_ref, w_ref, wg_ref, p_ref, gate_ref, *, ncols):
    h = _norm_mod(x_ref[...], g_ref[...], mod_ref[0, 0:1, :], mod_ref[0, 1:2, :]).astype(BF16)
    for j in range(ncols // 256):
        sl = slice(j * 256, (j + 1) * 256)
        p_ref[:, sl] = jnp.dot(h, w_ref[:, sl], preferred_element_type=F32).astype(BF16)
    gate_ref[0] = jnp.dot(h, wg_ref[...], preferred_element_type=F32)


def in_proj(x, mod, norm_g, w_main, w_gate, mod_row):
    n, d = x.shape
    ncol_blocks = 2
    ncols = OFF_G // ncol_blocks
    return pl.pallas_call(
        functools.partial(_in_proj_kernel, ncols=ncols),
        grid=(ncol_blocks, n // ROW_TILE),
        in_specs=[pl.BlockSpec((ROW_TILE, d), lambda c, i: (i, 0)),
                  pl.BlockSpec((1, 6, d), lambda c, i: (mod_row(i), 0, 0)),
                  pl.BlockSpec((1, d), lambda c, i: (0, 0)),
                  pl.BlockSpec((d, ncols), lambda c, i: (0, c)),
                  pl.BlockSpec((d, LANE), lambda c, i: (0, 0))],
        out_specs=[pl.BlockSpec((ROW_TILE, ncols), lambda c, i: (i, c)),
                   pl.BlockSpec((1, ROW_TILE, LANE), lambda c, i: (c, i, 0))],
        out_shape=[jax.ShapeDtypeStruct((n, OFF_G), BF16),
                   jax.ShapeDtypeStruct((ncol_blocks, n, LANE), F32)],
        compiler_params=_params(("arbitrary", "arbitrary")),
    )(x, mod, norm_g, w_main, w_gate)


def _sgu_kernel(u_ref, v_ref, g_ref, ws_ref, bs_ref, y_ref):
    u = _gelu(u_ref[...].astype(F32))
    v = _gelu(v_ref[...].astype(F32))
    ms = jnp.mean(v * v, axis=-1, keepdims=True)
    v = (v * lax.rsqrt(ms + EPS) * g_ref[...]).astype(BF16)
    for c in range(ROW_TILE // CHUNK):
        rows = slice(c * CHUNK, (c + 1) * CHUNK)
        for h in range(SGU_HEADS):
            cols = slice(h * SGU_HD, (h + 1) * SGU_HD)
            mixed = jnp.dot(ws_ref[h], v[rows, cols], preferred_element_type=F32) + bs_ref[:, h:h + 1]
            y_ref[rows, cols] = (u[rows, cols] * mixed).astype(BF16)


def sgu_mixer(p, sgu_norm, sgu_w, sgu_bt):
    n = p.shape[0]
    return pl.pallas_call(
        _sgu_kernel,
        grid=(n // ROW_TILE,),
        in_specs=[pl.BlockSpec((ROW_TILE, D_SGU), lambda i: (i, OFF_U // D_SGU)),
                  pl.BlockSpec((ROW_TILE, D_SGU), lambda i: (i, OFF_V // D_SGU)),
                  pl.BlockSpec((1, D_SGU), lambda i: (0, 0)),
                  pl.BlockSpec((SGU_HEADS, CHUNK, CHUNK), lambda i: (0, 0, 0)),
                  pl.BlockSpec((CHUNK, SGU_HEADS), lambda i: (0, 0))],
        out_specs=pl.BlockSpec((ROW_TILE, D_SGU), lambda i: (i, 0)),
        out_shape=jax.ShapeDtypeStruct((n, D_SGU), BF16),
        compiler_params=_params(("parallel",)),
    )(p, p, sgu_norm, sgu_w, sgu_bt)


def _gate_prep_kernel(g_ref, bias_ref, o_ref):
    g = g_ref[...] + bias_ref[...]
    lf = jnp.minimum(g, 0.0) - jnp.log(1.0 + jnp.exp(-jnp.abs(g)))
    row = lax.broadcasted_iota(I32, (CHUNK, CHUNK), 0)
    col = lax.broadcasted_iota(I32, (CHUNK, CHUNK), 1)
    lower = (col <= row).astype(F32)
    upper = (col >= row).astype(F32)
    b_fwd = jnp.dot(lower, lf, preferred_element_type=F32, precision=lax.Precision.HIGHEST)
    b_bwd = jnp.dot(upper, lf, preferred_element_type=F32, precision=lax.Precision.HIGHEST)
    lane = lax.broadcasted_iota(I32, (1, LANE), 1)
    is_input_gate = (lane // MLSTM_HEADS) % 2 == 0
    is_fwd = lane < 2 * MLSTM_HEADS
    o_ref[...] = jnp.where(is_input_gate, g, jnp.where(is_fwd, b_fwd, b_bwd))


def gate_prep(graw, bias):
    n = graw.shape[0]
    return pl.pallas_call(
        _gate_prep_kernel,
        grid=(n // CHUNK,),
        in_specs=[pl.BlockSpec((CHUNK, LANE), lambda i: (i, 0)),
                  pl.BlockSpec((1, LANE), lambda i: (0, 0))],
        out_specs=pl.BlockSpec((CHUNK, LANE), lambda i: (i, 0)),
        out_shape=jax.ShapeDtypeStruct((n, LANE), F32),
        compiler_params=_params(("parallel",)),
    )(graw, bias)


HALO = 16


def _qk_conv_kernel(xm_ref, xp_ref, xn_ref, w_ref, o_ref, *, blocks_per_batch, ctx_blocks):
    i = pl.program_id(0)
    j = pl.program_id(1)
    x = xm_ref[...].astype(F32)
    tm = x.shape[0]
    prev_row = xp_ref[...].astype(F32)[HALO - 1:HALO, :]
    next_row = xn_ref[...].astype(F32)[0:1, :]
    ib = i % blocks_per_batch
    at_start = jnp.logical_or(ib == 0, ib == ctx_blocks)
    at_end = jnp.logical_or(ib == ctx_blocks - 1, ib == blocks_per_batch - 1)
    prev_row = jnp.where(at_start, 0.0, prev_row)
    next_row = jnp.where(at_end, 0.0, next_row)
    rows = lax.broadcasted_iota(I32, (tm, 1), 0)
    x_prev = jnp.where(rows == 0, prev_row, pltpu.roll(x, 1, 0))
    x_next = jnp.where(rows == tm - 1, next_row, pltpu.roll(x, tm - 1, 0))
    y = w_ref[0:1, :] * x_prev + w_ref[1:2, :] * x + w_ref[2:3, :] * x_next
    y = y * _sigmoid(y)
    y = y * jnp.where(j >= 2, MLSTM_HD ** -0.5, 1.0)
    o_ref[...] = y.astype(BF16)


def qk_conv(p, conv_w, blocks_per_batch, ctx_blocks):
    n = p.shape[0]
    tc = 512
    nhalo = n // HALO
    per = ROW_TILE // HALO

    def col(j):
        return jnp.where(j < 2, OFF_Q // tc + j, OFF_K // tc + j - 2)

    return pl.pallas_call(
        functools.partial(_qk_conv_kernel, blocks_per_batch=blocks_per_batch, ctx_blocks=ctx_blocks),
        grid=(n // ROW_TILE, 2 * D_MLSTM // tc),
        in_specs=[pl.BlockSpec((ROW_TILE, tc), lambda i, j: (i, col(j))),
                  pl.BlockSpec((HALO, tc), lambda i, j: (jnp.maximum(i * per - 1, 0), col(j))),
                  pl.BlockSpec((HALO, tc), lambda i, j: (jnp.minimum((i + 1) * per, nhalo - 1), col(j))),
                  pl.BlockSpec((3, tc), lambda i, j: (0, j))],
        out_specs=pl.BlockSpec((ROW_TILE, tc), lambda i, j: (i, j)),
        out_shape=jax.ShapeDtypeStruct((n, 2 * D_MLSTM), BF16),
        compiler_params=_params(("parallel", "parallel")),
    )(p, p, p, conv_w)


def _mlstm_kernel(q_ref, k_ref, v_ref, o_ref, gc_ref, gr_ref, nrm_ref, y_ref,
                  hf_ref, hb_ref, c_ref, n_ref, m_ref, *, nchunks, nctx):
    c_ref[...] = jnp.zeros_like(c_ref)
    n_ref[...] = jnp.zeros_like(n_ref)
    m_ref[...] = jnp.zeros_like(m_ref)
    row = lax.broadcasted_iota(I32, (CHUNK, CHUNK), 0)
    col = lax.broadcasted_iota(I32, (CHUNK, CHUNK), 1)
    seen = (row >= col, row <= col)
    h_refs = (hf_ref, hb_ref)

    def step(d, c):
        r0 = pl.multiple_of(c * CHUNK, CHUNK)
        rows = pl.ds(r0, CHUNK)
        q = q_ref[rows, :]
        k = k_ref[rows, :]
        v = v_ref[rows, :]
        gcol = gc_ref[0, rows, :]
        grow = gr_ref[0, c]
        ig_c, b_c = gcol[:, 2 * d:2 * d + 1], gcol[:, 2 * d + 1:2 * d + 2]
        ig_r, b_r = grow[2 * d:2 * d + 1, :], grow[2 * d + 1:2 * d + 2, :]
        b_last = b_c[CHUNK - 1:CHUNK, :] if d == 0 else b_c[0:1, :]
        m = m_ref[d, 0:1, 0:1]
        g_c = b_last - b_c + ig_c
        m_new = jnp.maximum(b_last + m, jnp.max(g_c, axis=0, keepdims=True))
        wk_c = jnp.exp(g_c - m_new)
        decay = jnp.exp(b_last + m - m_new)

        log_w = jnp.where(seen[d], b_c - b_r + ig_r, -jnp.inf)
        inter = b_c + m
        m_t = jnp.maximum(inter, jnp.max(log_w, axis=1, keepdims=True))
        w_prev = jnp.exp(inter - m_t)
        s = lax.dot_general(q, k, (((1,), (1,)), ((), ())), preferred_element_type=F32)
        s = s * jnp.exp(log_w - m_t)
        ct = c_ref[d]
        nvec = n_ref[d]
        num = (jnp.dot(s.astype(BF16), v, preferred_element_type=F32)
               + w_prev * jnp.dot(q, ct.astype(BF16), preferred_element_type=F32))
        qf = q.astype(F32)
        kf = k.astype(F32)
        den = jnp.sum(s, axis=1, keepdims=True) + w_prev * jnp.sum(qf * nvec, axis=1, keepdims=True)
        h_refs[d][rows, :] = num / jnp.maximum(jnp.abs(den), jnp.exp(-m_t))

        wv = (v.astype(F32) * wk_c).astype(BF16)
        c_ref[d] = decay * ct + lax.dot_general(k, wv, (((0,), (0,)), ((), ())),
                                                preferred_element_type=F32)
        n_ref[d] = decay * nvec + jnp.sum(kf * wk_c, axis=0, keepdims=True)
        m_ref[d] = jnp.broadcast_to(m_new, (1, LANE))

    def body(i, carry):
        step(0, i)
        step(1, jnp.where(i < nctx, nctx - 1 - i, nchunks - 1 - (i - nctx)))
        return carry

    lax.fori_loop(0, nchunks, body, 0)

    def finish(c, carry):
        rows = pl.ds(pl.multiple_of(c * CHUNK, CHUNK), CHUNK)
        h = hf_ref[rows, :] + hb_ref[rows, :]
        hn = h * lax.rsqrt(jnp.mean(h * h, axis=-1, keepdims=True) + EPS) * nrm_ref[...]
        y_ref[rows, :] = (_sigmoid(o_ref[rows, :].astype(F32)) * hn).astype(BF16)
        return carry

    lax.fori_loop(0, nchunks, finish, 0)


def mlstm_mixer(p, qk, gcol, grow, mlstm_norm, batch, tb, nctx):
    n = p.shape[0]
    nchunks = tb // CHUNK
    hd = MLSTM_HD
    return pl.pallas_call(
        functools.partial(_mlstm_kernel, nchunks=nchunks, nctx=nctx),
        grid=(batch, MLSTM_HEADS),
        in_specs=[pl.BlockSpec((tb, hd), lambda b, h: (b, h)),
                  pl.BlockSpec((tb, hd), lambda b, h: (b, MLSTM_HEADS + h)),
                  pl.BlockSpec((tb, hd), lambda b, h: (b, OFF_VM // hd + h)),
                  pl.BlockSpec((tb, hd), lambda b, h: (b, OFF_O // hd + h)),
                  pl.BlockSpec((1, tb, 4), lambda b, h: (h, b, 0)),
                  pl.BlockSpec((1, nchunks, 4, CHUNK), lambda b, h: (h, b, 0, 0)),
                  pl.BlockSpec((1, hd), lambda b, h: (0, h))],
        out_specs=pl.BlockSpec((tb, hd), lambda b, h: (b, h)),
        out_shape=jax.ShapeDtypeStruct((n, D_MLSTM), BF16),
        scratch_shapes=[pltpu.VMEM((tb, hd), F32), pltpu.VMEM((tb, hd), F32),
                        pltpu.VMEM((2, hd, hd), F32), pltpu.VMEM((2, 1, hd), F32),
                        pltpu.VMEM((2, 1, LANE), F32)],
        compiler_params=_params(("parallel", "parallel")),
    )(qk, qk, p, p, gcol, grow, mlstm_norm)


def _pool_kernel(xfull_ref, xrow_ref, w_ref, sc_ref, y_ref, m_ref, inv_ref, *, tb, ctx, seq):
    g = pl.program_id(0)
    i = pl.program_id(1)
    b = pl.program_id(2)
    tm = xrow_ref.shape[0]
    half = jnp.left_shift(1, g)
    win = 2 * half

    def grid_coords(pos):
        is_ctx = pos < ctx
        lat = pos - ctx
        r = jnp.where(is_ctx, 0, jnp.right_shift(lat, GRID_W_LOG2))
        c = jnp.where(is_ctx, pos, jnp.bitwise_and(lat, GRID_W - 1))
        return is_ctx, r, c

    band = m_ref.shape[1]
    first_tile = jnp.clip(i - POOL_HALO_TILES, 0, (tb - band) // tm)
    s0 = pl.multiple_of(first_tile * tm, tm)

    @pl.when(b == 0)
    def _build_window_matrix():
        t = i * tm + lax.broadcasted_iota(I32, (tm, 1), 0)
        s = s0 + lax.broadcasted_iota(I32, (1, band), 1)
        ctx_t, r_t, c_t = grid_coords(t)
        ctx_s, r_s, c_s = grid_coords(s)
        dr = r_s - r_t + half
        dc = c_s - c_t + half
        inside = (ctx_t == ctx_s) & (dr >= 0) & (dr < win) & (dc >= 0) & (dc < win)
        m_ref[...] = jnp.where(inside, 1.0, 0.0).astype(BF16)
        nrows = jnp.where(ctx_t, 1, seq // GRID_W)
        ncols = jnp.where(ctx_t, ctx, GRID_W)
        cnt_r = jnp.minimum(r_t - half + win, nrows) - jnp.maximum(r_t - half, 0)
        cnt_c = jnp.minimum(c_t - half + win, ncols) - jnp.maximum(c_t - half, 0)
        inv_ref[...] = 1.0 / (cnt_r * cnt_c).astype(F32)

    tot = jnp.dot(m_ref[...], xfull_ref[pl.ds(s0, band), :], preferred_element_type=F32)
    d = (tot * inv_ref[...] - xrow_ref[...].astype(F32)).astype(BF16)
    y = jnp.dot(d, w_ref[0], preferred_element_type=F32) * sc_ref[0]
    y_ref[...] = y.astype(BF16)


def pool_mixer(p, pool_w, pool_scale, batch, tb, ctx, seq):
    n = p.shape[0]
    nrb = tb // ROW_TILE
    c0 = OFF_P // POOL_GC
    band = min(2 * POOL_HALO_TILES + 1, nrb) * ROW_TILE
    return pl.pallas_call(
        functools.partial(_pool_kernel, tb=tb, ctx=ctx, seq=seq),
        grid=(POOL_GROUPS, nrb, batch),
        in_specs=[pl.BlockSpec((tb, POOL_GC), lambda g, i, b: (b, c0 + g)),
                  pl.BlockSpec((ROW_TILE, POOL_GC), lambda g, i, b: (b * nrb + i, c0 + g)),
                  pl.BlockSpec((1, POOL_GC, POOL_GC), lambda g, i, b: (g, 0, 0)),
                  pl.BlockSpec((1, 1, POOL_GC), lambda g, i, b: (g, 0, 0))],
        out_specs=pl.BlockSpec((ROW_TILE, POOL_GC), lambda g, i, b: (b * nrb + i, g)),
        out_shape=jax.ShapeDtypeStruct((n, D_POOL), BF16),
        scratch_shapes=[pltpu.VMEM((ROW_TILE, band), BF16), pltpu.VMEM((ROW_TILE, 1), F32)],
        compiler_params=_params(("arbitrary", "arbitrary", "arbitrary")),
    )(p, p, pool_w, pool_scale)


def _out_proj_kernel(x_ref, ya_ref, yb_ref, yc_ref, mod_ref, w_ref, o_ref):
    ya, yb, yc = ya_ref[...], yb_ref[...], yc_ref[...]
    k1 = D_SGU
    k2 = D_SGU + D_MLSTM
    for j in range(D_MODEL // 256):
        sl = slice(j * 256, (j + 1) * 256)
        acc = jnp.dot(ya, w_ref[0:k1, sl], preferred_element_type=F32)
        acc += jnp.dot(yb, w_ref[k1:k2, sl], preferred_element_type=F32)
        acc += jnp.dot(yc, w_ref[k2:D_MODEL, sl], preferred_element_type=F32)
        o_ref[:, sl] = x_ref[:, sl] + mod_ref[0, 2:3, sl] * acc


def out_proj(x, ya, yb, yc, mod, w_out, mod_row):
    n, d = x.shape
    return pl.pallas_call(
        _out_proj_kernel,
        grid=(n // ROW_TILE,),
        in_specs=[pl.BlockSpec((ROW_TILE, d), lambda i: (i, 0)),
                  pl.BlockSpec((ROW_TILE, D_SGU), lambda i: (i, 0)),
                  pl.BlockSpec((ROW_TILE, D_MLSTM), lambda i: (i, 0)),
                  pl.BlockSpec((ROW_TILE, D_POOL), lambda i: (i, 0)),
                  pl.BlockSpec((1, 6, d), lambda i: (mod_row(i), 0, 0)),
                  pl.BlockSpec((d, d), lambda i: (0, 0))],
        out_specs=pl.BlockSpec((ROW_TILE, d), lambda i: (i, 0)),
        out_shape=jax.ShapeDtypeStruct((n, d), F32),
        compiler_params=_params(("parallel",)),
    )(x, ya, yb, yc, mod, w_out)


def _topk_rows(vals, k):
    nrows = vals.shape[0]
    iota = lax.broadcasted_iota(I32, vals.shape, 0)
    out_v, out_i = [], []
    for _ in range(k):
        mx = jnp.max(vals, axis=0, keepdims=True)
        am = jnp.min(jnp.where(vals == mx, iota, nrows), axis=0, keepdims=True)
        out_v.append(mx)
        out_i.append(am)
        vals = jnp.where(iota == am, -jnp.inf, vals)
    return out_v, out_i


def _peer_route_kernel(x_ref, mod_ref, g_ref, wq_ref, keys_ref, hf_ref, e_ref, gate_ref,
                       hb_ref, cand_ref, cid_ref):
    h = pl.program_id(1)

    @pl.when(h == 0)
    def _norm():
        hf = _norm_mod(x_ref[...], g_ref[...], mod_ref[0, 3:4, :], mod_ref[0, 4:5, :])
        hf_ref[...] = hf
        hb_ref[...] = hf.astype(BF16)

    q = jnp.dot(hb_ref[...], wq_ref[...], preferred_element_type=F32).astype(BF16)
    half = PEER_DK // 2
    nt = (((1,), (1,)), ((), ()))
    scores1 = lax.dot_general(keys_ref[0], q[:, :half], nt, preferred_element_type=F32)
    scores2 = lax.dot_general(keys_ref[1], q[:, half:], nt, preferred_element_type=F32)
    npad = CAND_ROWS - len(CAND_PAIRS)
    for c in range(q.shape[0] // LANE):
        lanes = slice(c * LANE, (c + 1) * LANE)
        s1, i1 = _topk_rows(scores1[:, lanes], PEER_TOPK)
        s2, i2 = _topk_rows(scores2[:, lanes], PEER_TOPK)
        for r, (a, b) in enumerate(CAND_PAIRS):
            cand_ref[r:r + 1, lanes] = s1[a] + s2[b]
            cid_ref[r:r + 1, lanes] = i1[a] * N_KEYS + i2[b]
        cand_ref[len(CAND_PAIRS):, lanes] = jnp.full((npad, LANE), -jnp.inf, F32)
        cid_ref[len(CAND_PAIRS):, lanes] = jnp.zeros((npad, LANE), I32)
        sc, slot = _topk_rows(cand_ref[:, lanes], PEER_TOPK)
        cid = cid_ref[:, lanes]
        riota = lax.broadcasted_iota(I32, cid.shape, 0)
        ex = [jnp.exp(v - sc[0]) for v in sc]
        tot = ex[0]
        for v in ex[1:]:
            tot = tot + v
        for j in range(PEER_TOPK):
            e_ref[0, j:j + 1, lanes] = jnp.sum(jnp.where(riota == slot[j], cid, 0), axis=0, keepdims=True)
            gate_ref[0, j:j + 1, lanes] = ex[j] / tot


def peer_route(x, mod, norm_g, wq, keys, mod_row, nblocks, blk):
    n, d = x.shape
    tt = ROW_TILE
    return pl.pallas_call(
        _peer_route_kernel,
        grid=(nblocks, PEER_HEADS),
        in_specs=[pl.BlockSpec((tt, d), lambda i, h: (blk(i), 0)),
                  pl.BlockSpec((1, 6, d), lambda i, h: (mod_row(blk(i)), 0, 0)),
                  pl.BlockSpec((1, d), lambda i, h: (0, 0)),
                  pl.BlockSpec((d, PEER_DK), lambda i, h: (0, h)),
                  pl.BlockSpec((2, N_KEYS, PEER_DK // 2), lambda i, h: (0, 0, 0))],
        out_specs=[pl.BlockSpec((tt, d), lambda i, h: (blk(i), 0)),
                   pl.BlockSpec((1, PEER_TOPK, tt), lambda i, h: (h, 0, blk(i))),
                   pl.BlockSpec((1, PEER_TOPK, tt), lambda i, h: (h, 0, blk(i)))],
        out_shape=[jax.ShapeDtypeStruct((n, d), F32),
                   jax.ShapeDtypeStruct((PEER_HEADS, PEER_TOPK, n), I32),
                   jax.ShapeDtypeStruct((PEER_HEADS, PEER_TOPK, n), F32)],
        scratch_shapes=[pltpu.VMEM((tt, d), BF16), pltpu.VMEM((CAND_ROWS, tt), F32),
                        pltpu.VMEM((CAND_ROWS, tt), I32)],
        compiler_params=_params(("parallel", "arbitrary")),
    )(x, mod, norm_g, wq, keys)


ISSUE_UNROLL = 8
SUBLANES = 8


def _transpose8(vs, sub):
    vs = list(vs)
    for d in (4, 2, 1):
        keep = jnp.bitwise_and(sub, d) == 0
        nxt = list(vs)
        for i in range(SUBLANES):
            if i & d == 0:
                a, b = vs[i], vs[i + d]
                nxt[i] = jnp.where(keep, a, pltpu.roll(b, d, 0))
                nxt[i + d] = jnp.where(keep, pltpu.roll(a, SUBLANES - d, 0), b)
        vs = nxt
    return vs


def _unpack_pair(words):
    lo = pltpu.bitcast(jnp.left_shift(words, 16), F32)
    hi = pltpu.bitcast(jnp.bitwise_and(words, jnp.uint32(0xFFFF0000)), F32)
    return lo, hi


def _peer_gather_kernel(ids_ref, id1_ref, idn_ref, hf_ref, gt_ref, xn_ref, mod_ref, tab_hbm, o_ref,
                        tbuf, wbuf, sem, *, ngroups):
    i = pl.program_id(0)
    slot = lax.rem(i, GATHER_SLOTS)
    nslot = lax.rem(i + GATHER_SLOTS - 1, GATHER_SLOTS)
    nrows = GATHER_TOKENS * PEER_PAIRS

    def start_row(ids, r, s, priority):
        pltpu.make_async_copy(tab_hbm.at[ids[0, 0, r]], tbuf.at[s, r], sem.at[s]).start(priority=priority)

    def wait_slot(s):
        pltpu.make_async_copy(tab_hbm.at[pl.ds(0, nrows)], tbuf.at[s], sem.at[s]).wait()

    @pl.when(i == 0)
    def _first_groups():
        def body(g, carry):
            for j in range(ISSUE_UNROLL):
                start_row(ids_ref, g * ISSUE_UNROLL + j, 0, j % 2)
                start_row(id1_ref, g * ISSUE_UNROLL + j, 1, j % 2)
            return carry
        lax.fori_loop(0, nrows // ISSUE_UNROLL, body, 0)

    wait_slot(slot)

    sub = lax.broadcasted_iota(I32, (SUBLANES, LANE), 0)
    sel_shape = (SUBLANES, SUBLANES * SUBLANES)
    sel = jnp.where(lax.broadcasted_iota(I32, sel_shape, 1) // SUBLANES
                    == lax.broadcasted_iota(I32, sel_shape, 0), 1.0, 0.0).astype(BF16)
    ngrp = PEER_PAIRS // SUBLANES
    early = 6
    late = SUBLANES - early
    def chunks(ref, first):
        return [ref[:, (first + c) * LANE:(first + c + 1) * LANE] for c in range(SUBLANES)]
    xs_lo = _transpose8(chunks(hf_ref, 0), sub)
    xs_hi = _transpose8(chunks(hf_ref, SUBLANES), sub)
    out_lo, out_hi = [], []
    for t in range(GATHER_TOKENS):
        x_lo = xs_lo[t]
        x_hi = xs_hi[t]
        base = t * PEER_PAIRS
        for j in range(ngrp):
            for q in range(early):
                start_row(idn_ref, base + j * early + q, nslot, q % 2)
            r = []
            for p in range(SUBLANES):
                lo, hi = _unpack_pair(tbuf[slot, base + j * SUBLANES + p, 0])
                r.append(lo * x_lo + hi * x_hi)
            y = jnp.dot(sel, jnp.concatenate(r, axis=0).astype(BF16), preferred_element_type=F32)
            rows = slice(j * SUBLANES, (j + 1) * SUBLANES)
            a = jnp.sum(y, axis=1, keepdims=True)
            w = gt_ref[0, rows, t:t + 1] * _gelu(a)
            wbuf[rows, :] = jnp.broadcast_to(w, (SUBLANES, LANE))
        nacc = 2
        acc_lo = [jnp.zeros((SUBLANES, LANE), F32) for _ in range(nacc)]
        acc_hi = [jnp.zeros((SUBLANES, LANE), F32) for _ in range(nacc)]
        late0 = ngrp * early
        for k in range(PEER_PAIRS):
            if k % SUBLANES == 0:
                for q in range(late0 + (k // SUBLANES) * late, late0 + (k // SUBLANES + 1) * late):
                    start_row(idn_ref, base + q, nslot, q % 2)
            lo, hi = _unpack_pair(tbuf[slot, base + k, 1])
            wk = wbuf[k:k + 1, :]
            acc_lo[k % nacc] = acc_lo[k % nacc] + lo * wk
            acc_hi[k % nacc] = acc_hi[k % nacc] + hi * wk
        out_lo.append(acc_lo[0] + acc_lo[1])
        out_hi.append(acc_hi[0] + acc_hi[1])

    for first, outs in ((0, out_lo), (SUBLANES, out_hi)):
        for c, rows_c in enumerate(_transpose8(outs, sub)):
            cols = slice((first + c) * LANE, (first + c + 1) * LANE)
            o_ref[:, cols] = xn_ref[:, cols] + mod_ref[0, 5:6, cols] * rows_c

    @pl.when(i == ngroups - 1)
    def _drain():
        wait_slot(lax.rem(i + 1, GATHER_SLOTS))
        wait_slot(nslot)


PACK_ROWS = 256


def _pack_kernel(u_ref, v_ref, o_ref):
    sub = lax.broadcasted_iota(I32, (SUBLANES, LANE), 0)
    half = D_MODEL // 2
    for g in range(PACK_ROWS // SUBLANES):
        rows = slice(g * SUBLANES, (g + 1) * SUBLANES)
        for which, ref in enumerate((u_ref, v_ref)):
            bits = pltpu.bitcast(ref[0, rows, :], jnp.uint32)
            rnd = bits + jnp.uint32(0x7FFF) + jnp.bitwise_and(jnp.right_shift(bits, 16), jnp.uint32(1))
            words = jnp.bitwise_or(jnp.right_shift(rnd[:, :half], 16),
                                   jnp.bitwise_and(rnd[:, half:], jnp.uint32(0xFFFF0000)))
            tiles = _transpose8([words[:, c * LANE:(c + 1) * LANE] for c in range(SUBLANES)], sub)
            for t in range(SUBLANES):
                o_ref[g * SUBLANES + t, which] = tiles[t]


def pack_expert_table(u_tabs, v_tabs, layer):
    _, nexp, d = u_tabs.shape
    in_spec = pl.BlockSpec((1, PACK_ROWS, d), lambda i: (layer, i, 0))
    return pl.pallas_call(
        _pack_kernel,
        grid=(nexp // PACK_ROWS,),
        in_specs=[in_spec, in_spec],
        out_specs=pl.BlockSpec((PACK_ROWS, 2, SUBLANES, LANE), lambda i: (i, 0, 0, 0)),
        out_shape=jax.ShapeDtypeStruct((nexp, 2, SUBLANES, LANE), jnp.uint32),
        compiler_params=_params(("parallel",)),
    )(u_tabs, v_tabs)


def peer_gather(ids, hf, gates_t, xn, mod, table, mod_row_group, ngroups, grp):
    n, d = xn.shape
    gt = GATHER_TOKENS
    assert ngroups >= GATHER_SLOTS - 1 and gt == SUBLANES
    nrows = gt * PEER_PAIRS
    tok_spec = pl.BlockSpec((gt, d), lambda i: (grp(i), 0))

    def ids_spec(ahead):
        return pl.BlockSpec((1, 1, nrows), lambda i: (grp(jnp.minimum(i + ahead, ngroups - 1)), 0, 0),
                            memory_space=pltpu.SMEM)

    return pl.pallas_call(
        functools.partial(_peer_gather_kernel, ngroups=ngroups),
        grid=(ngroups,),
        in_specs=[ids_spec(0), ids_spec(1), ids_spec(GATHER_SLOTS - 1),
                  tok_spec,
                  pl.BlockSpec((1, PEER_PAIRS, gt), lambda i: (grp(i), 0, 0)),
                  tok_spec,
                  pl.BlockSpec((1, 6, d), lambda i: (mod_row_group(i), 0, 0)),
                  pl.BlockSpec(memory_space=pl.ANY)],
        out_specs=tok_spec,
        out_shape=jax.ShapeDtypeStruct((n, d), F32),
        scratch_shapes=[pltpu.VMEM((GATHER_SLOTS, nrows, 2, SUBLANES, LANE), jnp.uint32),
                        pltpu.VMEM((PEER_PAIRS, LANE), F32),
                        pltpu.SemaphoreType.DMA((GATHER_SLOTS,))],
        compiler_params=_params(("arbitrary",)),
    )(ids, ids, ids, hf, gates_t, xn, mod, table)


def _final_norm_kernel(x_ref, g_ref, o_ref):
    x = x_ref[...]
    o_ref[0] = x * lax.rsqrt(jnp.mean(x * x, axis=-1, keepdims=True) + EPS) * g_ref[...]


def final_norm(x, g, batch, seq, ctx):
    n, d = x.shape
    tb = ctx + seq
    nrb = tb // ROW_TILE
    cb = ctx // ROW_TILE
    return pl.pallas_call(
        _final_norm_kernel,
        grid=(batch, seq // ROW_TILE),
        in_specs=[pl.BlockSpec((ROW_TILE, d), lambda b, i: (b * nrb + cb + i, 0)),
                  pl.BlockSpec((1, d), lambda b, i: (0, 0))],
        out_specs=pl.BlockSpec((1, ROW_TILE, d), lambda b, i: (b, i, 0)),
        out_shape=jax.ShapeDtypeStruct((batch, seq, d), F32),
        compiler_params=_params(("parallel", "parallel")),
    )(x, g)


def kernel(x, c, ctx, c_ctx, ada_w, ada_b, norm_mix, norm_ffn, w_in, b_gate, sgu_norm, sgu_w, sgu_b,
           qk_conv_w, mlstm_norm, pool_w, pool_scale, w_out, peer_wq, peer_keys, peer_u, peer_v,
           norm_final):
    batch, seq, d = x.shape
    ctx_len = ctx.shape[1]
    depth = ada_w.shape[0]
    tb = ctx_len + seq
    n = batch * tb
    assert d == D_MODEL and w_in.shape[2] == OFF_G + N_GATES
    assert ctx_len % ROW_TILE == 0 and seq % ROW_TILE == 0 and seq % GRID_W == 0
    assert n % GATHER_TOKENS == 0 and ROW_TILE % GATHER_TOKENS == 0
    blocks_per_batch = tb // ROW_TILE
    ctx_blocks = ctx_len // ROW_TILE
    nchunks = tb // CHUNK

    def mod_row(i):
        return jnp.where(i % blocks_per_batch < ctx_blocks, batch, i // blocks_per_batch)

    groups_per_block = ROW_TILE // GATHER_TOKENS
    lat_blocks = seq // ROW_TILE

    def mod_row_group(i):
        return mod_row(i // groups_per_block)

    def lat_blk(j):
        return j + ctx_blocks * (j // lat_blocks + 1)

    cond_rows = -(-(batch + 1) // 8) * 8
    cond = jnp.zeros((cond_rows, d), F32).at[:batch].set(c).at[batch].set(c_ctx)
    mods = ada_modulation_all(cond, ada_w, ada_b).reshape(depth, cond_rows, 6, d)

    xs = jnp.concatenate([ctx, x], axis=1).reshape(n, d)
    lane_pad = LANE - N_GATES
    for l in range(depth):
        mod = mods[l]
        w_main = w_in[l, :, :OFF_G].astype(BF16)
        w_gate = jnp.pad(w_in[l, :, OFF_G:], ((0, 0), (0, lane_pad))).astype(BF16)
        p, graw = in_proj(xs, mod, norm_mix[l][None], w_main, w_gate, mod_row)

        y_a = sgu_mixer(p, sgu_norm[l][None], sgu_w[l].astype(BF16), sgu_b[l].T)

        gp = gate_prep(graw[0], jnp.pad(b_gate[l], (0, lane_pad))[None])
        gp = gp[:, :N_GATES].reshape(n, 2, 2, MLSTM_HEADS)
        gcol = gp.transpose(3, 0, 1, 2).reshape(MLSTM_HEADS, n, 4)
        grow = gcol.reshape(MLSTM_HEADS, n // CHUNK, CHUNK, 4).transpose(0, 1, 3, 2)
        qk = qk_conv(p, qk_conv_w[l], blocks_per_batch, ctx_blocks)
        y_b = mlstm_mixer(p, qk, gcol, grow, mlstm_norm[l][None], batch, tb, ctx_len // CHUNK)

        y_c = pool_mixer(p, pool_w[l].astype(BF16), pool_scale[l].reshape(POOL_GROUPS, 1, POOL_GC),
                         batch, tb, ctx_len, seq)

        xn = out_proj(xs, y_a, y_b, y_c, mod, w_out[l].astype(BF16), mod_row)

        if l == depth - 1:
            nblocks, blk = batch * lat_blocks, lat_blk

            def grp(g):
                return g + ctx_blocks * groups_per_block * (g // (lat_blocks * groups_per_block) + 1)

            def grp_mod_row(g):
                return g // (lat_blocks * groups_per_block)
        else:
            nblocks, blk = n // ROW_TILE, lambda j: j
            grp_mod_row = mod_row_group

            def grp(g):
                return g
        hf, e, gates = peer_route(xn, mod, norm_ffn[l][None], peer_wq[l].astype(BF16),
                                  peer_keys[l].astype(BF16), mod_row, nblocks, blk)
        ids = e.reshape(PEER_PAIRS, n).T.reshape(n // GATHER_TOKENS, 1, GATHER_TOKENS * PEER_PAIRS)
        gates_t = gates.reshape(PEER_PAIRS, n // GATHER_TOKENS, GATHER_TOKENS).transpose(1, 0, 2)
        xs = peer_gather(ids, hf, gates_t, xn, mod, pack_expert_table(peer_u, peer_v, l), grp_mod_row,
                         nblocks * groups_per_block, grp)

    return final_norm(xs, norm_final[None], batch, seq, ctx_len)
```
